```python
import math
import jax, jax.numpy as jnp
from jax import lax
import numpy as np

D_MODEL = 1024
BATCH = 1
SEQ = 16384
DEPTH = 2

GRID_W = 64
CTX_LEN = 256
HEAD_DIM = 64
NORM_EPS = 1e-6
A_Q_HEADS = 8
A_KV_HEADS = 2
A_WIDTH = A_Q_HEADS * HEAD_DIM
A_KV_WIDTH = A_KV_HEADS * HEAD_DIM
ROPE_THETA = 10000.0
Q_BLOCK = 128
B_HEADS = 4
B_WIDTH = B_HEADS * HEAD_DIM
B_CONV = 3
B_CHUNK = 64
C_WIDTH = 256
C_GROUP = 16
C_GROUPS = C_WIDTH // C_GROUP
C_STATE = 64
MIX_WIDTH = A_WIDTH + B_WIDTH + C_WIDTH
DEEPNORM_ALPHA = (2 * DEPTH) ** 0.25
DEEPNORM_BETA = (8 * DEPTH) ** -0.25
_SPLIT_WIDTHS = (A_WIDTH, A_KV_WIDTH, A_KV_WIDTH, A_WIDTH,
                 B_WIDTH, B_WIDTH, B_WIDTH, B_WIDTH, 2 * B_HEADS, 2 * B_HEADS,
                 C_WIDTH, C_WIDTH)
IN_WIDTH = sum(_SPLIT_WIDTHS)
SPLIT_POINTS = tuple(int(v) for v in np.cumsum(_SPLIT_WIDTHS)[:-1])

kernel_name = "hybrid_gqa_gdn_s5_prefix_ctx"

F32 = jnp.float32


def layer_norm(x):
    xf = x.astype(F32)
    mu = xf.mean(-1, keepdims=True)
    var = jnp.mean(jnp.square(xf - mu), -1, keepdims=True)
    return ((xf - mu) * lax.rsqrt(var + NORM_EPS)).astype(x.dtype)


def rms_norm(x, g):
    xf = x.astype(F32)
    y = xf * lax.rsqrt(jnp.mean(xf * xf, -1, keepdims=True) + NORM_EPS)
    return (y * g.astype(F32)).astype(x.dtype)


def l2norm(x):
    xf = x.astype(F32)
    return xf * lax.rsqrt(jnp.sum(xf * xf, -1, keepdims=True) + NORM_EPS)


def split_heads(z, n_heads):
    return z.reshape(*z.shape[:-1], n_heads, HEAD_DIM)


def flip_if(z, rev):
    return jnp.flip(z, 1) if rev else z


def axial_rope(n_tok):
    rows = n_tok // GRID_W
    row = jnp.repeat(jnp.arange(rows, dtype=F32), GRID_W)
    col = jnp.tile(jnp.arange(GRID_W, dtype=F32), rows)
    axis_dim = HEAD_DIM // 2
    inv = ROPE_THETA ** (-jnp.arange(0, axis_dim, 2, dtype=F32) / axis_dim)
    ar = row[:, None] * inv
    ac = col[:, None] * inv
    ang = jnp.concatenate([ar, ar, ac, ac], -1)
    return jnp.cos(ang), jnp.sin(ang)


def rotate_half(z):
    z1, z2 = jnp.split(z, 2, -1)
    return jnp.concatenate([-z2, z1], -1)


def apply_axial_rope(z, cos, sin):
    zf = z.astype(F32)
    zr, zc = jnp.split(zf, 2, -1)
    rot = jnp.concatenate([rotate_half(zr), rotate_half(zc)], -1)
    return (zf * cos[None, :, None, :] + rot * sin[None, :, None, :]).astype(z.dtype)


def attend(q, k, v):
    s = jnp.einsum('bqhgd,bkhd->bhgqk', q, k).astype(F32) * (HEAD_DIM ** -0.5)
    p = jax.nn.softmax(s, axis=-1).astype(v.dtype)
    return jnp.einsum('bhgqk,bkhd->bqhgd', p, v)


def attention_branch(lat, ctx, q_gain, k_gain, cos, sin, with_ctx_out):
    q_l_raw, k_l_raw, v_l_raw, gate_l = lat
    q_c_raw, k_c_raw, v_c_raw, gate_c = ctx
    groups = A_Q_HEADS // A_KV_HEADS
    k_l = apply_axial_rope(rms_norm(split_heads(k_l_raw, A_KV_HEADS), k_gain), cos, sin)
    q_l = apply_axial_rope(rms_norm(split_heads(q_l_raw, A_Q_HEADS), q_gain), cos, sin)
    k_c = rms_norm(split_heads(k_c_raw, A_KV_HEADS), k_gain)
    v_l = split_heads(v_l_raw, A_KV_HEADS)
    v_c = split_heads(v_c_raw, A_KV_HEADS)
    k_all = jnp.concatenate([k_l, k_c], 1)
    v_all = jnp.concatenate([v_l, v_c], 1)
    B, L = q_l.shape[:2]
    qb = q_l.reshape(B, L // Q_BLOCK, Q_BLOCK, A_KV_HEADS, groups, HEAD_DIM).swapaxes(0, 1)
    o_l = lax.map(lambda blk: attend(blk, k_all, v_all), qb)
    o_l = o_l.swapaxes(0, 1).reshape(B, L, A_WIDTH) * jax.nn.silu(gate_l)
    o_c = None
    if with_ctx_out:
        q_c = rms_norm(split_heads(q_c_raw, A_Q_HEADS), q_gain)
        Lc = q_c.shape[1]
        o_c = attend(q_c.reshape(B, Lc, A_KV_HEADS, groups, HEAD_DIM), k_c, v_c)
        o_c = o_c.reshape(B, Lc, A_WIDTH) * jax.nn.silu(gate_c)
    return o_l, o_c


def short_conv(z, w):
    K = w.shape[0]
    pad = K // 2
    L = z.shape[1]
    zp = jnp.pad(z, ((0, 0), (pad, pad), (0, 0)))
    return sum(zp[:, i:i + L] * w[i] for i in range(K))


def gated_delta_chunked(q, k, v, beta, g, s0):
    B, L, H, dk = k.shape
    dv = v.shape[-1]
    C = B_CHUNK
    n = L // C
    to_c = lambda z: z.reshape(B, n, C, H, z.shape[-1]).transpose(1, 0, 3, 2, 4)
    to_cs = lambda z: z.reshape(B, n, C, H).transpose(1, 0, 3, 2)
    qc, kc, vc = to_c(q), to_c(k), to_c(v)
    bc, gcum = to_cs(beta), jnp.cumsum(to_cs(g), -1)
    tri_incl = jnp.tril(jnp.ones((C, C), bool))
    tri_strict = jnp.tril(jnp.ones((C, C), bool), -1)
    decay = jnp.exp(jnp.where(tri_incl, gcum[..., :, None] - gcum[..., None, :], -jnp.inf))
    kb = kc * bc[..., None]
    lower = jnp.where(tri_strict, jnp.einsum('nbhid,nbhjd->nbhij', kb, kc) * decay, 0.0)
    eye = jnp.eye(C, dtype=F32)
    rhs = jnp.concatenate([vc * bc[..., None], kb * jnp.exp(gcum)[..., None]], -1)
    sol = lax.linalg.triangular_solve(lower + eye, rhs, left_side=True, lower=True,
                                      unit_diagonal=True)
    w_val, k_cum = sol[..., :dv], sol[..., dv:]
    attn = jnp.einsum('nbhid,nbhjd->nbhij', qc, kc) * decay
    q_dec = qc * jnp.exp(gcum)[..., None]
    k_dec = kc * jnp.exp(gcum[..., -1:] - gcum)[..., None]
    g_last = jnp.exp(gcum[..., -1])

    def step(S, xs):
        w_c, kcum_c, attn_c, qd_c, kd_c, gl_c = xs
        u = w_c - jnp.einsum('bhcd,bhde->bhce', kcum_c, S)
        o = jnp.einsum('bhcd,bhde->bhce', qd_c, S) + jnp.einsum('bhij,bhje->bhie', attn_c, u)
        S = S * gl_c[..., None, None] + jnp.einsum('bhcd,bhce->bhde', kd_c, u)
        return S, o

    S, o = lax.scan(step, s0, (w_val, k_cum, attn, q_dec, k_dec, g_last))
    o = o.transpose(1, 0, 3, 2, 4).reshape(B, L, H, dv)
    return o, S


def deltanet_prep(q, k, v, beta_raw, a_raw, conv_w, A_log, dt_bias):
    B, L = q.shape[:2]
    qkv = jax.nn.silu(short_conv(jnp.concatenate([q, k, v], -1), conv_w))
    q, k, v = jnp.split(qkv, 3, -1)
    q = l2norm(split_heads(q, B_HEADS)) * (HEAD_DIM ** -0.5)
    k = l2norm(split_heads(k, B_HEADS))
    v = split_heads(v, B_HEADS).astype(F32)
    beta = jax.nn.sigmoid(beta_raw.astype(F32)).reshape(B, L, 2, B_HEADS)
    g = -jnp.exp(A_log.astype(F32)) * jax.nn.softplus(
        a_raw.astype(F32).reshape(B, L, 2, B_HEADS) + dt_bias.astype(F32))
    return q, k, v, beta, g


def deltanet_branch(lat, ctx, conv_w, A_log, dt_bias, out_gain, with_ctx_out):
    ql, kl, vl, bl, gl = deltanet_prep(lat[0], lat[1], lat[2], lat[4], lat[5], conv_w, A_log, dt_bias)
    qc, kc, vc, bc, gc = deltanet_prep(ctx[0], ctx[1], ctx[2], ctx[4], ctx[5], conv_w, A_log, dt_bias)
    B, L = ql.shape[:2]
    s0 = jnp.zeros((B, B_HEADS, HEAD_DIM, HEAD_DIM), F32)
    o_l = jnp.zeros_like(vl)
    o_c = jnp.zeros_like(vc)
    for d, rev in ((0, False), (1, True)):
        oc_d, s_ctx = gated_delta_chunked(flip_if(qc, rev), flip_if(kc, rev), flip_if(vc, rev),
                                          flip_if(bc[:, :, d], rev), flip_if(gc[:, :, d], rev), s0)
        ol_d, _ = gated_delta_chunked(flip_if(ql, rev), flip_if(kl, rev), flip_if(vl, rev),
                                      flip_if(bl[:, :, d], rev), flip_if(gl[:, :, d], rev), s_ctx)
        o_l = o_l + flip_if(ol_d, rev)
        if with_ctx_out:
            o_c = o_c + flip_if(oc_d, rev)
    gate_l, gate_c = lat[3], ctx[3]
    out_l = rms_norm(o_l, out_gain).reshape(B, L, B_WIDTH).astype(gate_l.dtype) * jax.nn.silu(gate_l)
    out_c = None
    if with_ctx_out:
        out_c = rms_norm(o_c, out_gain).reshape(B, -1, B_WIDTH).astype(gate_c.dtype) * jax.nn.silu(gate_c)
    return out_l, out_c


def s5_discretise(A_re, A_im, log_dt, B_re, B_im):
    lam = lax.complex(A_re.astype(F32), A_im.astype(F32))
    dt = jnp.exp(log_dt.astype(F32))[:, None]
    a_bar = jnp.exp(lam * dt)
    b_bar = ((a_bar - 1.0) / lam)[..., None] * lax.complex(B_re.astype(F32), B_im.astype(F32))
    return a_bar, b_bar


def s5_states(u, a_bar, b_bar, h0):
    bu = jnp.einsum('blgc,gpc->blgp', u.astype(jnp.complex64), b_bar)
    bu = bu.at[:, 0].add(a_bar * h0)
    a = jnp.broadcast_to(a_bar, bu.shape)

    def combine(e1, e2):
        a1, b1 = e1
        a2, b2 = e2
        return a1 * a2, a2 * b1 + b2

    _, h = lax.associative_scan(combine, (a, bu), axis=1)
    return h


def s5_readout(h, c_mat):
    return jnp.einsum('blgp,gcp->blgc', h, c_mat).real


def s5_glu(y, glu_w, glu_b):
    z = jax.nn.gelu(y)
    return z * jax.nn.sigmoid(z @ glu_w.astype(F32) + glu_b.astype(F32))


def s5_branch(u_l_raw, gate_l, u_c_raw, gate_c, A_re, A_im, log_dt, B_re, B_im, C_re, C_im,
              D, glu_w, glu_b, with_ctx_out):
    B, L = u_l_raw.shape[:2]
    Lc = u_c_raw.shape[1]
    ul = u_l_raw.astype(F32).reshape(B, L, C_GROUPS, C_GROUP)
    uc = u_c_raw.astype(F32).reshape(B, Lc, C_GROUPS, C_GROUP)
    d_skip = D.astype(F32).reshape(C_GROUPS, C_GROUP)
    y_l = ul * d_skip
    y_c = uc * d_skip
    h0 = jnp.zeros((B, C_GROUPS, C_STATE), jnp.complex64)
    for d, rev in ((0, False), (1, True)):
        a_bar, b_bar = s5_discretise(A_re[d], A_im[d], log_dt[d], B_re[d], B_im[d])
        c_mat = lax.complex(C_re[d].astype(F32), C_im[d].astype(F32))
        h_c = s5_states(flip_if(uc, rev), a_bar, b_bar, h0)
        h_l = s5_states(flip_if(ul, rev), a_bar, b_bar, h_c[:, -1])
        y_l = y_l + flip_if(s5_readout(h_l, c_mat), rev)
        if with_ctx_out:
            y_c = y_c + flip_if(s5_readout(h_c, c_mat), rev)
    out_l = s5_glu(y_l.reshape(B, L, C_WIDTH), glu_w, glu_b).astype(gate_l.dtype) * jax.nn.silu(gate_l)
    out_c = None
    if with_ctx_out:
        out_c = s5_glu(y_c.reshape(B, Lc, C_WIDTH), glu_w, glu_b).astype(gate_c.dtype) * jax.nn.silu(gate_c)
    return out_l, out_c


def setup_inputs(seed: int = 0) -> dict:
    key = jax.random.key(seed)
    ks = jax.random.split(key, 32)
    nrm = lambda k, shape, s: jax.random.normal(k, shape, F32) * s
    dt_lo, dt_hi = math.log(1e-3), math.log(1e-1)
    x = nrm(ks[0], (BATCH, SEQ, D_MODEL), 1.0)
    c = nrm(ks[1], (BATCH, D_MODEL), 1.0)
    ctx = nrm(ks[2], (BATCH, CTX_LEN, D_MODEL), 1.0)
    c_ctx = nrm(ks[3], (D_MODEL,), 1.0)
    w_mod = nrm(ks[4], (DEPTH, D_MODEL, 3 * D_MODEL), D_MODEL ** -0.5)
    b_mod = nrm(ks[5], (DEPTH, 3 * D_MODEL), 0.02)
    w_in = nrm(ks[6], (DEPTH, D_MODEL, IN_WIDTH), D_MODEL ** -0.5)
    attn_q_gain = 1.0 + nrm(ks[7], (DEPTH, HEAD_DIM), 0.02)
    attn_k_gain = 1.0 + nrm(ks[8], (DEPTH, HEAD_DIM), 0.02)
    dn_conv_w = nrm(ks[9], (DEPTH, B_CONV, 3 * B_WIDTH), B_CONV ** -0.5)
    dn_A_log = jnp.log(jax.random.uniform(ks[10], (DEPTH, 2, B_HEADS), F32, 1.0, 16.0))
    dn_dt = jnp.exp(jax.random.uniform(ks[11], (DEPTH, 2, B_HEADS), F32, dt_lo, dt_hi))
    dn_dt_bias = dn_dt + jnp.log(-jnp.expm1(-dn_dt))
    dn_out_gain = 1.0 + nrm(ks[12], (DEPTH, HEAD_DIM), 0.02)
    n_idx = jnp.arange(C_STATE, dtype=F32)
    s5_A_re = -0.5 + nrm(ks[13], (DEPTH, 2, C_GROUPS, C_STATE), 0.01)
    s5_A_im = math.pi * n_idx + nrm(ks[14], (DEPTH, 2, C_GROUPS, C_STATE), 0.01)
    s5_log_dt = jax.random.uniform(ks[15], (DEPTH, 2, C_GROUPS), F32, dt_lo, dt_hi)
    s5_B_re = nrm(ks[16], (DEPTH, 2, C_GROUPS, C_STATE, C_GROUP), (2 * C_GROUP) ** -0.5)
    s5_B_im = nrm(ks[17], (DEPTH, 2, C_GROUPS, C_STATE, C_GROUP), (2 * C_GROUP) ** -0.5)
    s5_C_re = nrm(ks[18], (DEPTH, 2, C_GROUPS, C_GROUP, C_STATE), (2 * C_STATE) ** -0.5)
    s5_C_im = nrm(ks[19], (DEPTH, 2, C_GROUPS, C_GROUP, C_STATE), (2 * C_STATE) ** -0.5)
    s5_D = nrm(ks[20], (DEPTH, C_WIDTH), 1.0)
    glu_w = nrm(ks[21], (DEPTH, C_WIDTH, C_WIDTH), C_WIDTH ** -0.5)
    glu_b = nrm(ks[22], (DEPTH, C_WIDTH), 0.02)
    w_out = nrm(ks[23], (DEPTH, MIX_WIDTH, D_MODEL), MIX_WIDTH ** -0.5 * DEEPNORM_BETA)
    ln_g = 1.0 + nrm(ks[24], (DEPTH, D_MODEL), 0.02)
    ln_b = nrm(ks[25], (DEPTH, D_MODEL), 0.02)
    return {"x": x, "c": c, "ctx": ctx, "c_ctx": c_ctx, "w_mod": w_mod, "b_mod": b_mod,
            "w_in": w_in, "attn_q_gain": attn_q_gain, "attn_k_gain": attn_k_gain,
            "dn_conv_w": dn_conv_w, "dn_A_log": dn_A_log, "dn_dt_bias": dn_dt_bias,
            "dn_out_gain": dn_out_gain, "s5_A_re": s5_A_re, "s5_A_im": s5_A_im,
            "s5_log_dt": s5_log_dt, "s5_B_re": s5_B_re, "s5_B_im": s5_B_im,
            "s5_C_re": s5_C_re, "s5_C_im": s5_C_im, "s5_D": s5_D, "glu_w": glu_w,
            "glu_b": glu_b, "w_out": w_out, "ln_g": ln_g, "ln_b": ln_b}


def reference(x, c, ctx, c_ctx, w_mod, b_mod, w_in, attn_q_gain, attn_k_gain, dn_conv_w,
              dn_A_log, dn_dt_bias, dn_out_gain, s5_A_re, s5_A_im, s5_log_dt, s5_B_re, s5_B_im,
              s5_C_re, s5_C_im, s5_D, glu_w, glu_b, w_out, ln_g, ln_b):
    cos, sin = axial_rope(x.shape[1])
    for l in range(DEPTH):
        with_ctx_out = l < DEPTH - 1
        mod = jax.nn.silu(c) @ w_mod[l] + b_mod[l]
        shift, scale, gate = jnp.split(mod[:, None, :], 3, -1)
        mod_c = jax.nn.silu(c_ctx) @ w_mod[l] + b_mod[l]
        shift_c, scale_c, gate_c = jnp.split(mod_c, 3, -1)
        h = layer_norm(x) * (1 + scale) + shift
        hc = layer_norm(ctx) * (1 + scale_c) + shift_c
        zl = jnp.split(h @ w_in[l], SPLIT_POINTS, -1)
        zc = jnp.split(hc @ w_in[l], SPLIT_POINTS, -1)
        a_l, a_c = attention_branch(zl[0:4], zc[0:4], attn_q_gain[l], attn_k_gain[l],
                                    cos, sin, with_ctx_out)
        d_l, d_c = deltanet_branch(zl[4:10], zc[4:10], dn_conv_w[l], dn_A_log[l], dn_dt_bias[l],
                                   dn_out_gain[l], with_ctx_out)
        s_l, s_c = s5_branch(zl[10], zl[11], zc[10], zc[11], s5_A_re[l], s5_A_im[l], s5_log_dt[l],
                             s5_B_re[l], s5_B_im[l], s5_C_re[l], s5_C_im[l], s5_D[l],
                             glu_w[l], glu_b[l], with_ctx_out)
        mix_l = jnp.concatenate([a_l.astype(x.dtype), d_l.astype(x.dtype), s_l.astype(x.dtype)], -1)
        x_new = layer_norm(DEEPNORM_ALPHA * x + gate * (mix_l @ w_out[l])) * ln_g[l] + ln_b[l]
        if with_ctx_out:
            mix_c = jnp.concatenate([a_c.astype(ctx.dtype), d_c.astype(ctx.dtype),
                                     s_c.astype(ctx.dtype)], -1)
            ctx = layer_norm(DEEPNORM_ALPHA * ctx + gate_c * (mix_c @ w_out[l])) * ln_g[l] + ln_b[l]
        x = x_new
    return x
```

```python
import functools
import math

import jax
import jax.numpy as jnp
from jax import lax
from jax.experimental import pallas as pl
from jax.experimental.pallas import tpu as pltpu

F32 = jnp.float32
BF16 = jnp.bfloat16
HIGHEST = lax.Precision.HIGHEST

GRID_W = 64
HEAD_DIM = 64
NORM_EPS = 1e-6
ROPE_THETA = 10000.0
A_Q_HEADS = 8
A_KV_HEADS = 2
A_GROUPS = A_Q_HEADS // A_KV_HEADS
A_WIDTH = A_Q_HEADS * HEAD_DIM
A_KV_WIDTH = A_KV_HEADS * HEAD_DIM
B_HEADS = 4
B_WIDTH = B_HEADS * HEAD_DIM
B_CHUNK = 64
C_WIDTH = 256
C_GROUP = 16
C_GROUPS = C_WIDTH // C_GROUP
C_STATE = 64
C_STATES = C_GROUPS * C_STATE
LANES = 128
SUBLANES = 8
ROW_TILE = 256
S5_STEPS = ROW_TILE // SUBLANES
VMEM_LIMIT = 48 * 1024 * 1024

DN_PREC = 1
S5_PREC = 3


def _cparams(*sem):
    return pltpu.CompilerParams(dimension_semantics=sem, vmem_limit_bytes=VMEM_LIMIT)


def _split(a):
    hi = a.astype(BF16)
    lo = (a - hi.astype(F32)).astype(BF16)
    return hi, lo


def _dg(a, b, dims):
    return lax.dot_general(a, b, (dims, ((), ())), preferred_element_type=F32)


def _mm(a, b, prec, dims=((1,), (0,))):
    if prec == 6:
        return lax.dot_general(a, b, (dims, ((), ())), precision=HIGHEST,
                               preferred_element_type=F32)
    if prec == 1:
        return _dg(a.astype(BF16), b.astype(BF16), dims)
    ah, al = _split(a)
    bh, bl = _split(b)
    return _dg(ah, bh, dims) + (_dg(ah, bl, dims) + _dg(al, bh, dims))


_NT = ((1,), (1,))
_TN = ((0,), (0,))


def _silu(x):
    return x * jax.nn.sigmoid(x)


def _mod_kernel(cv_ref, w_ref, b_ref, o_ref):
    sv = _silu(cv_ref[...])
    o_ref[0] = _mm(sv, w_ref[0], 6) + b_ref[0]


def _modulation(cv, w_mod, b_mod):
    depth, d, d3 = w_mod.shape
    nblk = d3 // d
    return pl.pallas_call(
        _mod_kernel,
        grid=(depth, nblk),
        in_specs=[pl.BlockSpec((SUBLANES, d), lambda l, n: (0, 0)),
                  pl.BlockSpec((1, d, d), lambda l, n: (l, 0, n)),
                  pl.BlockSpec((1, 1, d), lambda l, n: (l, 0, n))],
        out_specs=pl.BlockSpec((1, SUBLANES, d), lambda l, n: (l, 0, n)),
        out_shape=jax.ShapeDtypeStruct((depth, SUBLANES, d3), F32),
        compiler_params=_cparams("arbitrary", "arbitrary"),
        name="modulation",
    )(cv, w_mod, b_mod.reshape(depth, 1, d3))


def _layer_norm(x):
    mu = jnp.mean(x, axis=-1, keepdims=True)
    xc = x - mu
    var = jnp.mean(xc * xc, axis=-1, keepdims=True)
    return xc * lax.rsqrt(var + NORM_EPS)


def _mod_row(mod_ref, is_ctx):
    mod = mod_ref[...]
    return jnp.where(is_ctx, mod[1:2], mod[0:1])


_IN_SPLITS = (("qa", A_WIDTH), ("ka", A_KV_WIDTH), ("va", A_KV_WIDTH), ("ga", A_WIDTH),
              ("qkvb", 3 * B_WIDTH), ("gb", B_WIDTH), ("bd", LANES), ("uc", C_WIDTH),
              ("gc", C_WIDTH))
_IN_PAD_WIDTH = sum(w for _, w in _IN_SPLITS)


def _inproj_kernel(x_ref, mod_ref, w_ref, *out_refs, n_ctx_tiles, d_model):
    row = _mod_row(mod_ref, pl.program_id(0) < n_ctx_tiles)
    shift, scale = row[:, :d_model], row[:, d_model:2 * d_model]
    h = _layer_norm(x_ref[...]) * (1.0 + scale) + shift
    z = jnp.dot(h.astype(BF16), w_ref[...], preferred_element_type=F32)
    off = 0
    for ref, (_, w) in zip(out_refs, _IN_SPLITS):
        ref[...] = z[:, off:off + w]
        off += w


def _pad_w_in(w):
    d = w.shape[0]
    n_main = 2 * A_WIDTH + 2 * A_KV_WIDTH + 4 * B_WIDTH
    n_small = 4 * B_HEADS
    return jnp.concatenate(
        [w[:, :n_main], w[:, n_main:n_main + n_small], jnp.zeros((d, LANES - n_small), w.dtype),
         w[:, n_main + n_small:]], axis=1).astype(BF16)


def _inproj(xall, mod_l, w_pad, n_ctx_tiles):
    n, d = xall.shape
    tm = ROW_TILE
    return pl.pallas_call(
        functools.partial(_inproj_kernel, n_ctx_tiles=n_ctx_tiles, d_model=d),
        grid=(n // tm,),
        in_specs=[pl.BlockSpec((tm, d), lambda i: (i, 0)),
                  pl.BlockSpec(mod_l.shape, lambda i: (0, 0)),
                  pl.BlockSpec(w_pad.shape, lambda i: (0, 0))],
        out_specs=[pl.BlockSpec((tm, w), lambda i: (i, 0)) for _, w in _IN_SPLITS],
        out_shape=[jax.ShapeDtypeStruct((n, w), F32) for _, w in _IN_SPLITS],
        compiler_params=_cparams("arbitrary"),
        name="inproj",
    )(xall, mod_l, w_pad)


def _block_diag_const(width, value):
    r = jnp.arange(width) // HEAD_DIM
    return jnp.where(r[:, None] == r[None, :], value, 0.0).astype(F32)


def _attn_prep_kernel(q_ref, k_ref, v_ref, cos_ref, sa_ref, sb_ref, gq_ref, gk_ref, bd_ref,
                      qo_ref, ko_ref, vo_ref):
    cos, sa, sb, bd = cos_ref[...], sa_ref[...], sb_ref[...], bd_ref[...]

    def norm_rope(y, gain):
        ms = _mm(y * y, bd, 6)
        yn = y * lax.rsqrt(ms + NORM_EPS) * gain
        return (yn * cos + pltpu.roll(yn, LANES - HEAD_DIM // 4, 1) * sa
                + pltpu.roll(yn, HEAD_DIM // 4, 1) * sb)

    gq, gk = gq_ref[...], gk_ref[...]
    for j in range(A_Q_HEADS // 2):
        y = norm_rope(q_ref[:, j * LANES:(j + 1) * LANES], gq) * (HEAD_DIM ** -0.5)
        yb = y.astype(BF16)
        qo_ref[2 * j] = yb[:, :HEAD_DIM]
        qo_ref[2 * j + 1] = yb[:, HEAD_DIM:]
    yk = norm_rope(k_ref[...], gk).astype(BF16)
    v = v_ref[...].astype(BF16)
    ones = jnp.ones((v.shape[0], HEAD_DIM), BF16)
    for h in range(A_KV_HEADS):
        ko_ref[h] = yk[:, h * HEAD_DIM:(h + 1) * HEAD_DIM]
        vo_ref[h] = jnp.concatenate([v[:, h * HEAD_DIM:(h + 1) * HEAD_DIM], ones], axis=1)


def _attn_prep(qa, ka, va, cos, sa, sb, gq, gk):
    n = qa.shape[0]
    tm = ROW_TILE
    bd = _block_diag_const(LANES, 1.0 / HEAD_DIM)
    row = lambda w: pl.BlockSpec((tm, w), lambda i: (i, 0))
    full = lambda a: pl.BlockSpec(a.shape, lambda i: (0,) * a.ndim)
    return pl.pallas_call(
        _attn_prep_kernel,
        grid=(n // tm,),
        in_specs=[row(A_WIDTH), row(A_KV_WIDTH), row(A_KV_WIDTH), row(LANES), row(LANES),
                  row(LANES), full(gq), full(gk), full(bd)],
        out_specs=[pl.BlockSpec((A_Q_HEADS, tm, HEAD_DIM), lambda i: (0, i, 0)),
                   pl.BlockSpec((A_KV_HEADS, tm, HEAD_DIM), lambda i: (0, i, 0)),
                   pl.BlockSpec((A_KV_HEADS, tm, LANES), lambda i: (0, i, 0))],
        out_shape=[jax.ShapeDtypeStruct((A_Q_HEADS, n, HEAD_DIM), BF16),
                   jax.ShapeDtypeStruct((A_KV_HEADS, n, HEAD_DIM), BF16),
                   jax.ShapeDtypeStruct((A_KV_HEADS, n, LANES), BF16)],
        compiler_params=_cparams("arbitrary"),
        name="attn_prep",
    )(qa, ka, va, cos, sa, sb, gq, gk, bd)


def _flash_kernel(q_ref, k_ref, v_ref, g_ref, o_ref, m_sc, acc_sc, *, n_ctx, n_ctx_tiles, tq):
    i, j = pl.program_id(0), pl.program_id(2)
    nkv = pl.num_programs(2)
    is_ctx = i < n_ctx_tiles

    @pl.when(j == 0)
    def _():
        m_sc[...] = jnp.full(m_sc.shape, -jnp.inf, F32)
        acc_sc[...] = jnp.zeros(acc_sc.shape, F32)

    def step(masked):
        q = q_ref[0].reshape(A_GROUPS * tq, HEAD_DIM)
        s = _dg(q, k_ref[0], _NT)
        if masked:
            col = lax.broadcasted_iota(jnp.int32, (1, s.shape[1]), 1)
            s = jnp.where(col < n_ctx, s, -jnp.inf)
        m_old = m_sc[...]
        m_new = jnp.maximum(m_old, jnp.max(s, axis=1, keepdims=True))
        p = jnp.exp(s - m_new).astype(BF16)
        acc_sc[...] = jnp.exp(m_old - m_new) * acc_sc[...] + jnp.dot(
            p, v_ref[0], preferred_element_type=F32)
        m_sc[...] = m_new

    pl.when(jnp.logical_not(is_ctx))(lambda: step(False))
    pl.when(jnp.logical_and(is_ctx, j == 0))(lambda: step(True))

    @pl.when(j == nkv - 1)
    def _():
        acc = acc_sc[...]
        o = acc[:, :HEAD_DIM] / acc[:, HEAD_DIM:]
        gate = _silu(g_ref[...])
        for g in range(A_GROUPS):
            sl = slice(g * HEAD_DIM, (g + 1) * HEAD_DIM)
            o_ref[:, sl] = o[g * tq:(g + 1) * tq] * gate[:, sl]


def _kv_tile(n, n_ctx):
    for tk in (1280, 1024, 640, 512, 256):
        if n % tk == 0 and tk >= n_ctx:
            return tk
    raise ValueError(f"no key tile for {n} rows")


def _flash(q4, k, v, ga, n_ctx):
    n = k.shape[1]
    tq = ROW_TILE
    tk = _kv_tile(n, n_ctx)
    gw = A_GROUPS * HEAD_DIM
    return pl.pallas_call(
        functools.partial(_flash_kernel, n_ctx=n_ctx, n_ctx_tiles=n_ctx // tq, tq=tq),
        grid=(n // tq, A_KV_HEADS, n // tk),
        in_specs=[pl.BlockSpec((1, A_GROUPS, tq, HEAD_DIM), lambda i, h, j: (h, 0, i, 0)),
                  pl.BlockSpec((1, tk, HEAD_DIM), lambda i, h, j: (h, j, 0)),
                  pl.BlockSpec((1, tk, LANES), lambda i, h, j: (h, j, 0)),
                  pl.BlockSpec((tq, gw), lambda i, h, j: (i, h))],
        out_specs=pl.BlockSpec((tq, gw), lambda i, h, j: (i, h)),
        out_shape=jax.ShapeDtypeStruct((n, A_WIDTH), F32),
        scratch_shapes=[pltpu.VMEM((A_GROUPS * tq, 1), F32),
                        pltpu.VMEM((A_GROUPS * tq, LANES), F32)],
        compiler_params=_cparams("arbitrary", "arbitrary", "arbitrary"),
        name="flash_attn",
    )(q4, k, v, ga)


def _dn_prep_kernel(z_ref, zp_ref, zn_ref, r_ref, w_ref, par_ref, bd_ref,
                    q_o, k_o, v_o, bg_o, *, seq_starts, seq_ends):
    i = pl.program_id(0)
    z = z_ref[...]
    tm = z.shape[0]
    is_start = functools.reduce(jnp.logical_or, [i == s for s in seq_starts])
    is_end = functools.reduce(jnp.logical_or, [i == s for s in seq_ends])
    prev_row = jnp.where(is_start, 0.0, zp_ref[SUBLANES - 1:SUBLANES, :])
    next_row = jnp.where(is_end, 0.0, zn_ref[0:1, :])
    row = lax.broadcasted_iota(jnp.int32, (tm, 1), 0)
    z_prev = jnp.where(row == 0, prev_row, pltpu.roll(z, 1, 0))
    z_next = jnp.where(row == tm - 1, next_row, pltpu.roll(z, tm - 1, 0))
    w = w_ref[...]
    y = _silu(z_prev * w[0:1] + z * w[1:2] + z_next * w[2:3])
    bd = bd_ref[...]

    def l2norm(t):
        return t * lax.rsqrt(_mm(t * t, bd, 6) + NORM_EPS)

    q_o[...] = l2norm(y[:, :B_WIDTH]) * (HEAD_DIM ** -0.5)
    k_o[...] = l2norm(y[:, B_WIDTH:2 * B_WIDTH])
    v_o[...] = y[:, 2 * B_WIDTH:]
    r = r_ref[...]
    par = par_ref[...]
    xa = r + par[1:2]
    softplus = jnp.maximum(xa, 0.0) + jnp.log(1.0 + jnp.exp(-jnp.abs(xa)))
    g = -jnp.exp(par[0:1]) * softplus
    lane = lax.broadcasted_iota(jnp.int32, (1, LANES), 1)
    bg_o[...] = jnp.where(lane < 2 * B_HEADS, jax.nn.sigmoid(r), g)


def _dn_prep(qkvb, bdraw, conv_w, a_log, dt_bias, n_ctx):
    n = qkvb.shape[0]
    tm = ROW_TILE
    nt, nct = n // tm, n_ctx // tm
    r8 = tm // SUBLANES
    par = jnp.zeros((2, LANES), F32)
    par = par.at[0, 2 * B_HEADS:4 * B_HEADS].set(a_log.reshape(-1))
    par = par.at[1, 2 * B_HEADS:4 * B_HEADS].set(dt_bias.reshape(-1))
    bd = _block_diag_const(B_WIDTH, 1.0)
    w3 = 3 * B_WIDTH
    return pl.pallas_call(
        functools.partial(_dn_prep_kernel, seq_starts=(0, nct), seq_ends=(nct - 1, nt - 1)),
        grid=(nt,),
        in_specs=[pl.BlockSpec((tm, w3), lambda i: (i, 0)),
                  pl.BlockSpec((SUBLANES, w3), lambda i: (jnp.maximum(i * r8 - 1, 0), 0)),
                  pl.BlockSpec((SUBLANES, w3),
                               lambda i: (jnp.minimum((i + 1) * r8, n // SUBLANES - 1), 0)),
                  pl.BlockSpec((tm, LANES), lambda i: (i, 0)),
                  pl.BlockSpec(conv_w.shape, lambda i: (0, 0)),
                  pl.BlockSpec(par.shape, lambda i: (0, 0)),
                  pl.BlockSpec(bd.shape, lambda i: (0, 0))],
        out_specs=[pl.BlockSpec((tm, B_WIDTH), lambda i: (i, 0))] * 3
        + [pl.BlockSpec((tm, LANES), lambda i: (i, 0))],
        out_shape=[jax.ShapeDtypeStruct((n, B_WIDTH), F32)] * 3
        + [jax.ShapeDtypeStruct((n, LANES), F32)],
        compiler_params=_cparams("arbitrary"),
        name="dn_prep",
    )(qkvb, qkvb, qkvb, bdraw, conv_w, par, bd)


def _dn_scan_kernel(qf, kf, vf, bgf, qb, kb, vb, bgb, of_ref, ob_ref, s_ref, *, prec):
    @pl.when(pl.program_id(0) == 0)
    def _():
        s_ref[...] = jnp.zeros(s_ref.shape, F32)

    c = B_CHUNK
    row = lax.broadcasted_iota(jnp.int32, (c, c), 0)
    col = lax.broadcasted_iota(jnp.int32, (c, c), 1)
    eye = (row == col).astype(F32)
    level_masks = [(row // 2) == (col // 2)]
    bs = 2
    while bs < c:
        level_masks.append(jnp.logical_and((row // (2 * bs)) == (col // (2 * bs)),
                                           (row // bs) != (col // bs)))
        bs *= 2
    dirs =((qf, kf, vf, bgf, of_ref), (qb, kb, vb, bgb, ob_ref))
    for d, (q_ref, k_ref, v_ref, bg_ref, o_ref) in enumerate(dirs):
        incl = (col <= row) if d == 0 else (col >= row)
        strict = (col < row) if d == 0 else (col > row)
        last = c - 1 if d == 0 else 0
        bg = bg_ref[...]
        gcum = _mm(incl.astype(F32), bg, 6)
        gcum_t = gcum.T
        outs = []
        for h in range(B_HEADS):
            lb = d * B_HEADS + h
            lg = 2 * B_HEADS + lb
            sl = slice(h * HEAD_DIM, (h + 1) * HEAD_DIM)
            q, k, v = q_ref[:, sl], k_ref[:, sl], v_ref[:, sl]
            beta = bg[:, lb:lb + 1]
            gc = gcum[:, lg:lg + 1]
            g_last = gcum[last:last + 1, lg:lg + 1]
            decay = jnp.where(incl, jnp.exp(gc - gcum_t[lg:lg + 1, :]), 0.0)
            eg = jnp.exp(gc)
            kk = _mm(k, k, prec, _NT)
            qk = _mm(q, k, prec, _NT)
            a_mat = jnp.where(strict, kk * beta * decay, 0.0)
            t = eye - jnp.where(level_masks[0], a_mat, 0.0)
            for lm in level_masks[1:]:
                t = t - _mm(_mm(t, jnp.where(lm, a_mat, 0.0), prec), t, prec)
            sol = _mm(t, jnp.concatenate([v * beta, k * (beta * eg)], axis=1), prec)
            w_val, k_cum = sol[:, :HEAD_DIM], sol[:, HEAD_DIM:]
            s = s_ref[lb]
            u = w_val - _mm(k_cum, s, prec)
            outs.append(_mm(q * eg, s, prec) + _mm(qk * decay, u, prec))
            s_ref[lb] = s * jnp.exp(g_last) + _mm(k * jnp.exp(g_last - gc), u, prec, _TN)
        o_ref[...] = jnp.concatenate(outs, axis=1)


def _scan_order_maps(n_chunks, n_ctx_chunks):
    fwd = lambda j: (j, 0)
    bwd = lambda j: (jnp.where(j < n_ctx_chunks, n_ctx_chunks - 1 - j,
                               n_chunks - 1 - (j - n_ctx_chunks)), 0)
    return fwd, bwd


def _dn_scan(qn, kn, vn, bg, n_ctx):
    n = qn.shape[0]
    c = B_CHUNK
    fwd, bwd = _scan_order_maps(n // c, n_ctx // c)
    specs = lambda m: [pl.BlockSpec((c, B_WIDTH), m)] * 3 + [pl.BlockSpec((c, LANES), m)]
    return pl.pallas_call(
        functools.partial(_dn_scan_kernel, prec=DN_PREC),
        grid=(n // c,),
        in_specs=specs(fwd) + specs(bwd),
        out_specs=[pl.BlockSpec((c, B_WIDTH), fwd), pl.BlockSpec((c, B_WIDTH), bwd)],
        out_shape=[jax.ShapeDtypeStruct((n, B_WIDTH), F32)] * 2,
        scratch_shapes=[pltpu.VMEM((2 * B_HEADS, HEAD_DIM, HEAD_DIM), F32)],
        compiler_params=_cparams("arbitrary"),
        name="dn_scan",
    )(qn, kn, vn, bg, qn, kn, vn, bg)


def _s5_kernel(uf_ref, ub_ref, bre_ref, bim_ref, cre_ref, cimn_ref, a_ref, ap_ref,
               yf_ref, yb_ref, hre_sc, him_sc, carry_sc, *, prec):
    @pl.when(pl.program_id(0) == 0)
    def _():
        carry_sc[...] = jnp.zeros(carry_sc.shape, F32)

    steps, nsub = S5_STEPS, SUBLANES
    nslab = C_STATES // LANES
    lanes = lambda k: slice(k * LANES, (k + 1) * LANES)

    def put_all(ref, val):
        for k in range(nslab):
            ref[k] = val[:, lanes(k)]

    def get_all(ref):
        return jnp.concatenate([ref[k] for k in range(nslab)], axis=1)

    def put_rows(ref, rows, val):
        for k in range(nslab):
            ref[k, rows, :] = val[:, lanes(k)]

    def get_rows(ref, rows):
        return jnp.concatenate([ref[k, rows, :] for k in range(nslab)], axis=1)

    for d, (u_ref, y_ref) in enumerate(((uf_ref, yf_ref), (ub_ref, yb_ref))):
        u = u_ref[...]
        put_all(hre_sc, _mm(u, bre_ref[d], prec))
        put_all(him_sc, _mm(u, bim_ref[d], prec))
        a_re, a_im = a_ref[d, 0:1, :], a_ref[d, 1:2, :]

        def rows_at(jj, d=d):
            return pl.ds(jj if d == 0 else steps - 1 - jj, nsub, stride=steps)

        def local_scan(jj, carry, a_re=a_re, a_im=a_im, rows_at=rows_at):
            h_re, h_im = carry
            rows = rows_at(jj)
            n_re = a_re * h_re - a_im * h_im + get_rows(hre_sc, rows)
            n_im = a_re * h_im + a_im * h_re + get_rows(him_sc, rows)
            put_rows(hre_sc, rows, n_re)
            put_rows(him_sc, rows, n_im)
            return n_re, n_im

        zero = jnp.zeros((nsub, C_STATES), F32)
        end_re, end_im = lax.fori_loop(0, steps, local_scan, (zero, zero))
        as_re = ap_ref[d, 0, steps - 1:steps, :]
        as_im = ap_ref[d, 1, steps - 1:steps, :]
        c_re, c_im = carry_sc[2 * d:2 * d + 1, :], carry_sc[2 * d + 1:2 * d + 2, :]
        cin = [None] * nsub
        for s in (range(nsub) if d == 0 else reversed(range(nsub))):
            cin[s] = (c_re, c_im)
            c_re, c_im = (end_re[s:s + 1] + as_re * c_re - as_im * c_im,
                          end_im[s:s + 1] + as_re * c_im + as_im * c_re)
        carry_sc[2 * d:2 * d + 1, :] = c_re
        carry_sc[2 * d + 1:2 * d + 2, :] = c_im
        cin_re = jnp.concatenate([x[0] for x in cin], axis=0)
        cin_im = jnp.concatenate([x[1] for x in cin], axis=0)

        def fixup(jj, _, d=d, rows_at=rows_at, cin_re=cin_re, cin_im=cin_im):
            rows = rows_at(jj)
            p_re = ap_ref[d, 0, pl.ds(jj, 1), :]
            p_im = ap_ref[d, 1, pl.ds(jj, 1), :]
            put_rows(hre_sc, rows, get_rows(hre_sc, rows) + (p_re * cin_re - p_im * cin_im))
            put_rows(him_sc, rows, get_rows(him_sc, rows) + (p_re * cin_im + p_im * cin_re))
            return 0

        lax.fori_loop(0, steps, fixup, 0)
        y_ref[...] = (_mm(get_all(hre_sc), cre_ref[d], prec)
                      + _mm(get_all(him_sc), cimn_ref[d], prec))


def _s5_params(a_re, a_im, log_dt, b_re, b_im, c_re, c_im):
    dt = jnp.exp(log_dt.astype(F32))[..., None]
    lam_re, lam_im = a_re.astype(F32), a_im.astype(F32)
    k = jnp.arange(1, S5_STEPS + 1, dtype=F32)[None, :, None, None]
    mag = jnp.exp(lam_re[:, None] * dt[:, None] * k)
    ang = lam_im[:, None] * dt[:, None] * k
    ap = jnp.stack([mag * jnp.cos(ang), mag * jnp.sin(ang)], axis=1)
    ap = ap.reshape(2, 2, S5_STEPS, C_STATES)
    ab_re, ab_im = ap[:, 0, 0], ap[:, 1, 0]
    abar = jnp.stack([ab_re, ab_im], axis=1)
    x, y = ab_re.reshape(lam_re.shape) - 1.0, ab_im.reshape(lam_re.shape)
    den = lam_re * lam_re + lam_im * lam_im
    cf_re, cf_im = (x * lam_re + y * lam_im) / den, (y * lam_re - x * lam_im) / den
    bb_re = cf_re[..., None] * b_re - cf_im[..., None] * b_im
    bb_im = cf_re[..., None] * b_im + cf_im[..., None] * b_re
    eye = jnp.eye(C_GROUPS, dtype=F32)
    dense_b = lambda b: jnp.einsum('dgpc,gh->dgchp', b, eye).reshape(2, C_WIDTH, C_STATES)
    dense_c = lambda c: jnp.einsum('dgcp,gh->dgphc', c, eye).reshape(2, C_STATES, C_WIDTH)
    return (dense_b(bb_re), dense_b(bb_im), dense_c(c_re.astype(F32)),
            dense_c(-c_im.astype(F32)), abar, ap)


def _s5_scan(uc, params, n_ctx):
    n = uc.shape[0]
    ts = ROW_TILE
    bre, bim, cre, cimn, abar, ap = params
    fwd, bwd = _scan_order_maps(n // ts, n_ctx // ts)
    full = lambda a: pl.BlockSpec(a.shape, lambda j: (0,) * a.ndim)
    return pl.pallas_call(
        functools.partial(_s5_kernel, prec=S5_PREC),
        grid=(n // ts,),
        in_specs=[pl.BlockSpec((ts, C_WIDTH), fwd), pl.BlockSpec((ts, C_WIDTH), bwd),
                  full(bre), full(bim), full(cre), full(cimn), full(abar), full(ap)],
        out_specs=[pl.BlockSpec((ts, C_WIDTH), fwd), pl.BlockSpec((ts, C_WIDTH), bwd)],
        out_shape=[jax.ShapeDtypeStruct((n, C_WIDTH), F32)] * 2,
        scratch_shapes=[pltpu.VMEM((C_STATES // LANES, ts, LANES), F32),
                        pltpu.VMEM((C_STATES // LANES, ts, LANES), F32),
                        pltpu.VMEM((SUBLANES, C_STATES), F32)],
        compiler_params=_cparams("arbitrary"),
        name="s5_scan",
    )(uc, uc, bre, bim, cre, cimn, abar, ap)


def _gelu_tanh(x):
    return 0.5 * x * (1.0 + jnp.tanh(math.sqrt(2.0 / math.pi) * (x + 0.044715 * (x * x * x))))


def _out_kernel(x_ref, mod_ref, ao_ref, dof_ref, dob_ref, gb_ref, u_ref, yf_ref, yb_ref, gc_ref,
                vec_ref, bd_ref, gluw_ref, wo_ref, lnv_ref, o_ref, *, n_ctx_tiles, d_model, alpha):
    vec = vec_ref[...]
    o = dof_ref[...] + dob_ref[...]
    ms = _mm(o * o, bd_ref[...], 6)
    dn = o * lax.rsqrt(ms + NORM_EPS) * vec[0:1] * _silu(gb_ref[...])
    z = _gelu_tanh(u_ref[...] * vec[1:2] + yf_ref[...] + yb_ref[...])
    lin = jnp.dot(z.astype(BF16), gluw_ref[...], preferred_element_type=F32) + vec[2:3]
    s5 = z * jax.nn.sigmoid(lin) * _silu(gc_ref[...])
    wo = wo_ref
    proj = (jnp.dot(ao_ref[...].astype(BF16), wo[:A_WIDTH, :], preferred_element_type=F32)
            + jnp.dot(dn.astype(BF16), wo[A_WIDTH:A_WIDTH + B_WIDTH, :],
                      preferred_element_type=F32)
            + jnp.dot(s5.astype(BF16), wo[A_WIDTH + B_WIDTH:, :], preferred_element_type=F32))
    gate = _mod_row(mod_ref, pl.program_id(0) < n_ctx_tiles)[:, 2 * d_model:]
    lnv = lnv_ref[...]
    o_ref[...] = _layer_norm(alpha * x_ref[...] + gate * proj) * lnv[0:1] + lnv[1:2]


def _out_proj(xall, mod_l, ao, dof, dob, gb, uc, yf, yb, gc, vec, gluw, wo, lnv, n_ctx_tiles,
              alpha):
    n, d = xall.shape
    tm = ROW_TILE
    bd = _block_diag_const(B_WIDTH, 1.0 / HEAD_DIM)
    row = lambda w: pl.BlockSpec((tm, w), lambda i: (i, 0))
    full = lambda a: pl.BlockSpec(a.shape, lambda i: (0,) * a.ndim)
    return pl.pallas_call(
        functools.partial(_out_kernel, n_ctx_tiles=n_ctx_tiles, d_model=d, alpha=alpha),
        grid=(n // tm,),
        in_specs=[row(d), full(mod_l), row(A_WIDTH), row(B_WIDTH), row(B_WIDTH), row(B_WIDTH),
                  row(C_WIDTH), row(C_WIDTH), row(C_WIDTH), row(C_WIDTH), full(vec), full(bd),
                  full(gluw), full(wo), full(lnv)],
        out_specs=row(d),
        out_shape=jax.ShapeDtypeStruct((n, d), F32),
        compiler_params=_cparams("arbitrary"),
        name="out_proj",
    )(xall, mod_l, ao, dof, dob, gb, uc, yf, yb, gc, vec, bd, gluw, wo, lnv)


def _rope_tables(n_lat, n_ctx):
    rows = n_lat // GRID_W
    row = jnp.repeat(jnp.arange(rows, dtype=F32), GRID_W)
    col = jnp.tile(jnp.arange(GRID_W, dtype=F32), rows)
    axis_dim = HEAD_DIM // 2
    inv = ROPE_THETA ** (-jnp.arange(0, axis_dim, 2, dtype=F32) / axis_dim)
    ar, ac = row[:, None] * inv, col[:, None] * inv
    ang = jnp.concatenate([ar, ar, ac, ac], -1)
    cos = jnp.concatenate([jnp.ones((n_ctx, HEAD_DIM), F32), jnp.cos(ang)], 0)
    sin = jnp.concatenate([jnp.zeros((n_ctx, HEAD_DIM), F32), jnp.sin(ang)], 0)
    upper = (jnp.arange(HEAD_DIM) % (HEAD_DIM // 2)) >= HEAD_DIM // 4
    sa = jnp.where(upper, 0.0, -sin)
    sb = jnp.where(upper, sin, 0.0)
    tile = lambda t: jnp.tile(t, (1, LANES // HEAD_DIM))
    return tile(cos), tile(sa), tile(sb)


def kernel(x, c, ctx, c_ctx, w_mod, b_mod, w_in, attn_q_gain, attn_k_gain, dn_conv_w, dn_A_log, dn_dt_bias, dn_out_gain, s5_A_re, s5_A_im, s5_log_dt, s5_B_re, s5_B_im, s5_C_re, s5_C_im, s5_D, glu_w, glu_b, w_out, ln_g, ln_b):
    batch, n_lat, d = x.shape
    n_ctx = ctx.shape[1]
    depth = w_mod.shape[0]
    assert batch == 1 and n_ctx % ROW_TILE == 0 and n_lat % ROW_TILE == 0 and n_lat % GRID_W == 0
    nct = n_ctx // ROW_TILE
    alpha = (2 * depth) ** 0.25

    xall = jnp.concatenate([ctx[0], x[0]], axis=0)
    cv = jnp.zeros((SUBLANES, d), F32).at[0].set(c[0]).at[1].set(c_ctx)
    mod = _modulation(cv, w_mod, b_mod)
    cos, sa, sb = _rope_tables(n_lat, n_ctx)
    lane_tile = lambda g: jnp.tile(g.astype(F32)[None, :], (1, LANES // HEAD_DIM))

    for l in range(depth):
        (qa, ka, va, ga, qkvb, gb, bdraw, uc, gc) = _inproj(xall, mod[l], _pad_w_in(w_in[l]), nct)
        q8, k2, v2 = _attn_prep(qa, ka, va, cos, sa, sb, lane_tile(attn_q_gain[l]),
                                lane_tile(attn_k_gain[l]))
        ao = _flash(q8.reshape(A_KV_HEADS, A_GROUPS, *q8.shape[1:]), k2, v2, ga, n_ctx)
        qn, kn, vn, bg = _dn_prep(qkvb, bdraw, dn_conv_w[l].astype(F32), dn_A_log[l],
                                  dn_dt_bias[l], n_ctx)
        dof, dob = _dn_scan(qn, kn, vn, bg, n_ctx)
        yf, yb = _s5_scan(uc, _s5_params(s5_A_re[l], s5_A_im[l], s5_log_dt[l], s5_B_re[l],
                                         s5_B_im[l], s5_C_re[l], s5_C_im[l]), n_ctx)
        vec = jnp.zeros((SUBLANES, C_WIDTH), F32)
        vec = vec.at[0].set(jnp.tile(dn_out_gain[l].astype(F32), B_HEADS))
        vec = vec.at[1].set(s5_D[l].astype(F32)).at[2].set(glu_b[l].astype(F32))
        lnv = jnp.zeros((SUBLANES, d), F32).at[0].set(ln_g[l]).at[1].set(ln_b[l])
        xall = _out_proj(xall, mod[l], ao, dof, dob, gb, uc, yf, yb, gc, vec,
                         glu_w[l].astype(BF16), w_out[l].astype(BF16), lnv, nct, alpha)
    return xall[n_ctx:][None]
```

```python
import functools
import math

import jax
import jax.numpy as jnp
from jax import lax
from jax.experimental import pallas as pl
from jax.experimental.pallas import tpu as pltpu

F32 = jnp.float32
BF16 = jnp.bfloat16
HIGHEST = lax.Precision.HIGHEST

GRID_W = 64
HEAD_DIM = 64
NORM_EPS = 1e-6
ROPE_THETA = 10000.0
A_Q_HEADS = 8
A_KV_HEADS = 2
A_GROUPS = A_Q_HEADS // A_KV_HEADS
A_WIDTH = A_Q_HEADS * HEAD_DIM
A_KV_WIDTH = A_KV_HEADS * HEAD_DIM
B_HEADS = 4
B_WIDTH = B_HEADS * HEAD_DIM
B_CHUNK = 64
C_WIDTH = 256
C_GROUP = 16
C_GROUPS = C_WIDTH // C_GROUP
C_STATE = 64
C_STATES = C_GROUPS * C_STATE
LANES = 128
SUBLANES = 8
ROW_TILE = 256
S5_STEPS = ROW_TILE // SUBLANES
VMEM_LIMIT = 48 * 1024 * 1024
LOG2E = math.log2(math.e)

DN_PREC = 1
S5_PREC = 1
DN_CHUNKS_PER_STEP = 2


def _cparams(*sem):
    return pltpu.CompilerParams(dimension_semantics=sem, vmem_limit_bytes=VMEM_LIMIT)


def _split(a):
    hi = a.astype(BF16)
    lo = (a - hi.astype(F32)).astype(BF16)
    return hi, lo


def _dg(a, b, dims):
    return lax.dot_general(a, b, (dims, ((), ())), preferred_element_type=F32)


def _mm(a, b, prec, dims=((1,), (0,))):
    if prec == 6:
        return lax.dot_general(a, b, (dims, ((), ())), precision=HIGHEST,
                               preferred_element_type=F32)
    if prec == 1:
        return _dg(a.astype(BF16), b.astype(BF16), dims)
    ah, al = _split(a)
    bh, bl = _split(b)
    return _dg(ah, bh, dims) + (_dg(ah, bl, dims) + _dg(al, bh, dims))


_NT = ((1,), (1,))
_TN = ((0,), (0,))


def _silu(x):
    return x * jax.nn.sigmoid(x)


def _mod_kernel(cv_ref, w_ref, b_ref, o_ref):
    sv = _silu(cv_ref[...])
    o_ref[0] = _mm(sv, w_ref[0], 6) + b_ref[0]


def _modulation(cv, w_mod, b_mod):
    depth, d, d3 = w_mod.shape
    nblk = d3 // d
    return pl.pallas_call(
        _mod_kernel,
        grid=(depth, nblk),
        in_specs=[pl.BlockSpec((SUBLANES, d), lambda l, n: (0, 0)),
                  pl.BlockSpec((1, d, d), lambda l, n: (l, 0, n)),
                  pl.BlockSpec((1, 1, d), lambda l, n: (l, 0, n))],
        out_specs=pl.BlockSpec((1, SUBLANES, d), lambda l, n: (l, 0, n)),
        out_shape=jax.ShapeDtypeStruct((depth, SUBLANES, d3), F32),
        compiler_params=_cparams("arbitrary", "arbitrary"),
        name="modulation",
    )(cv, w_mod, b_mod.reshape(depth, 1, d3))


def _layer_norm(x):
    mu = jnp.mean(x, axis=-1, keepdims=True)
    xc = x - mu
    var = jnp.mean(xc * xc, axis=-1, keepdims=True)
    return xc * lax.rsqrt(var + NORM_EPS)


def _mod_row(mod_ref, is_ctx):
    mod = mod_ref[...]
    return jnp.where(is_ctx, mod[1:2], mod[0:1])


_IN_SPLITS = (("qa", A_WIDTH), ("ka", A_KV_WIDTH), ("va", A_KV_WIDTH), ("ga", A_WIDTH),
              ("qkvb", 3 * B_WIDTH), ("gb", B_WIDTH), ("bd", LANES), ("uc", C_WIDTH),
              ("gc", C_WIDTH))
_IN_PAD_WIDTH = sum(w for _, w in _IN_SPLITS)


def _inproj_kernel(x_ref, mod_ref, w_ref, *out_refs, n_ctx_tiles, d_model):
    row = _mod_row(mod_ref, pl.program_id(0) < n_ctx_tiles)
    shift, scale = row[:, :d_model], row[:, d_model:2 * d_model]
    h = _layer_norm(x_ref[...]) * (1.0 + scale) + shift
    z = jnp.dot(h.astype(BF16), w_ref[...], preferred_element_type=F32)
    off = 0
    for ref, (_, w) in zip(out_refs, _IN_SPLITS):
        ref[...] = z[:, off:off + w]
        off += w


def _pad_w_in(w):
    d = w.shape[0]
    n_main = 2 * A_WIDTH + 2 * A_KV_WIDTH + 4 * B_WIDTH
    n_small = 4 * B_HEADS
    return jnp.concatenate(
        [w[:, :n_main], w[:, n_main:n_main + n_small], jnp.zeros((d, LANES - n_small), w.dtype),
         w[:, n_main + n_small:]], axis=1).astype(BF16)


def _inproj(xall, mod_l, w_pad, n_ctx_tiles):
    n, d = xall.shape
    tm = ROW_TILE
    return pl.pallas_call(
        functools.partial(_inproj_kernel, n_ctx_tiles=n_ctx_tiles, d_model=d),
        grid=(n // tm,),
        in_specs=[pl.BlockSpec((tm, d), lambda i: (i, 0)),
                  pl.BlockSpec(mod_l.shape, lambda i: (0, 0)),
                  pl.BlockSpec(w_pad.shape, lambda i: (0, 0))],
        out_specs=[pl.BlockSpec((tm, w), lambda i: (i, 0)) for _, w in _IN_SPLITS],
        out_shape=[jax.ShapeDtypeStruct((n, w), F32) for _, w in _IN_SPLITS],
        compiler_params=_cparams("arbitrary"),
        name="inproj",
    )(xall, mod_l, w_pad)


def _block_diag_const(width, value):
    r = jnp.arange(width) // HEAD_DIM
    return jnp.where(r[:, None] == r[None, :], value, 0.0).astype(F32)


def _attn_prep_kernel(q_ref, k_ref, v_ref, cos_ref, sa_ref, sb_ref, gq_ref, gk_ref, bd_ref,
                      qo_ref, ko_ref, vo_ref):
    cos, sa, sb, bd = cos_ref[...], sa_ref[...], sb_ref[...], bd_ref[...]

    def norm_rope(y, gain):
        ms = _mm(y * y, bd, 6)
        yn = y * lax.rsqrt(ms + NORM_EPS) * gain
        return (yn * cos + pltpu.roll(yn, LANES - HEAD_DIM // 4, 1) * sa
                + pltpu.roll(yn, HEAD_DIM // 4, 1) * sb)

    gq, gk = gq_ref[...], gk_ref[...]
    lane = lax.broadcasted_iota(jnp.int32, (1, LANES), 1)
    lower = lane < HEAD_DIM
    for j in range(A_Q_HEADS // 2):
        y = norm_rope(q_ref[:, j * LANES:(j + 1) * LANES], gq) * (HEAD_DIM ** -0.5 * LOG2E)
        y_sw = pltpu.roll(y, HEAD_DIM, 1)
        for half in range(2):
            h = 2 * j + half
            src = y if (h // A_GROUPS) == half else y_sw
            keep = lower if (h // A_GROUPS) == 0 else jnp.logical_not(lower)
            qo_ref[h] = jnp.where(keep, src, 0.0).astype(BF16)
    ko_ref[...] = norm_rope(k_ref[...], gk).astype(BF16)
    vt = v_ref[...].T
    ones = jnp.ones((HEAD_DIM, vt.shape[1]), F32)
    for h in range(A_KV_HEADS):
        vo_ref[h] = jnp.concatenate([vt[h * HEAD_DIM:(h + 1) * HEAD_DIM], ones],
                                    axis=0).astype(BF16)


def _attn_prep(qa, ka, va, cos, sa, sb, gq, gk):
    n = qa.shape[0]
    tm = ROW_TILE
    bd = _block_diag_const(LANES, 1.0 / HEAD_DIM)
    row = lambda w: pl.BlockSpec((tm, w), lambda i: (i, 0))
    full = lambda a: pl.BlockSpec(a.shape, lambda i: (0,) * a.ndim)
    return pl.pallas_call(
        _attn_prep_kernel,
        grid=(n // tm,),
        in_specs=[row(A_WIDTH), row(A_KV_WIDTH), row(A_KV_WIDTH), row(LANES), row(LANES),
                  row(LANES), full(gq), full(gk), full(bd)],
        out_specs=[pl.BlockSpec((A_Q_HEADS, tm, LANES), lambda i: (0, i, 0)),
                   pl.BlockSpec((tm, LANES), lambda i: (i, 0)),
                   pl.BlockSpec((A_KV_HEADS, LANES, tm), lambda i: (0, 0, i))],
        out_shape=[jax.ShapeDtypeStruct((A_Q_HEADS, n, LANES), BF16),
                   jax.ShapeDtypeStruct((n, LANES), BF16),
                   jax.ShapeDtypeStruct((A_KV_HEADS, LANES, n), BF16)],
        compiler_params=_cparams("arbitrary"),
        name="attn_prep",
    )(qa, ka, va, cos, sa, sb, gq, gk, bd)


def _flash_kernel(q_ref, k_ref, vt_ref, g_ref, o_ref, m_sc, mblk_sc, alpha_sc, acc_sc, s_sc, p_sc,
                  *, n_ctx, n_ctx_tiles, tc, n_chunks):
    is_ctx = pl.program_id(0) < n_ctx_tiles
    tq = q_ref.shape[1]
    m_sc[...] = jnp.full(m_sc.shape, -jnp.inf, F32)
    acc_sc[...] = jnp.zeros(acc_sc.shape, F32)

    def q_group(g):
        return q_ref[g * A_GROUPS:(g + 1) * A_GROUPS].reshape(A_GROUPS * tq, LANES)

    def scores(g, c0, size):
        s = _dg(k_ref[pl.ds(c0, size), :], q_group(g), _NT)
        s_sc[g, :size, :] = s
        mblk_sc[g] = jnp.max(s, axis=0, keepdims=True)

    def exponentials(g, size):
        m_old = m_sc[g]
        m_new = jnp.maximum(m_old, mblk_sc[g])
        alpha_sc[g] = jnp.exp2(m_old - m_new)
        p_sc[g, :size, :] = jnp.exp2(s_sc[g, :size, :] - m_new).astype(BF16)
        m_sc[g] = m_new

    def weighted_values(g, c0, size):
        acc_sc[g] = alpha_sc[g] * acc_sc[g] + jnp.dot(
            vt_ref[g, :, pl.ds(c0, size)], p_sc[g, :size, :], preferred_element_type=F32)

    @pl.when(is_ctx)
    def _():
        for g in range(A_KV_HEADS):
            scores(g, 0, n_ctx)
            exponentials(g, n_ctx)
            weighted_values(g, 0, n_ctx)

    @pl.when(jnp.logical_not(is_ctx))
    def _():
        first, second = 0, A_KV_HEADS - 1
        p_sc[second] = jnp.zeros(p_sc.shape[1:], BF16)
        alpha_sc[second] = jnp.ones(alpha_sc.shape[1:], F32)
        scores(first, 0, tc)

        def body(c, carry):
            c0 = pl.multiple_of(c * tc, tc)
            c_prev = pl.multiple_of(jnp.maximum(c - 1, 0) * tc, tc)
            c_next = pl.multiple_of(jnp.minimum(c + 1, n_chunks - 1) * tc, tc)
            scores(second, c0, tc)
            exponentials(first, tc)
            weighted_values(second, c_prev, tc)
            scores(first, c_next, tc)
            exponentials(second, tc)
            weighted_values(first, c0, tc)
            return carry

        lax.fori_loop(0, n_chunks, body, 0)
        weighted_values(second, (n_chunks - 1) * tc, tc)

    gate = _silu(g_ref[...])
    for h in range(A_Q_HEADS):
        g, r = divmod(h, A_GROUPS)
        at = acc_sc[g, :, r * tq:(r + 1) * tq].T
        sl = slice(h * HEAD_DIM, (h + 1) * HEAD_DIM)
        o_ref[:, sl] = at[:, :HEAD_DIM] / at[:, HEAD_DIM:HEAD_DIM + 1] * gate[:, sl]


def _key_chunk(n):
    for tc in (1280, 1024, 640, 512, 256, 128):
        if n % tc == 0:
            return tc
    raise ValueError(f"no key chunk for {n} rows")


def _flash(qz, k, vt, ga, n_ctx):
    n = k.shape[0]
    tq = ROW_TILE
    tc = _key_chunk(n)
    gq = A_GROUPS * tq
    assert A_KV_HEADS == 2 and n_ctx <= tc
    full = lambda a: pl.BlockSpec(a.shape, lambda i: (0,) * a.ndim,
                                  pipeline_mode=pl.Buffered(1))
    return pl.pallas_call(
        functools.partial(_flash_kernel, n_ctx=n_ctx, n_ctx_tiles=n_ctx // tq, tc=tc,
                          n_chunks=n // tc),
        grid=(n // tq,),
        in_specs=[pl.BlockSpec((A_Q_HEADS, tq, LANES), lambda i: (0, i, 0)),
                  full(k), full(vt),
                  pl.BlockSpec((tq, A_WIDTH), lambda i: (i, 0))],
        out_specs=pl.BlockSpec((tq, A_WIDTH), lambda i: (i, 0)),
        out_shape=jax.ShapeDtypeStruct((n, A_WIDTH), F32),
        scratch_shapes=[pltpu.VMEM((A_KV_HEADS, 1, gq), F32),
                        pltpu.VMEM((A_KV_HEADS, 1, gq), F32),
                        pltpu.VMEM((A_KV_HEADS, 1, gq), F32),
                        pltpu.VMEM((A_KV_HEADS, LANES, gq), F32),
                        pltpu.VMEM((A_KV_HEADS, tc, gq), F32),
                        pltpu.VMEM((A_KV_HEADS, tc, gq), BF16)],
        compiler_params=_cparams("arbitrary"),
        name="flash_attn",
    )(qz, k, vt, ga)


def _dn_prep_kernel(z_ref, zp_ref, zn_ref, r_ref, w_ref, par_ref, bd_ref,
                    q_o, k_o, v_o, bg_o, *, seq_starts, seq_ends):
    i = pl.program_id(0)
    z = z_ref[...]
    tm = z.shape[0]
    is_start = functools.reduce(jnp.logical_or, [i == s for s in seq_starts])
    is_end = functools.reduce(jnp.logical_or, [i == s for s in seq_ends])
    prev_row = jnp.where(is_start, 0.0, zp_ref[SUBLANES - 1:SUBLANES, :])
    next_row = jnp.where(is_end, 0.0, zn_ref[0:1, :])
    row = lax.broadcasted_iota(jnp.int32, (tm, 1), 0)
    z_prev = jnp.where(row == 0, prev_row, pltpu.roll(z, 1, 0))
    z_next = jnp.where(row == tm - 1, next_row, pltpu.roll(z, tm - 1, 0))
    w = w_ref[...]
    y = _silu(z_prev * w[0:1] + z * w[1:2] + z_next * w[2:3])
    bd = bd_ref[...]

    def l2norm(t):
        return t * lax.rsqrt(_mm(t * t, bd, 6) + NORM_EPS)

    q_o[...] = l2norm(y[:, :B_WIDTH]) * (HEAD_DIM ** -0.5)
    k_o[...] = l2norm(y[:, B_WIDTH:2 * B_WIDTH])
    v_o[...] = y[:, 2 * B_WIDTH:]
    r = r_ref[...]
    par = par_ref[...]
    xa = r + par[1:2]
    softplus = jnp.maximum(xa, 0.0) + jnp.log(1.0 + jnp.exp(-jnp.abs(xa)))
    g = -jnp.exp(par[0:1]) * softplus
    lane = lax.broadcasted_iota(jnp.int32, (1, LANES), 1)
    bg_o[...] = jnp.where(lane < 2 * B_HEADS, jax.nn.sigmoid(r), g)


def _dn_prep(qkvb, bdraw, conv_w, a_log, dt_bias, n_ctx):
    n = qkvb.shape[0]
    tm = ROW_TILE
    nt, nct = n // tm, n_ctx // tm
    r8 = tm // SUBLANES
    par = jnp.zeros((2, LANES), F32)
    par = par.at[0, 2 * B_HEADS:4 * B_HEADS].set(a_log.reshape(-1))
    par = par.at[1, 2 * B_HEADS:4 * B_HEADS].set(dt_bias.reshape(-1))
    bd = _block_diag_const(B_WIDTH, 1.0)
    w3 = 3 * B_WIDTH
    return pl.pallas_call(
        functools.partial(_dn_prep_kernel, seq_starts=(0, nct), seq_ends=(nct - 1, nt - 1)),
        grid=(nt,),
        in_specs=[pl.BlockSpec((tm, w3), lambda i: (i, 0)),
                  pl.BlockSpec((SUBLANES, w3), lambda i: (jnp.maximum(i * r8 - 1, 0), 0)),
                  pl.BlockSpec((SUBLANES, w3),
                               lambda i: (jnp.minimum((i + 1) * r8, n // SUBLANES - 1), 0)),
                  pl.BlockSpec((tm, LANES), lambda i: (i, 0)),
                  pl.BlockSpec(conv_w.shape, lambda i: (0, 0)),
                  pl.BlockSpec(par.shape, lambda i: (0, 0)),
                  pl.BlockSpec(bd.shape, lambda i: (0, 0))],
        out_specs=[pl.BlockSpec((tm, B_WIDTH), lambda i: (i, 0))] * 3
        + [pl.BlockSpec((tm, LANES), lambda i: (i, 0))],
        out_shape=[jax.ShapeDtypeStruct((n, B_WIDTH), F32)] * 3
        + [jax.ShapeDtypeStruct((n, LANES), F32)],
        compiler_params=_cparams("arbitrary"),
        name="dn_prep",
    )(qkvb, qkvb, qkvb, bdraw, conv_w, par, bd)


def _dn_scan_kernel(qf, kf, vf, bgf, qb, kb, vb, bgb, of_ref, ob_ref, s_ref, *, prec, cps):
    @pl.when(pl.program_id(0) == 0)
    def _():
        s_ref[...] = jnp.zeros(s_ref.shape, F32)

    c, hc = B_CHUNK, B_HEADS * B_CHUNK
    row = lax.broadcasted_iota(jnp.int32, (hc, hc), 0)
    col = lax.broadcasted_iota(jnp.int32, (hc, hc), 1)
    same_head = (row // c) == (col // c)
    eye = (row == col).astype(F32)
    level_masks = [(row // 2) == (col // 2)]
    bs = 2
    while bs < c:
        level_masks.append(jnp.logical_and((row // (2 * bs)) == (col // (2 * bs)),
                                           (row // bs) != (col // bs)))
        bs *= 2
    rc, cc = lax.broadcasted_iota(jnp.int32, (c, c), 0), lax.broadcasted_iota(jnp.int32, (c, c), 1)

    def heads_bd(x):
        return jnp.where(same_head, jnp.concatenate([x] * B_HEADS, axis=0), 0.0)

    def rows_of(slab, lane0):
        return jnp.concatenate(
            [jnp.broadcast_to(slab[:, lane0 + h:lane0 + h + 1], (c, hc)) for h in range(B_HEADS)],
            axis=0)

    def chunk_terms(d, q_ref, k_ref, v_ref, bg_ref, r0):
        rows = slice(r0, r0 + c)
        incl = (col <= row) if d == 0 else (col >= row)
        strict = (col < row) if d == 0 else (col > row)
        incl_c = (cc <= rc) if d == 0 else (cc >= rc)
        last = c - 1 if d == 0 else 0
        lb = d * B_HEADS
        lg = 2 * B_HEADS + lb
        bg = bg_ref[rows, :]
        gcum = _mm(incl_c.astype(F32), bg, 6)
        gcum_t = gcum.T
        g_row = jnp.concatenate([gcum_t[lg + h:lg + h + 1, :] for h in range(B_HEADS)], axis=1)
        g_last = gcum[last:last + 1, :]
        decay = jnp.where(jnp.logical_and(same_head, incl),
                          jnp.exp(rows_of(gcum, lg) - g_row), 0.0)
        beta = rows_of(bg, lb)
        eg = rows_of(jnp.exp(gcum), lg)
        to_last = rows_of(jnp.exp(g_last - gcum), lg)
        q_bd = heads_bd(q_ref[rows, :])
        k_bd = heads_bd(k_ref[rows, :])
        v_bd = heads_bd(v_ref[rows, :])
        k_b = k_bd.astype(BF16)
        kk = _mm(k_b, k_b, prec, _NT)
        qk = _mm(q_bd, k_b, prec, _NT)
        a_mat = jnp.where(jnp.logical_and(same_head, strict), kk * beta * decay, 0.0)
        t = eye - jnp.where(level_masks[0], a_mat, 0.0)
        for lm in level_masks[1:]:
            t = t - _mm(_mm(t, jnp.where(lm, a_mat, 0.0), prec), t, prec)
        t = t.astype(BF16)
        g_state = jnp.concatenate(
            [jnp.broadcast_to(jnp.exp(g_last[:, lg + h:lg + h + 1]), (c, hc))
             for h in range(B_HEADS)], axis=0)
        return dict(w=_mm(t, v_bd * beta, prec), k_cum=_mm(t, k_bd * (beta * eg), prec),
                    q_dec=q_bd * eg, attn=qk * decay, k_dec=k_bd * to_last, g_state=g_state)

    dirs = ((qf, kf, vf, bgf, of_ref), (qb, kb, vb, bgb, ob_ref))
    for d, (q_ref, k_ref, v_ref, bg_ref, o_ref) in enumerate(dirs):
        order = range(cps) if d == 0 else reversed(range(cps))
        terms = {j: chunk_terms(d, q_ref, k_ref, v_ref, bg_ref, j * c) for j in range(cps)}
        s = s_ref[d]
        for j in order:
            tm = terms[j]
            u = tm["w"] - _mm(tm["k_cum"], s, prec)
            o_bd = _mm(tm["q_dec"], s, prec) + _mm(tm["attn"], u, prec)
            s = s * tm["g_state"] + _mm(tm["k_dec"], u, prec, _TN)
            o_ref[j * c:(j + 1) * c, :] = functools.reduce(
                jnp.add, [o_bd[h * c:(h + 1) * c] for h in range(B_HEADS)])
        s_ref[d] = s


def _scan_order_maps(n_chunks, n_ctx_chunks):
    fwd = lambda j: (j, 0)
    bwd = lambda j: (jnp.where(j < n_ctx_chunks, n_ctx_chunks - 1 - j,
                               n_chunks - 1 - (j - n_ctx_chunks)), 0)
    return fwd, bwd


def _dn_scan(qn, kn, vn, bg, n_ctx):
    n = qn.shape[0]
    assert B_CHUNK == HEAD_DIM
    c = B_CHUNK * DN_CHUNKS_PER_STEP
    fwd, bwd = _scan_order_maps(n // c, n_ctx // c)
    specs = lambda m: [pl.BlockSpec((c, B_WIDTH), m)] * 3 + [pl.BlockSpec((c, LANES), m)]
    return pl.pallas_call(
        functools.partial(_dn_scan_kernel, prec=DN_PREC, cps=DN_CHUNKS_PER_STEP),
        grid=(n // c,),
        in_specs=specs(fwd) + specs(bwd),
        out_specs=[pl.BlockSpec((c, B_WIDTH), fwd), pl.BlockSpec((c, B_WIDTH), bwd)],
        out_shape=[jax.ShapeDtypeStruct((n, B_WIDTH), F32)] * 2,
        scratch_shapes=[pltpu.VMEM((2, B_WIDTH, B_WIDTH), F32)],
        compiler_params=_cparams("arbitrary"),
        name="dn_scan",
    )(qn, kn, vn, bg, qn, kn, vn, bg)


def _s5_kernel(uf_ref, ub_ref, bre_ref, bim_ref, cre_ref, cimn_ref, a_ref, ap_ref,
               yf_ref, yb_ref, hre_sc, him_sc, carry_sc, *, prec):
    @pl.when(pl.program_id(0) == 0)
    def _():
        carry_sc[...] = jnp.zeros(carry_sc.shape, F32)

    steps, nsub = S5_STEPS, SUBLANES
    ts = steps * nsub
    prow = lax.broadcasted_iota(jnp.int32, (ts, ts), 0)
    pcol = lax.broadcasted_iota(jnp.int32, (ts, ts), 1)
    scan_pos = (prow % nsub) * steps + prow // nsub
    perms = [jnp.where(pcol == scan_pos, 1.0, 0.0).astype(BF16),
             jnp.where(pcol == ts - 1 - scan_pos, 1.0, 0.0).astype(BF16)]
    ndir = 2

    def permute(pm, x, dims, pieces):
        out, rest = None, x
        for _ in range(pieces):
            part = rest.astype(BF16)
            rest = rest - part.astype(F32)
            term = _dg(pm, part, dims)
            out = term if out is None else out + term
        return out

    for d, u_ref in enumerate((uf_ref, ub_ref)):
        u_perm = permute(perms[d], u_ref[...], ((1,), (0,)), 1 if prec == 1 else 2)
        hre_sc[d] = _mm(u_perm, bre_ref[d], prec)
        him_sc[d] = _mm(u_perm, bim_ref[d], prec)

    a = [(a_ref[d, 0:1, :], a_ref[d, 1:2, :]) for d in range(ndir)]

    def rows_at(jj):
        return pl.ds(pl.multiple_of(jj * nsub, nsub), nsub)

    def local_scan(jj, carry):
        rows = rows_at(jj)
        out = []
        for d in range(ndir):
            (a_re, a_im), (h_re, h_im) = a[d], carry[d]
            n_re = a_re * h_re - a_im * h_im + hre_sc[d, rows, :]
            n_im = a_re * h_im + a_im * h_re + him_sc[d, rows, :]
            hre_sc[d, rows, :] = n_re
            him_sc[d, rows, :] = n_im
            out.append((n_re, n_im))
        return tuple(out)

    zero = jnp.zeros((nsub, C_STATES), F32)
    ends = lax.fori_loop(0, steps, local_scan, ((zero, zero),) * ndir)
    cins = []
    for d in range(ndir):
        as_re = ap_ref[d, 0, steps - 1:steps, :]
        as_im = ap_ref[d, 1, steps - 1:steps, :]
        c_re, c_im = carry_sc[2 * d:2 * d + 1, :], carry_sc[2 * d + 1:2 * d + 2, :]
        cin = []
        for s in range(nsub):
            cin.append((c_re, c_im))
            c_re, c_im = (ends[d][0][s:s + 1] + as_re * c_re - as_im * c_im,
                          ends[d][1][s:s + 1] + as_re * c_im + as_im * c_re)
        carry_sc[2 * d:2 * d + 1, :] = c_re
        carry_sc[2 * d + 1:2 * d + 2, :] = c_im
        cins.append((jnp.concatenate([x[0] for x in cin], axis=0),
                     jnp.concatenate([x[1] for x in cin], axis=0)))

    def fixup(jj, carry):
        rows = rows_at(jj)
        for d in range(ndir):
            p_re = ap_ref[d, 0, pl.ds(jj, 1), :]
            p_im = ap_ref[d, 1, pl.ds(jj, 1), :]
            cin_re, cin_im = cins[d]
            hre_sc[d, rows, :] += p_re * cin_re - p_im * cin_im
            him_sc[d, rows, :] += p_re * cin_im + p_im * cin_re
        return carry

    lax.fori_loop(0, steps, fixup, 0)
    for d, y_ref in enumerate((yf_ref, yb_ref)):
        y_perm = _mm(hre_sc[d], cre_ref[d], prec) + _mm(him_sc[d], cimn_ref[d], prec)
        y_ref[...] = permute(perms[d], y_perm, _TN, 2)


def _s5_params(a_re, a_im, log_dt, b_re, b_im, c_re, c_im):
    dt = jnp.exp(log_dt.astype(F32))[..., None]
    lam_re, lam_im = a_re.astype(F32), a_im.astype(F32)
    k = jnp.arange(1, S5_STEPS + 1, dtype=F32)[None, :, None, None]
    mag = jnp.exp(lam_re[:, None] * dt[:, None] * k)
    ang = lam_im[:, None] * dt[:, None] * k
    ap = jnp.stack([mag * jnp.cos(ang), mag * jnp.sin(ang)], axis=1)
    ap = ap.reshape(2, 2, S5_STEPS, C_STATES)
    ab_re, ab_im = ap[:, 0, 0], ap[:, 1, 0]
    abar = jnp.stack([ab_re, ab_im], axis=1)
    x, y = ab_re.reshape(lam_re.shape) - 1.0, ab_im.reshape(lam_re.shape)
    den = lam_re * lam_re + lam_im * lam_im
    cf_re, cf_im = (x * lam_re + y * lam_im) / den, (y * lam_re - x * lam_im) / den
    bb_re = cf_re[..., None] * b_re - cf_im[..., None] * b_im
    bb_im = cf_re[..., None] * b_im + cf_im[..., None] * b_re
    eye = jnp.eye(C_GROUPS, dtype=F32)
    dense_b = lambda b: jnp.einsum('dgpc,gh->dgchp', b, eye).reshape(2, C_WIDTH, C_STATES)
    dense_c = lambda c: jnp.einsum('dgcp,gh->dgphc', c, eye).reshape(2, C_STATES, C_WIDTH)
    return (dense_b(bb_re), dense_b(bb_im), dense_c(c_re.astype(F32)),
            dense_c(-c_im.astype(F32)), abar, ap)


def _s5_scan(uc, params, n_ctx):
    n = uc.shape[0]
    ts = ROW_TILE
    bre, bim, cre, cimn, abar, ap = params
    fwd, bwd = _scan_order_maps(n // ts, n_ctx // ts)
    full = lambda a: pl.BlockSpec(a.shape, lambda j: (0,) * a.ndim)
    return pl.pallas_call(
        functools.partial(_s5_kernel, prec=S5_PREC),
        grid=(n // ts,),
        in_specs=[pl.BlockSpec((ts, C_WIDTH), fwd), pl.BlockSpec((ts, C_WIDTH), bwd),
                  full(bre), full(bim), full(cre), full(cimn), full(abar), full(ap)],
        out_specs=[pl.BlockSpec((ts, C_WIDTH), fwd), pl.BlockSpec((ts, C_WIDTH), bwd)],
        out_shape=[jax.ShapeDtypeStruct((n, C_WIDTH), F32)] * 2,
        scratch_shapes=[pltpu.VMEM((2, ts, C_STATES), F32), pltpu.VMEM((2, ts, C_STATES), F32),
                        pltpu.VMEM((SUBLANES, C_STATES), F32)],
        compiler_params=_cparams("arbitrary"),
        name="s5_scan",
    )(uc, uc, bre, bim, cre, cimn, abar, ap)


def _gelu_tanh(x):
    return 0.5 * x * (1.0 + jnp.tanh(math.sqrt(2.0 / math.pi) * (x + 0.044715 * (x * x * x))))


def _out_kernel(x_ref, mod_ref, ao_ref, dof_ref, dob_ref, gb_ref, u_ref, yf_ref, yb_ref, gc_ref,
                vec_ref, bd_ref, gluw_ref, wo_ref, lnv_ref, o_ref, *, n_ctx_tiles, d_model, alpha):
    vec = vec_ref[...]
    o = dof_ref[...] + dob_ref[...]
    ms = _mm(o * o, bd_ref[...], 6)
    dn = o * lax.rsqrt(ms + NORM_EPS) * vec[0:1] * _silu(gb_ref[...])
    z = _gelu_tanh(u_ref[...] * vec[1:2] + yf_ref[...] + yb_ref[...])
    lin = jnp.dot(z.astype(BF16), gluw_ref[...], preferred_element_type=F32) + vec[2:3]
    s5 = z * jax.nn.sigmoid(lin) * _silu(gc_ref[...])
    wo = wo_ref
    proj = (jnp.dot(ao_ref[...].astype(BF16), wo[:A_WIDTH, :], preferred_element_type=F32)
            + jnp.dot(dn.astype(BF16), wo[A_WIDTH:A_WIDTH + B_WIDTH, :],
                      preferred_element_type=F32)
            + jnp.dot(s5.astype(BF16), wo[A_WIDTH + B_WIDTH:, :], preferred_element_type=F32))
    gate = _mod_row(mod_ref, pl.program_id(0) < n_ctx_tiles)[:, 2 * d_model:]
    lnv = lnv_ref[...]
    o_ref[...] = _layer_norm(alpha * x_ref[...] + gate * proj) * lnv[0:1] + lnv[1:2]


def _out_proj(xall, mod_l, ao, dof, dob, gb, uc, yf, yb, gc, vec, gluw, wo, lnv, n_ctx_tiles,
              alpha):
    n, d = xall.shape
    tm = ROW_TILE
    bd = _block_diag_const(B_WIDTH, 1.0 / HEAD_DIM)
    row = lambda w: pl.BlockSpec((tm, w), lambda i: (i, 0))
    full = lambda a: pl.BlockSpec(a.shape, lambda i: (0,) * a.ndim)
    return pl.pallas_call(
        functools.partial(_out_kernel, n_ctx_tiles=n_ctx_tiles, d_model=d, alpha=alpha),
        grid=(n // tm,),
        in_specs=[row(d), full(mod_l), row(A_WIDTH), row(B_WIDTH), row(B_WIDTH), row(B_WIDTH),
                  row(C_WIDTH), row(C_WIDTH), row(C_WIDTH), row(C_WIDTH), full(vec), full(bd),
                  full(gluw), full(wo), full(lnv)],
        out_specs=row(d),
        out_shape=jax.ShapeDtypeStruct((n, d), F32),
        compiler_params=_cparams("arbitrary"),
        name="out_proj",
    )(xall, mod_l, ao, dof, dob, gb, uc, yf, yb, gc, vec, bd, gluw, wo, lnv)


def _rope_tables(n_lat, n_ctx):
    rows = n_lat // GRID_W
    row = jnp.repeat(jnp.arange(rows, dtype=F32), GRID_W)
    col = jnp.tile(jnp.arange(GRID_W, dtype=F32), rows)
    axis_dim = HEAD_DIM // 2
    inv = ROPE_THETA ** (-jnp.arange(0, axis_dim, 2, dtype=F32) / axis_dim)
    ar, ac = row[:, None] * inv, col[:, None] * inv
    ang = jnp.concatenate([ar, ar, ac, ac], -1)
    cos = jnp.concatenate([jnp.ones((n_ctx, HEAD_DIM), F32), jnp.cos(ang)], 0)
    sin = jnp.concatenate([jnp.zeros((n_ctx, HEAD_DIM), F32), jnp.sin(ang)], 0)
    upper = (jnp.arange(HEAD_DIM) % (HEAD_DIM // 2)) >= HEAD_DIM // 4
    sa = jnp.where(upper, 0.0, -sin)
    sb = jnp.where(upper, sin, 0.0)
    tile = lambda t: jnp.tile(t, (1, LANES // HEAD_DIM))
    return tile(cos), tile(sa), tile(sb)


def kernel(x, c, ctx, c_ctx, w_mod, b_mod, w_in, attn_q_gain, attn_k_gain, dn_conv_w, dn_A_log, dn_dt_bias, dn_out_gain, s5_A_re, s5_A_im, s5_log_dt, s5_B_re, s5_B_im, s5_C_re, s5_C_im, s5_D, glu_w, glu_b, w_out, ln_g, ln_b):
    batch, n_lat, d = x.shape
    n_ctx = ctx.shape[1]
    depth = w_mod.shape[0]
    assert batch == 1 and n_ctx % ROW_TILE == 0 and n_lat % ROW_TILE == 0 and n_lat % GRID_W == 0
    nct = n_ctx // ROW_TILE
    alpha = (2 * depth) ** 0.25

    xall = jnp.concatenate([ctx[0], x[0]], axis=0)
    cv = jnp.zeros((SUBLANES, d), F32).at[0].set(c[0]).at[1].set(c_ctx)
    mod = _modulation(cv, w_mod, b_mod)
    cos, sa, sb = _rope_tables(n_lat, n_ctx)
    lane_tile = lambda g: jnp.tile(g.astype(F32)[None, :], (1, LANES // HEAD_DIM))

    for l in range(depth):
        (qa, ka, va, ga, qkvb, gb, bdraw, uc, gc) = _inproj(xall, mod[l], _pad_w_in(w_in[l]), nct)
        qz, kp, vt = _attn_prep(qa, ka, va, cos, sa, sb, lane_tile(attn_q_gain[l]),
                                lane_tile(attn_k_gain[l]))
        ao = _flash(qz, kp, vt, ga, n_ctx)
        qn, kn, vn, bg = _dn_prep(qkvb, bdraw, dn_conv_w[l].astype(F32), dn_A_log[l],
                                  dn_dt_bias[l], n_ctx)
        dof, dob = _dn_scan(qn, kn, vn, bg, n_ctx)
        yf, yb = _s5_scan(uc, _s5_params(s5_A_re[l], s5_A_im[l], s5_log_dt[l], s5_B_re[l],
                                         s5_B_im[l], s5_C_re[l], s5_C_im[l]), n_ctx)
        vec = jnp.zeros((SUBLANES, C_WIDTH), F32)
        vec = vec.at[0].set(jnp.tile(dn_out_gain[l].astype(F32), B_HEADS))
        vec = vec.at[1].set(s5_D[l].astype(F32)).at[2].set(glu_b[l].astype(F32))
        lnv = jnp.zeros((SUBLANES, d), F32).at[0].set(ln_g[l]).at[1].set(ln_b[l])
        xall = _out_proj(xall, mod[l], ao, dof, dob, gb, uc, yf, yb, gc, vec,
                         glu_w[l].astype(BF16), w_out[l].astype(BF16), lnv, nct, alpha)
    return xall[n_ctx:][None]
```

```python
import functools
import math

import jax
import jax.numpy as jnp
from jax import lax
from jax.experimental import pallas as pl
from jax.experimental.pallas import tpu as pltpu

F32 = jnp.float32
BF16 = jnp.bfloat16
HIGHEST = lax.Precision.HIGHEST

GRID_W = 64
HEAD_DIM = 64
NORM_EPS = 1e-6
ROPE_THETA = 10000.0
A_Q_HEADS = 8
A_KV_HEADS = 2
A_GROUPS = A_Q_HEADS // A_KV_HEADS
A_WIDTH = A_Q_HEADS * HEAD_DIM
A_KV_WIDTH = A_KV_HEADS * HEAD_DIM
B_HEADS = 4
B_WIDTH = B_HEADS * HEAD_DIM
B_CHUNK = 64
C_WIDTH = 256
C_GROUP = 16
C_GROUPS = C_WIDTH // C_GROUP
C_STATE = 64
C_STATES = C_GROUPS * C_STATE
LANES = 128
SUBLANES = 8
ROW_TILE = 256
S5_STEPS = ROW_TILE // SUBLANES
VMEM_LIMIT = 48 * 1024 * 1024
LOG2E = math.log2(math.e)

DN_PREC = 1
S5_PREC = 1
DN_CHUNKS_PER_STEP = 4


def _cparams(*sem):
    return pltpu.CompilerParams(dimension_semantics=sem, vmem_limit_bytes=VMEM_LIMIT)


def _split(a):
    hi = a.astype(BF16)
    lo = (a - hi.astype(F32)).astype(BF16)
    return hi, lo


def _dg(a, b, dims):
    return lax.dot_general(a, b, (dims, ((), ())), preferred_element_type=F32)


def _mm(a, b, prec, dims=((1,), (0,))):
    if prec == 6:
        return lax.dot_general(a, b, (dims, ((), ())), precision=HIGHEST,
                               preferred_element_type=F32)
    if prec == 1:
        return _dg(a.astype(BF16), b.astype(BF16), dims)
    ah, al = _split(a)
    bh, bl = _split(b)
    return _dg(ah, bh, dims) + (_dg(ah, bl, dims) + _dg(al, bh, dims))


_NT = ((1,), (1,))
_TN = ((0,), (0,))


def _silu(x):
    return x * jax.nn.sigmoid(x)


def _mod_kernel(cv_ref, w_ref, b_ref, o_ref):
    sv = _silu(cv_ref[...])
    o_ref[0] = _mm(sv, w_ref[0], 6) + b_ref[0]


def _modulation(cv, w_mod, b_mod):
    depth, d, d3 = w_mod.shape
    nblk = d3 // d
    return pl.pallas_call(
        _mod_kernel,
        grid=(depth, nblk),
        in_specs=[pl.BlockSpec((SUBLANES, d), lambda l, n: (0, 0)),
                  pl.BlockSpec((1, d, d), lambda l, n: (l, 0, n)),
                  pl.BlockSpec((1, 1, d), lambda l, n: (l, 0, n))],
        out_specs=pl.BlockSpec((1, SUBLANES, d), lambda l, n: (l, 0, n)),
        out_shape=jax.ShapeDtypeStruct((depth, SUBLANES, d3), F32),
        compiler_params=_cparams("arbitrary", "arbitrary"),
        name="modulation",
    )(cv, w_mod, b_mod.reshape(depth, 1, d3))


def _layer_norm(x):
    mu = jnp.mean(x, axis=-1, keepdims=True)
    xc = x - mu
    var = jnp.mean(xc * xc, axis=-1, keepdims=True)
    return xc * lax.rsqrt(var + NORM_EPS)


def _mod_row(mod_ref, is_ctx):
    mod = mod_ref[...]
    return jnp.where(is_ctx, mod[1:2], mod[0:1])


_IN_SPLITS = (("qa", A_WIDTH), ("ka", A_KV_WIDTH), ("va", A_KV_WIDTH), ("ga", A_WIDTH),
              ("qkvb", 3 * B_WIDTH), ("gb", B_WIDTH), ("bd", LANES), ("uc", C_WIDTH),
              ("gc", C_WIDTH))
_IN_PAD_WIDTH = sum(w for _, w in _IN_SPLITS)


def _row_source_specs(n_ctx_tiles, lat_off, d, first_tile=0):
    tm = ROW_TILE
    return [pl.BlockSpec((tm, d), lambda i: (jnp.minimum(i + first_tile, n_ctx_tiles - 1), 0)),
            pl.BlockSpec((tm, d),
                         lambda i: (jnp.maximum(i + first_tile - n_ctx_tiles, 0) + lat_off, 0))]


def _inproj_kernel(c_ref, l_ref, mod_ref, w_ref, *out_refs, n_ctx_tiles, d_model):
    is_ctx = pl.program_id(0) < n_ctx_tiles
    row = _mod_row(mod_ref, is_ctx)
    shift, scale = row[:, :d_model], row[:, d_model:2 * d_model]
    x = jnp.where(is_ctx, c_ref[...], l_ref[...])
    h = _layer_norm(x) * (1.0 + scale) + shift
    z = jnp.dot(h.astype(BF16), w_ref[...], preferred_element_type=F32)
    off = 0
    for ref, (_, w) in zip(out_refs, _IN_SPLITS):
        ref[...] = z[:, off:off + w]
        off += w


def _pad_w_in(w):
    d = w.shape[0]
    n_main = 2 * A_WIDTH + 2 * A_KV_WIDTH + 4 * B_WIDTH
    n_small = 4 * B_HEADS
    return jnp.concatenate(
        [w[:, :n_main], w[:, n_main:n_main + n_small], jnp.zeros((d, LANES - n_small), w.dtype),
         w[:, n_main + n_small:]], axis=1).astype(BF16)


def _inproj(ctx_src, lat_src, lat_off, n, mod_l, w_pad, n_ctx_tiles):
    d = lat_src.shape[1]
    tm = ROW_TILE
    return pl.pallas_call(
        functools.partial(_inproj_kernel, n_ctx_tiles=n_ctx_tiles, d_model=d),
        grid=(n // tm,),
        in_specs=_row_source_specs(n_ctx_tiles, lat_off, d)
        + [pl.BlockSpec(mod_l.shape, lambda i: (0, 0)),
           pl.BlockSpec(w_pad.shape, lambda i: (0, 0))],
        out_specs=[pl.BlockSpec((tm, w), lambda i: (i, 0)) for _, w in _IN_SPLITS],
        out_shape=[jax.ShapeDtypeStruct((n, w), F32) for _, w in _IN_SPLITS],
        compiler_params=_cparams("arbitrary"),
        name="inproj",
    )(ctx_src, lat_src, mod_l, w_pad)


def _block_diag_const(width, value):
    r = jnp.arange(width) // HEAD_DIM
    return jnp.where(r[:, None] == r[None, :], value, 0.0).astype(F32)


def _attn_prep_kernel(q_ref, k_ref, v_ref, cos_ref, sa_ref, sb_ref, gq_ref, gk_ref, bd_ref,
                      qo_ref, ko_ref, vo_ref):
    cos, sa, sb, bd = cos_ref[...], sa_ref[...], sb_ref[...], bd_ref[...]

    def norm_rope(y, gain):
        ms = _mm(y * y, bd, 6)
        yn = y * lax.rsqrt(ms + NORM_EPS) * gain
        return (yn * cos + pltpu.roll(yn, LANES - HEAD_DIM // 4, 1) * sa
                + pltpu.roll(yn, HEAD_DIM // 4, 1) * sb)

    gq, gk = gq_ref[...], gk_ref[...]
    lane = lax.broadcasted_iota(jnp.int32, (1, LANES), 1)
    lower = lane < HEAD_DIM
    for j in range(A_Q_HEADS // 2):
        y = norm_rope(q_ref[:, j * LANES:(j + 1) * LANES], gq) * (HEAD_DIM ** -0.5 * LOG2E)
        y_sw = pltpu.roll(y, HEAD_DIM, 1)
        for half in range(2):
            h = 2 * j + half
            src = y if (h // A_GROUPS) == half else y_sw
            keep = lower if (h // A_GROUPS) == 0 else jnp.logical_not(lower)
            qo_ref[h] = jnp.where(keep, src, 0.0).astype(BF16)
    ko_ref[...] = norm_rope(k_ref[...], gk).astype(BF16)
    vt = v_ref[...].T
    ones = jnp.ones((HEAD_DIM, vt.shape[1]), F32)
    for h in range(A_KV_HEADS):
        vo_ref[h] = jnp.concatenate([vt[h * HEAD_DIM:(h + 1) * HEAD_DIM], ones],
                                    axis=0).astype(BF16)


def _attn_prep(qa, ka, va, cos, sa, sb, gq, gk):
    n = qa.shape[0]
    tm = ROW_TILE
    bd = _block_diag_const(LANES, 1.0 / HEAD_DIM)
    row = lambda w: pl.BlockSpec((tm, w), lambda i: (i, 0))
    full = lambda a: pl.BlockSpec(a.shape, lambda i: (0,) * a.ndim)
    return pl.pallas_call(
        _attn_prep_kernel,
        grid=(n // tm,),
        in_specs=[row(A_WIDTH), row(A_KV_WIDTH), row(A_KV_WIDTH), row(LANES), row(LANES),
                  row(LANES), full(gq), full(gk), full(bd)],
        out_specs=[pl.BlockSpec((A_Q_HEADS, tm, LANES), lambda i: (0, i, 0)),
                   pl.BlockSpec((tm, LANES), lambda i: (i, 0)),
                   pl.BlockSpec((A_KV_HEADS, LANES, tm), lambda i: (0, 0, i))],
        out_shape=[jax.ShapeDtypeStruct((A_Q_HEADS, n, LANES), BF16),
                   jax.ShapeDtypeStruct((n, LANES), BF16),
                   jax.ShapeDtypeStruct((A_KV_HEADS, LANES, n), BF16)],
        compiler_params=_cparams("arbitrary"),
        name="attn_prep",
    )(qa, ka, va, cos, sa, sb, gq, gk, bd)


def _flash_kernel(q_ref, k_ref, vt_ref, g_ref, o_ref, m_sc, mblk_sc, alpha_sc, acc_sc, s_sc, p_sc,
                  *, n_ctx, n_ctx_tiles, tc, n_chunks):
    is_ctx = pl.program_id(0) < n_ctx_tiles
    tq = q_ref.shape[1]
    m_sc[...] = jnp.full(m_sc.shape, -jnp.inf, F32)
    acc_sc[...] = jnp.zeros(acc_sc.shape, F32)

    def q_group(g):
        return q_ref[g * A_GROUPS:(g + 1) * A_GROUPS].reshape(A_GROUPS * tq, LANES)

    def scores(g, c0, size):
        s = _dg(k_ref[pl.ds(c0, size), :], q_group(g), _NT)
        s_sc[g, :size, :] = s
        mblk_sc[g] = jnp.max(s, axis=0, keepdims=True)

    def exponentials(g, size):
        m_old = m_sc[g]
        m_new = jnp.maximum(m_old, mblk_sc[g])
        alpha_sc[g] = jnp.exp2(m_old - m_new)
        p_sc[g, :size, :] = jnp.exp2(s_sc[g, :size, :] - m_new).astype(BF16)
        m_sc[g] = m_new

    def weighted_values(g, c0, size):
        acc_sc[g] = alpha_sc[g] * acc_sc[g] + jnp.dot(
            vt_ref[g, :, pl.ds(c0, size)], p_sc[g, :size, :], preferred_element_type=F32)

    @pl.when(is_ctx)
    def _():
        for g in range(A_KV_HEADS):
            scores(g, 0, n_ctx)
            exponentials(g, n_ctx)
            weighted_values(g, 0, n_ctx)

    @pl.when(jnp.logical_not(is_ctx))
    def _():
        first, second = 0, A_KV_HEADS - 1
        p_sc[second] = jnp.zeros(p_sc.shape[1:], BF16)
        alpha_sc[second] = jnp.ones(alpha_sc.shape[1:], F32)
        scores(first, 0, tc)

        def body(c, carry):
            c0 = pl.multiple_of(c * tc, tc)
            c_prev = pl.multiple_of(jnp.maximum(c - 1, 0) * tc, tc)
            c_next = pl.multiple_of(jnp.minimum(c + 1, n_chunks - 1) * tc, tc)
            scores(second, c0, tc)
            exponentials(first, tc)
            weighted_values(second, c_prev, tc)
            scores(first, c_next, tc)
            exponentials(second, tc)
            weighted_values(first, c0, tc)
            return carry

        lax.fori_loop(0, n_chunks, body, 0)
        weighted_values(second, (n_chunks - 1) * tc, tc)

    gate = _silu(g_ref[...])
    for h in range(A_Q_HEADS):
        g, r = divmod(h, A_GROUPS)
        at = acc_sc[g, :, r * tq:(r + 1) * tq].T
        sl = slice(h * HEAD_DIM, (h + 1) * HEAD_DIM)
        o_ref[:, sl] = at[:, :HEAD_DIM] / at[:, HEAD_DIM:HEAD_DIM + 1] * gate[:, sl]


def _key_chunk(n):
    for tc in (1280, 1024, 640, 512, 256, 128):
        if n % tc == 0:
            return tc
    raise ValueError(f"no key chunk for {n} rows")


def _flash(qz, k, vt, ga, n_ctx):
    n = k.shape[0]
    tq = ROW_TILE
    tc = _key_chunk(n)
    gq = A_GROUPS * tq
    assert A_KV_HEADS == 2 and n_ctx <= tc
    full = lambda a: pl.BlockSpec(a.shape, lambda i: (0,) * a.ndim,
                                  pipeline_mode=pl.Buffered(1))
    return pl.pallas_call(
        functools.partial(_flash_kernel, n_ctx=n_ctx, n_ctx_tiles=n_ctx // tq, tc=tc,
                          n_chunks=n // tc),
        grid=(n // tq,),
        in_specs=[pl.BlockSpec((A_Q_HEADS, tq, LANES), lambda i: (0, i, 0)),
                  full(k), full(vt),
                  pl.BlockSpec((tq, A_WIDTH), lambda i: (i, 0))],
        out_specs=pl.BlockSpec((tq, A_WIDTH), lambda i: (i, 0)),
        out_shape=jax.ShapeDtypeStruct((n, A_WIDTH), F32),
        scratch_shapes=[pltpu.VMEM((A_KV_HEADS, 1, gq), F32),
                        pltpu.VMEM((A_KV_HEADS, 1, gq), F32),
                        pltpu.VMEM((A_KV_HEADS, 1, gq), F32),
                        pltpu.VMEM((A_KV_HEADS, LANES, gq), F32),
                        pltpu.VMEM((A_KV_HEADS, tc, gq), F32),
                        pltpu.VMEM((A_KV_HEADS, tc, gq), BF16)],
        compiler_params=_cparams("arbitrary"),
        name="flash_attn",
    )(qz, k, vt, ga)


def _dn_prep_kernel(z_ref, zp_ref, zn_ref, r_ref, w_ref, par_ref, bd_ref,
                    q_o, k_o, v_o, bg_o, *, seq_starts, seq_ends):
    i = pl.program_id(0)
    z = z_ref[...]
    tm = z.shape[0]
    is_start = functools.reduce(jnp.logical_or, [i == s for s in seq_starts])
    is_end = functools.reduce(jnp.logical_or, [i == s for s in seq_ends])
    prev_row = jnp.where(is_start, 0.0, zp_ref[SUBLANES - 1:SUBLANES, :])
    next_row = jnp.where(is_end, 0.0, zn_ref[0:1, :])
    row = lax.broadcasted_iota(jnp.int32, (tm, 1), 0)
    z_prev = jnp.where(row == 0, prev_row, pltpu.roll(z, 1, 0))
    z_next = jnp.where(row == tm - 1, next_row, pltpu.roll(z, tm - 1, 0))
    w = w_ref[...]
    y = _silu(z_prev * w[0:1] + z * w[1:2] + z_next * w[2:3])
    bd = bd_ref[...]

    def l2norm(t):
        return t * lax.rsqrt(_mm(t * t, bd, 6) + NORM_EPS)

    q_o[...] = l2norm(y[:, :B_WIDTH]) * (HEAD_DIM ** -0.5)
    k_o[...] = l2norm(y[:, B_WIDTH:2 * B_WIDTH])
    v_o[...] = y[:, 2 * B_WIDTH:]
    r = r_ref[...]
    par = par_ref[...]
    xa = r + par[1:2]
    softplus = jnp.maximum(xa, 0.0) + jnp.log(1.0 + jnp.exp(-jnp.abs(xa)))
    g = -jnp.exp(par[0:1]) * softplus
    lane = lax.broadcasted_iota(jnp.int32, (1, LANES), 1)
    bg_o[...] = jnp.where(lane < 2 * B_HEADS, jax.nn.sigmoid(r), g)


def _dn_prep(qkvb, bdraw, conv_w, a_log, dt_bias, n_ctx):
    n = qkvb.shape[0]
    tm = ROW_TILE
    nt, nct = n // tm, n_ctx // tm
    r8 = tm // SUBLANES
    par = jnp.zeros((2, LANES), F32)
    par = par.at[0, 2 * B_HEADS:4 * B_HEADS].set(a_log.reshape(-1))
    par = par.at[1, 2 * B_HEADS:4 * B_HEADS].set(dt_bias.reshape(-1))
    bd = _block_diag_const(B_WIDTH, 1.0)
    w3 = 3 * B_WIDTH
    return pl.pallas_call(
        functools.partial(_dn_prep_kernel, seq_starts=(0, nct), seq_ends=(nct - 1, nt - 1)),
        grid=(nt,),
        in_specs=[pl.BlockSpec((tm, w3), lambda i: (i, 0)),
                  pl.BlockSpec((SUBLANES, w3), lambda i: (jnp.maximum(i * r8 - 1, 0), 0)),
                  pl.BlockSpec((SUBLANES, w3),
                               lambda i: (jnp.minimum((i + 1) * r8, n // SUBLANES - 1), 0)),
                  pl.BlockSpec((tm, LANES), lambda i: (i, 0)),
                  pl.BlockSpec(conv_w.shape, lambda i: (0, 0)),
                  pl.BlockSpec(par.shape, lambda i: (0, 0)),
                  pl.BlockSpec(bd.shape, lambda i: (0, 0))],
        out_specs=[pl.BlockSpec((tm, B_WIDTH), lambda i: (i, 0))] * 3
        + [pl.BlockSpec((tm, LANES), lambda i: (i, 0))],
        out_shape=[jax.ShapeDtypeStruct((n, B_WIDTH), F32)] * 3
        + [jax.ShapeDtypeStruct((n, LANES), F32)],
        compiler_params=_cparams("arbitrary"),
        name="dn_prep",
    )(qkvb, qkvb, qkvb, bdraw, conv_w, par, bd)


def _dn_scan_kernel(qf, kf, vf, bgf, qb, kb, vb, bgb, of_ref, ob_ref, s_ref, *, prec, cps):
    @pl.when(pl.program_id(0) == 0)
    def _():
        s_ref[...] = jnp.zeros(s_ref.shape, F32)

    c, hc = B_CHUNK, B_HEADS * B_CHUNK
    row = lax.broadcasted_iota(jnp.int32, (hc, hc), 0)
    col = lax.broadcasted_iota(jnp.int32, (hc, hc), 1)
    same_head = (row // c) == (col // c)
    eye = (row == col).astype(F32)
    level_masks = [(row // 2) == (col // 2)]
    bs = 2
    while bs < c:
        level_masks.append(jnp.logical_and((row // (2 * bs)) == (col // (2 * bs)),
                                           (row // bs) != (col // bs)))
        bs *= 2
    rc, cc = lax.broadcasted_iota(jnp.int32, (c, c), 0), lax.broadcasted_iota(jnp.int32, (c, c), 1)

    def heads_bd(x):
        return jnp.where(same_head, jnp.concatenate([x] * B_HEADS, axis=0), 0.0)

    def rows_of(slab, lane0):
        return jnp.concatenate(
            [jnp.broadcast_to(slab[:, lane0 + h:lane0 + h + 1], (c, hc)) for h in range(B_HEADS)],
            axis=0)

    def chunk_setup(d, q_ref, k_ref, v_ref, bg_ref, r0):
        rows = slice(r0, r0 + c)
        incl = (col <= row) if d == 0 else (col >= row)
        strict = (col < row) if d == 0 else (col > row)
        incl_c = (cc <= rc) if d == 0 else (cc >= rc)
        last = c - 1 if d == 0 else 0
        lb = d * B_HEADS
        lg = 2 * B_HEADS + lb
        bg = bg_ref[rows, :]
        gcum = _mm(incl_c.astype(F32), bg, 6)
        gcum_t = gcum.T
        g_row = jnp.concatenate([gcum_t[lg + h:lg + h + 1, :] for h in range(B_HEADS)], axis=1)
        g_last = gcum[last:last + 1, :]
        decay = jnp.where(jnp.logical_and(same_head, incl),
                          jnp.exp(rows_of(gcum, lg) - g_row), 0.0)
        beta = rows_of(bg, lb)
        eg = rows_of(jnp.exp(gcum), lg)
        to_last = rows_of(jnp.exp(g_last - gcum), lg)
        q_bd = heads_bd(q_ref[rows, :])
        k_bd = heads_bd(k_ref[rows, :])
        v_bd = heads_bd(v_ref[rows, :])
        k_b = k_bd.astype(BF16)
        kk = _mm(k_b, k_b, prec, _NT)
        qk = _mm(q_bd, k_b, prec, _NT)
        a_mat = jnp.where(jnp.logical_and(same_head, strict), kk * beta * decay, 0.0)
        g_state = jnp.concatenate(
            [jnp.broadcast_to(jnp.exp(g_last[:, lg + h:lg + h + 1]), (c, hc))
             for h in range(B_HEADS)], axis=0)
        return dict(a_mat=a_mat, t=eye - jnp.where(level_masks[0], a_mat, 0.0),
                    v_beta=(v_bd * beta).astype(BF16), k_beta=(k_bd * (beta * eg)).astype(BF16),
                    q_dec=(q_bd * eg).astype(BF16), attn=(qk * decay).astype(BF16),
                    k_dec=(k_bd * to_last).astype(BF16), g_state=g_state)

    dirs = ((qf, kf, vf, bgf, of_ref), (qb, kb, vb, bgb, ob_ref))
    chains = [chunk_setup(d, *dirs[d][:4], j * c) for j in range(cps) for d in range(2)]
    for lm in level_masks[1:]:
        part = [_mm(ch["t"], jnp.where(lm, ch["a_mat"], 0.0), prec) for ch in chains]
        for ch, pt in zip(chains, part):
            ch["t"] = ch["t"] - _mm(pt, ch["t"], prec)
    for ch in chains:
        t = ch["t"].astype(BF16)
        ch["w"] = _mm(t, ch["v_beta"], prec)
        ch["k_cum"] = _mm(t, ch["k_beta"], prec)

    states = [s_ref[d] for d in range(2)]
    for step in range(cps):
        sub = [step, cps - 1 - step]
        tms = [chains[sub[d] * 2 + d] for d in range(2)]
        us = [tms[d]["w"] - _mm(tms[d]["k_cum"], states[d], prec) for d in range(2)]
        for d in range(2):
            tm, u, s = tms[d], us[d], states[d]
            o_bd = _mm(tm["q_dec"], s, prec) + _mm(tm["attn"], u, prec)
            states[d] = s * tm["g_state"] + _mm(tm["k_dec"], u, prec, _TN)
            dirs[d][4][sub[d] * c:(sub[d] + 1) * c, :] = functools.reduce(
                jnp.add, [o_bd[h * c:(h + 1) * c] for h in range(B_HEADS)])
    for d in range(2):
        s_ref[d] = states[d]


def _scan_order_maps(n_chunks, n_ctx_chunks):
    fwd = lambda j: (j, 0)
    bwd = lambda j: (jnp.where(j < n_ctx_chunks, n_ctx_chunks - 1 - j,
                               n_chunks - 1 - (j - n_ctx_chunks)), 0)
    return fwd, bwd


def _dn_scan(qn, kn, vn, bg, n_ctx):
    n = qn.shape[0]
    assert B_CHUNK == HEAD_DIM
    c = B_CHUNK * DN_CHUNKS_PER_STEP
    fwd, bwd = _scan_order_maps(n // c, n_ctx // c)
    specs = lambda m: [pl.BlockSpec((c, B_WIDTH), m)] * 3 + [pl.BlockSpec((c, LANES), m)]
    return pl.pallas_call(
        functools.partial(_dn_scan_kernel, prec=DN_PREC, cps=DN_CHUNKS_PER_STEP),
        grid=(n // c,),
        in_specs=specs(fwd) + specs(bwd),
        out_specs=[pl.BlockSpec((c, B_WIDTH), fwd), pl.BlockSpec((c, B_WIDTH), bwd)],
        out_shape=[jax.ShapeDtypeStruct((n, B_WIDTH), F32)] * 2,
        scratch_shapes=[pltpu.VMEM((2, B_WIDTH, B_WIDTH), F32)],
        compiler_params=_cparams("arbitrary"),
        name="dn_scan",
    )(qn, kn, vn, bg, qn, kn, vn, bg)


def _s5_kernel(uf_ref, ub_ref, bre_ref, bim_ref, cre_ref, cimn_ref, a_ref, ap_ref,
               yf_ref, yb_ref, hre_sc, him_sc, carry_sc, *, prec):
    @pl.when(pl.program_id(0) == 0)
    def _():
        carry_sc[...] = jnp.zeros(carry_sc.shape, F32)

    steps, nsub = S5_STEPS, SUBLANES
    ts = steps * nsub
    prow = lax.broadcasted_iota(jnp.int32, (ts, ts), 0)
    pcol = lax.broadcasted_iota(jnp.int32, (ts, ts), 1)
    scan_pos = (prow % nsub) * steps + prow // nsub
    perms = [jnp.where(pcol == scan_pos, 1.0, 0.0).astype(BF16),
             jnp.where(pcol == ts - 1 - scan_pos, 1.0, 0.0).astype(BF16)]
    ndir = 2

    def permute(pm, x, dims, pieces):
        out, rest = None, x
        for _ in range(pieces):
            part = rest.astype(BF16)
            rest = rest - part.astype(F32)
            term = _dg(pm, part, dims)
            out = term if out is None else out + term
        return out

    for d, u_ref in enumerate((uf_ref, ub_ref)):
        u_perm = permute(perms[d], u_ref[...], ((1,), (0,)), 1 if prec == 1 else 2)
        hre_sc[d] = _mm(u_perm, bre_ref[d], prec)
        him_sc[d] = _mm(u_perm, bim_ref[d], prec)

    a = [(a_ref[d, 0:1, :], a_ref[d, 1:2, :]) for d in range(ndir)]

    def rows_at(jj):
        return pl.ds(pl.multiple_of(jj * nsub, nsub), nsub)

    def local_scan(jj, carry):
        rows = rows_at(jj)
        out = []
        for d in range(ndir):
            (a_re, a_im), (h_re, h_im) = a[d], carry[d]
            n_re = a_re * h_re - a_im * h_im + hre_sc[d, rows, :]
            n_im = a_re * h_im + a_im * h_re + him_sc[d, rows, :]
            hre_sc[d, rows, :] = n_re
            him_sc[d, rows, :] = n_im
            out.append((n_re, n_im))
        return tuple(out)

    zero = jnp.zeros((nsub, C_STATES), F32)
    ends = lax.fori_loop(0, steps, local_scan, ((zero, zero),) * ndir)
    cins = []
    for d in range(ndir):
        as_re = ap_ref[d, 0, steps - 1:steps, :]
        as_im = ap_ref[d, 1, steps - 1:steps, :]
        c_re, c_im = carry_sc[2 * d:2 * d + 1, :], carry_sc[2 * d + 1:2 * d + 2, :]
        cin = []
        for s in range(nsub):
            cin.append((c_re, c_im))
            c_re, c_im = (ends[d][0][s:s + 1] + as_re * c_re - as_im * c_im,
                          ends[d][1][s:s + 1] + as_re * c_im + as_im * c_re)
        carry_sc[2 * d:2 * d + 1, :] = c_re
        carry_sc[2 * d + 1:2 * d + 2, :] = c_im
        cins.append((jnp.concatenate([x[0] for x in cin], axis=0),
                     jnp.concatenate([x[1] for x in cin], axis=0)))

    def fixup(jj, carry):
        rows = rows_at(jj)
        for d in range(ndir):
            p_re = ap_ref[d, 0, pl.ds(jj, 1), :]
            p_im = ap_ref[d, 1, pl.ds(jj, 1), :]
            cin_re, cin_im = cins[d]
            hre_sc[d, rows, :] += p_re * cin_re - p_im * cin_im
            him_sc[d, rows, :] += p_re * cin_im + p_im * cin_re
        return carry

    lax.fori_loop(0, steps, fixup, 0)
    for d, y_ref in enumerate((yf_ref, yb_ref)):
        y_perm = _mm(hre_sc[d], cre_ref[d], prec) + _mm(him_sc[d], cimn_ref[d], prec)
        y_ref[...] = permute(perms[d], y_perm, _TN, 2)


def _s5_params(a_re, a_im, log_dt, b_re, b_im, c_re, c_im):
    dt = jnp.exp(log_dt.astype(F32))[..., None]
    lam_re, lam_im = a_re.astype(F32), a_im.astype(F32)
    k = jnp.arange(1, S5_STEPS + 1, dtype=F32)[None, :, None, None]
    mag = jnp.exp(lam_re[:, None] * dt[:, None] * k)
    ang = lam_im[:, None] * dt[:, None] * k
    ap = jnp.stack([mag * jnp.cos(ang), mag * jnp.sin(ang)], axis=1)
    ap = ap.reshape(2, 2, S5_STEPS, C_STATES)
    ab_re, ab_im = ap[:, 0, 0], ap[:, 1, 0]
    abar = jnp.stack([ab_re, ab_im], axis=1)
    x, y = ab_re.reshape(lam_re.shape) - 1.0, ab_im.reshape(lam_re.shape)
    den = lam_re * lam_re + lam_im * lam_im
    cf_re, cf_im = (x * lam_re + y * lam_im) / den, (y * lam_re - x * lam_im) / den
    bb_re = cf_re[..., None] * b_re - cf_im[..., None] * b_im
    bb_im = cf_re[..., None] * b_im + cf_im[..., None] * b_re
    eye = jnp.eye(C_GROUPS, dtype=F32)
    dense_b = lambda b: jnp.einsum('dgpc,gh->dgchp', b, eye).reshape(2, C_WIDTH, C_STATES)
    dense_c = lambda c: jnp.einsum('dgcp,gh->dgphc', c, eye).reshape(2, C_STATES, C_WIDTH)
    return (dense_b(bb_re), dense_b(bb_im), dense_c(c_re.astype(F32)),
            dense_c(-c_im.astype(F32)), abar, ap)


def _s5_scan(uc, params, n_ctx):
    n = uc.shape[0]
    ts = ROW_TILE
    bre, bim, cre, cimn, abar, ap = params
    fwd, bwd = _scan_order_maps(n // ts, n_ctx // ts)
    full = lambda a: pl.BlockSpec(a.shape, lambda j: (0,) * a.ndim)
    return pl.pallas_call(
        functools.partial(_s5_kernel, prec=S5_PREC),
        grid=(n // ts,),
        in_specs=[pl.BlockSpec((ts, C_WIDTH), fwd), pl.BlockSpec((ts, C_WIDTH), bwd),
                  full(bre), full(bim), full(cre), full(cimn), full(abar), full(ap)],
        out_specs=[pl.BlockSpec((ts, C_WIDTH), fwd), pl.BlockSpec((ts, C_WIDTH), bwd)],
        out_shape=[jax.ShapeDtypeStruct((n, C_WIDTH), F32)] * 2,
        scratch_shapes=[pltpu.VMEM((2, ts, C_STATES), F32), pltpu.VMEM((2, ts, C_STATES), F32),
                        pltpu.VMEM((SUBLANES, C_STATES), F32)],
        compiler_params=_cparams("arbitrary"),
        name="s5_scan",
    )(uc, uc, bre, bim, cre, cimn, abar, ap)


def _gelu_tanh(x):
    return 0.5 * x * (1.0 + jnp.tanh(math.sqrt(2.0 / math.pi) * (x + 0.044715 * (x * x * x))))


def _out_kernel(c_ref, l_ref, mod_ref, ao_ref, dof_ref, dob_ref, gb_ref, u_ref, yf_ref, yb_ref,
                gc_ref, vec_ref, bd_ref, gluw_ref, wo_ref, lnv_ref, o_ref, *, n_ctx_tiles,
                first_tile, d_model, alpha):
    is_ctx = pl.program_id(0) + first_tile < n_ctx_tiles
    vec = vec_ref[...]
    o = dof_ref[...] + dob_ref[...]
    ms = _mm(o * o, bd_ref[...], 6)
    dn = o * lax.rsqrt(ms + NORM_EPS) * vec[0:1] * _silu(gb_ref[...])
    z = _gelu_tanh(u_ref[...] * vec[1:2] + yf_ref[...] + yb_ref[...])
    lin = jnp.dot(z.astype(BF16), gluw_ref[...], preferred_element_type=F32) + vec[2:3]
    s5 = z * jax.nn.sigmoid(lin) * _silu(gc_ref[...])
    wo = wo_ref
    proj = (jnp.dot(ao_ref[...].astype(BF16), wo[:A_WIDTH, :], preferred_element_type=F32)
            + jnp.dot(dn.astype(BF16), wo[A_WIDTH:A_WIDTH + B_WIDTH, :],
                      preferred_element_type=F32)
            + jnp.dot(s5.astype(BF16), wo[A_WIDTH + B_WIDTH:, :], preferred_element_type=F32))
    gate = _mod_row(mod_ref, is_ctx)[:, 2 * d_model:]
    lnv = lnv_ref[...]
    x = jnp.where(is_ctx, c_ref[...], l_ref[...])
    o_ref[...] = _layer_norm(alpha * x + gate * proj) * lnv[0:1] + lnv[1:2]


def _out_proj(ctx_src, lat_src, lat_off, n, first_tile, mod_l, ao, dof, dob, gb, uc, yf, yb, gc,
              vec, gluw, wo, lnv, n_ctx_tiles, alpha):
    d = lat_src.shape[1]
    tm = ROW_TILE
    bd = _block_diag_const(B_WIDTH, 1.0 / HEAD_DIM)
    row = lambda w: pl.BlockSpec((tm, w), lambda i: (i + first_tile, 0))
    full = lambda a: pl.BlockSpec(a.shape, lambda i: (0,) * a.ndim)
    return pl.pallas_call(
        functools.partial(_out_kernel, n_ctx_tiles=n_ctx_tiles, first_tile=first_tile, d_model=d,
                          alpha=alpha),
        grid=(n // tm - first_tile,),
        in_specs=_row_source_specs(n_ctx_tiles, lat_off, d, first_tile)
        + [full(mod_l), row(A_WIDTH), row(B_WIDTH), row(B_WIDTH), row(B_WIDTH),
           row(C_WIDTH), row(C_WIDTH), row(C_WIDTH), row(C_WIDTH), full(vec), full(bd),
           full(gluw), full(wo), full(lnv)],
        out_specs=pl.BlockSpec((tm, d), lambda i: (i, 0)),
        out_shape=jax.ShapeDtypeStruct((n - first_tile * tm, d), F32),
        compiler_params=_cparams("arbitrary"),
        name="out_proj",
    )(ctx_src, lat_src, mod_l, ao, dof, dob, gb, uc, yf, yb, gc, vec, bd, gluw, wo, lnv)


def _rope_tables(n_lat, n_ctx):
    rows = n_lat // GRID_W
    row = jnp.repeat(jnp.arange(rows, dtype=F32), GRID_W)
    col = jnp.tile(jnp.arange(GRID_W, dtype=F32), rows)
    axis_dim = HEAD_DIM // 2
    inv = ROPE_THETA ** (-jnp.arange(0, axis_dim, 2, dtype=F32) / axis_dim)
    ar, ac = row[:, None] * inv, col[:, None] * inv
    ang = jnp.concatenate([ar, ar, ac, ac], -1)
    cos = jnp.concatenate([jnp.ones((n_ctx, HEAD_DIM), F32), jnp.cos(ang)], 0)
    sin = jnp.concatenate([jnp.zeros((n_ctx, HEAD_DIM), F32), jnp.sin(ang)], 0)
    upper = (jnp.arange(HEAD_DIM) % (HEAD_DIM // 2)) >= HEAD_DIM // 4
    sa = jnp.where(upper, 0.0, -sin)
    sb = jnp.where(upper, sin, 0.0)
    tile = lambda t: jnp.tile(t, (1, LANES // HEAD_DIM))
    return tile(cos), tile(sa), tile(sb)


def kernel(x, c, ctx, c_ctx, w_mod, b_mod, w_in, attn_q_gain, attn_k_gain, dn_conv_w, dn_A_log, dn_dt_bias, dn_out_gain, s5_A_re, s5_A_im, s5_log_dt, s5_B_re, s5_B_im, s5_C_re, s5_C_im, s5_D, glu_w, glu_b, w_out, ln_g, ln_b):
    batch, n_lat, d = x.shape
    n_ctx = ctx.shape[1]
    depth = w_mod.shape[0]
    assert batch == 1 and n_ctx % ROW_TILE == 0 and n_lat % ROW_TILE == 0 and n_lat % GRID_W == 0
    nct = n_ctx // ROW_TILE
    alpha = (2 * depth) ** 0.25

    n = n_ctx + n_lat
    cv = jnp.zeros((SUBLANES, d), F32).at[0].set(c[0]).at[1].set(c_ctx)
    mod = _modulation(cv, w_mod, b_mod)
    cos, sa, sb = _rope_tables(n_lat, n_ctx)
    lane_tile = lambda g: jnp.tile(g.astype(F32)[None, :], (1, LANES // HEAD_DIM))
    s5_par = jax.vmap(_s5_params)(s5_A_re, s5_A_im, s5_log_dt, s5_B_re, s5_B_im, s5_C_re, s5_C_im)

    ctx_src, lat_src, lat_off = ctx[0], x[0], 0
    for l in range(depth):
        (qa, ka, va, ga, qkvb, gb, bdraw, uc, gc) = _inproj(ctx_src, lat_src, lat_off, n, mod[l],
                                                            _pad_w_in(w_in[l]), nct)
        qz, kp, vt = _attn_prep(qa, ka, va, cos, sa, sb, lane_tile(attn_q_gain[l]),
                                lane_tile(attn_k_gain[l]))
        ao = _flash(qz, kp, vt, ga, n_ctx)
        qn, kn, vn, bg = _dn_prep(qkvb, bdraw, dn_conv_w[l].astype(F32), dn_A_log[l],
                                  dn_dt_bias[l], n_ctx)
        dof, dob = _dn_scan(qn, kn, vn, bg, n_ctx)
        yf, yb = _s5_scan(uc, tuple(p[l] for p in s5_par), n_ctx)
        vec = jnp.zeros((SUBLANES, C_WIDTH), F32)
        vec = vec.at[0].set(jnp.tile(dn_out_gain[l].astype(F32), B_HEADS))
        vec = vec.at[1].set(s5_D[l].astype(F32)).at[2].set(glu_b[l].astype(F32))
        lnv = jnp.zeros((SUBLANES, d), F32).at[0].set(ln_g[l]).at[1].set(ln_b[l])
        first_tile = nct if l == depth - 1 else 0
        xall = _out_proj(ctx_src, lat_src, lat_off, n, first_tile, mod[l], ao, dof, dob, gb, uc,
                         yf, yb, gc, vec, glu_w[l].astype(BF16), w_out[l].astype(BF16), lnv, nct,
                         alpha)
        ctx_src, lat_src, lat_off = xall, xall, nct
    return xall[None]
```

```python
import functools
import math

import jax
import jax.numpy as jnp
from jax import lax
from jax.experimental import pallas as pl
from jax.experimental.pallas import tpu as pltpu

F32 = jnp.float32
BF16 = jnp.bfloat16
HIGHEST = lax.Precision.HIGHEST

GRID_W = 64
HEAD_DIM = 64
NORM_EPS = 1e-6
ROPE_THETA = 10000.0
A_Q_HEADS = 8
A_KV_HEADS = 2
A_GROUPS = A_Q_HEADS // A_KV_HEADS
A_WIDTH = A_Q_HEADS * HEAD_DIM
A_KV_WIDTH = A_KV_HEADS * HEAD_DIM
B_HEADS = 4
B_WIDTH = B_HEADS * HEAD_DIM
B_CHUNK = 64
C_WIDTH = 256
C_GROUP = 16
C_GROUPS = C_WIDTH // C_GROUP
C_STATE = 64
C_STATES = C_GROUPS * C_STATE
LANES = 128
SUBLANES = 8
ROW_TILE = 256
S5_STEPS = ROW_TILE // SUBLANES
VMEM_LIMIT = 48 * 1024 * 1024
LOG2E = math.log2(math.e)

DN_PREC = 1
S5_PREC = 1
DN_CHUNKS_PER_STEP = 4


def _cparams(*sem):
    return pltpu.CompilerParams(dimension_semantics=sem, vmem_limit_bytes=VMEM_LIMIT)


def _split(a):
    hi = a.astype(BF16)
    lo = (a - hi.astype(F32)).astype(BF16)
    return hi, lo


def _dg(a, b, dims):
    return lax.dot_general(a, b, (dims, ((), ())), preferred_element_type=F32)


def _mm(a, b, prec, dims=((1,), (0,))):
    if prec == 6:
        return lax.dot_general(a, b, (dims, ((), ())), precision=HIGHEST,
                               preferred_element_type=F32)
    if prec == 1:
        return _dg(a.astype(BF16), b.astype(BF16), dims)
    ah, al = _split(a)
    bh, bl = _split(b)
    return _dg(ah, bh, dims) + (_dg(ah, bl, dims) + _dg(al, bh, dims))


_NT = ((1,), (1,))
_TN = ((0,), (0,))


def _silu(x):
    return x * jax.nn.sigmoid(x)


def _mod_kernel(cv_ref, w_ref, b_ref, o_ref):
    sv = _silu(cv_ref[...])
    o_ref[0] = _mm(sv, w_ref[0], 6) + b_ref[0]


def _modulation(cv, w_mod, b_mod):
    depth, d, d3 = w_mod.shape
    nblk = d3 // d
    return pl.pallas_call(
        _mod_kernel,
        grid=(depth, nblk),
        in_specs=[pl.BlockSpec((SUBLANES, d), lambda l, n: (0, 0)),
                  pl.BlockSpec((1, d, d), lambda l, n: (l, 0, n)),
                  pl.BlockSpec((1, 1, d), lambda l, n: (l, 0, n))],
        out_specs=pl.BlockSpec((1, SUBLANES, d), lambda l, n: (l, 0, n)),
        out_shape=jax.ShapeDtypeStruct((depth, SUBLANES, d3), F32),
        compiler_params=_cparams("arbitrary", "arbitrary"),
        name="modulation",
    )(cv, w_mod, b_mod.reshape(depth, 1, d3))


def _layer_norm(x):
    mu = jnp.mean(x, axis=-1, keepdims=True)
    xc = x - mu
    var = jnp.mean(xc * xc, axis=-1, keepdims=True)
    return xc * lax.rsqrt(var + NORM_EPS)


def _mod_row(mod_ref, is_ctx):
    mod = mod_ref[...]
    return jnp.where(is_ctx, mod[1:2], mod[0:1])


_ATTN_QKV_WIDTH = A_WIDTH + 2 * A_KV_WIDTH
_IN_SPLITS = (("ga", A_WIDTH), ("qkvb", 3 * B_WIDTH), ("gb", B_WIDTH), ("bd", LANES),
              ("uc", C_WIDTH), ("gc", C_WIDTH))


def _row_source_specs(n_ctx_tiles, lat_off, d, first_tile=0):
    tm = ROW_TILE
    return [pl.BlockSpec((tm, d), lambda i: (jnp.minimum(i + first_tile, n_ctx_tiles - 1), 0)),
            pl.BlockSpec((tm, d),
                         lambda i: (jnp.maximum(i + first_tile - n_ctx_tiles, 0) + lat_off, 0))]


def _attn_prep_tile(q, k, v, cos, sa, sb, gq, gk, bd, qo_ref, ko_ref, vo_ref):
    def norm_rope(y, gain):
        ms = _mm(y * y, bd, 6)
        yn = y * lax.rsqrt(ms + NORM_EPS) * gain
        return (yn * cos + pltpu.roll(yn, LANES - HEAD_DIM // 4, 1) * sa
                + pltpu.roll(yn, HEAD_DIM // 4, 1) * sb)

    lane = lax.broadcasted_iota(jnp.int32, (1, LANES), 1)
    lower = lane < HEAD_DIM
    for j in range(A_Q_HEADS // 2):
        y = norm_rope(q[:, j * LANES:(j + 1) * LANES], gq) * (HEAD_DIM ** -0.5 * LOG2E)
        y_sw = pltpu.roll(y, HEAD_DIM, 1)
        for half in range(2):
            h = 2 * j + half
            src = y if (h // A_GROUPS) == half else y_sw
            keep = lower if (h // A_GROUPS) == 0 else jnp.logical_not(lower)
            qo_ref[h] = jnp.where(keep, src, 0.0).astype(BF16)
    ko_ref[...] = norm_rope(k, gk).astype(BF16)
    vt = v.T
    ones = jnp.ones((HEAD_DIM, vt.shape[1]), F32)
    for h in range(A_KV_HEADS):
        vo_ref[h] = jnp.concatenate([vt[h * HEAD_DIM:(h + 1) * HEAD_DIM], ones],
                                    axis=0).astype(BF16)


def _inproj_kernel(c_ref, l_ref, mod_ref, w_ref, cos_ref, sa_ref, sb_ref, gq_ref, gk_ref, bd_ref,
                   qo_ref, ko_ref, vo_ref, *out_refs, n_ctx_tiles, d_model):
    is_ctx = pl.program_id(0) < n_ctx_tiles
    row = _mod_row(mod_ref, is_ctx)
    shift, scale = row[:, :d_model], row[:, d_model:2 * d_model]
    x = jnp.where(is_ctx, c_ref[...], l_ref[...])
    h = _layer_norm(x) * (1.0 + scale) + shift
    z = jnp.dot(h.astype(BF16), w_ref[...], preferred_element_type=F32)
    off = _ATTN_QKV_WIDTH
    for ref, (_, w) in zip(out_refs, _IN_SPLITS):
        ref[...] = z[:, off:off + w]
        off += w
    _attn_prep_tile(z[:, :A_WIDTH], z[:, A_WIDTH:A_WIDTH + A_KV_WIDTH],
                    z[:, A_WIDTH + A_KV_WIDTH:_ATTN_QKV_WIDTH], cos_ref[...], sa_ref[...],
                    sb_ref[...], gq_ref[...], gk_ref[...], bd_ref[...], qo_ref, ko_ref, vo_ref)


def _pad_w_in(w):
    d = w.shape[0]
    n_main = 2 * A_WIDTH + 2 * A_KV_WIDTH + 4 * B_WIDTH
    n_small = 4 * B_HEADS
    return jnp.concatenate(
        [w[:, :n_main], w[:, n_main:n_main + n_small], jnp.zeros((d, LANES - n_small), w.dtype),
         w[:, n_main + n_small:]], axis=1).astype(BF16)


def _block_diag_const(width, value):
    r = jnp.arange(width) // HEAD_DIM
    return jnp.where(r[:, None] == r[None, :], value, 0.0).astype(F32)


def _inproj(ctx_src, lat_src, lat_off, n, mod_l, w_pad, rope, gq, gk, n_ctx_tiles):
    d = lat_src.shape[1]
    tm = ROW_TILE
    bd = _block_diag_const(LANES, 1.0 / HEAD_DIM)
    row = lambda w: pl.BlockSpec((tm, w), lambda i: (i, 0))
    full = lambda a: pl.BlockSpec(a.shape, lambda i: (0,) * a.ndim)
    return pl.pallas_call(
        functools.partial(_inproj_kernel, n_ctx_tiles=n_ctx_tiles, d_model=d),
        grid=(n // tm,),
        in_specs=_row_source_specs(n_ctx_tiles, lat_off, d)
        + [full(mod_l), full(w_pad), row(LANES), row(LANES), row(LANES), full(gq), full(gk),
           full(bd)],
        out_specs=[pl.BlockSpec((A_Q_HEADS, tm, LANES), lambda i: (0, i, 0)),
                   row(LANES),
                   pl.BlockSpec((A_KV_HEADS, LANES, tm), lambda i: (0, 0, i))]
        + [row(w) for _, w in _IN_SPLITS],
        out_shape=[jax.ShapeDtypeStruct((A_Q_HEADS, n, LANES), BF16),
                   jax.ShapeDtypeStruct((n, LANES), BF16),
                   jax.ShapeDtypeStruct((A_KV_HEADS, LANES, n), BF16)]
        + [jax.ShapeDtypeStruct((n, w), F32) for _, w in _IN_SPLITS],
        compiler_params=_cparams("arbitrary"),
        name="inproj",
    )(ctx_src, lat_src, mod_l, w_pad, *rope, gq, gk, bd)


def _flash_kernel(q_ref, k_ref, vt_ref, g_ref, o_ref, m_sc, mblk_sc, alpha_sc, acc_sc, s_sc, p_sc,
                  *, n_ctx, n_ctx_tiles, tc, n_chunks):
    is_ctx = pl.program_id(0) < n_ctx_tiles
    tq = q_ref.shape[1]
    m_sc[...] = jnp.full(m_sc.shape, -jnp.inf, F32)
    acc_sc[...] = jnp.zeros(acc_sc.shape, F32)

    def q_group(g):
        return q_ref[g * A_GROUPS:(g + 1) * A_GROUPS].reshape(A_GROUPS * tq, LANES)

    def scores(g, c0, size):
        s = _dg(k_ref[pl.ds(c0, size), :], q_group(g), _NT)
        s_sc[g, :size, :] = s
        mblk_sc[g] = jnp.max(s, axis=0, keepdims=True)

    def exponentials(g, size):
        m_old = m_sc[g]
        m_new = jnp.maximum(m_old, mblk_sc[g])
        alpha_sc[g] = jnp.exp2(m_old - m_new)
        p_sc[g, :size, :] = jnp.exp2(s_sc[g, :size, :] - m_new).astype(BF16)
        m_sc[g] = m_new

    def weighted_values(g, c0, size):
        acc_sc[g] = alpha_sc[g] * acc_sc[g] + jnp.dot(
            vt_ref[g, :, pl.ds(c0, size)], p_sc[g, :size, :], preferred_element_type=F32)

    @pl.when(is_ctx)
    def _():
        for g in range(A_KV_HEADS):
            scores(g, 0, n_ctx)
            exponentials(g, n_ctx)
            weighted_values(g, 0, n_ctx)

    @pl.when(jnp.logical_not(is_ctx))
    def _():
        first, second = 0, A_KV_HEADS - 1
        p_sc[second] = jnp.zeros(p_sc.shape[1:], BF16)
        alpha_sc[second] = jnp.ones(alpha_sc.shape[1:], F32)
        scores(first, 0, tc)

        def body(c, carry):
            c0 = pl.multiple_of(c * tc, tc)
            c_prev = pl.multiple_of(jnp.maximum(c - 1, 0) * tc, tc)
            c_next = pl.multiple_of(jnp.minimum(c + 1, n_chunks - 1) * tc, tc)
            scores(second, c0, tc)
            exponentials(first, tc)
            weighted_values(second, c_prev, tc)
            scores(first, c_next, tc)
            exponentials(second, tc)
            weighted_values(first, c0, tc)
            return carry

        lax.fori_loop(0, n_chunks, body, 0)
        weighted_values(second, (n_chunks - 1) * tc, tc)

    gate = _silu(g_ref[...])
    for h in range(A_Q_HEADS):
        g, r = divmod(h, A_GROUPS)
        acc = acc_sc[g, :, r * tq:(r + 1) * tq]
        sl = slice(h * HEAD_DIM, (h + 1) * HEAD_DIM)
        o_ref[:, sl] = (acc[:HEAD_DIM] / acc[HEAD_DIM:HEAD_DIM + 1]).T * gate[:, sl]


def _key_chunk(n):
    for tc in (1280, 1024, 640, 512, 256, 128):
        if n % tc == 0:
            return tc
    raise ValueError(f"no key chunk for {n} rows")


def _flash(qz, k, vt, ga, n_ctx):
    n = k.shape[0]
    tq = ROW_TILE
    tc = _key_chunk(n)
    gq = A_GROUPS * tq
    assert A_KV_HEADS == 2 and n_ctx <= tc
    full = lambda a: pl.BlockSpec(a.shape, lambda i: (0,) * a.ndim,
                                  pipeline_mode=pl.Buffered(1))
    return pl.pallas_call(
        functools.partial(_flash_kernel, n_ctx=n_ctx, n_ctx_tiles=n_ctx // tq, tc=tc,
                          n_chunks=n // tc),
        grid=(n // tq,),
        in_specs=[pl.BlockSpec((A_Q_HEADS, tq, LANES), lambda i: (0, i, 0)),
                  full(k), full(vt),
                  pl.BlockSpec((tq, A_WIDTH), lambda i: (i, 0))],
        out_specs=pl.BlockSpec((tq, A_WIDTH), lambda i: (i, 0)),
        out_shape=jax.ShapeDtypeStruct((n, A_WIDTH), F32),
        scratch_shapes=[pltpu.VMEM((A_KV_HEADS, 1, gq), F32),
                        pltpu.VMEM((A_KV_HEADS, 1, gq), F32),
                        pltpu.VMEM((A_KV_HEADS, 1, gq), F32),
                        pltpu.VMEM((A_KV_HEADS, LANES, gq), F32),
                        pltpu.VMEM((A_KV_HEADS, tc, gq), F32),
                        pltpu.VMEM((A_KV_HEADS, tc, gq), BF16)],
        compiler_params=_cparams("arbitrary"),
        name="flash_attn",
    )(qz, k, vt, ga)


def _dn_prep_kernel(z_ref, zp_ref, zn_ref, r_ref, w_ref, par_ref, bd_ref,
                    q_o, k_o, v_o, bg_o, *, seq_starts, seq_ends):
    i = pl.program_id(0)
    z = z_ref[...]
    tm = z.shape[0]
    is_start = functools.reduce(jnp.logical_or, [i == s for s in seq_starts])
    is_end = functools.reduce(jnp.logical_or, [i == s for s in seq_ends])
    prev_row = jnp.where(is_start, 0.0, zp_ref[SUBLANES - 1:SUBLANES, :])
    next_row = jnp.where(is_end, 0.0, zn_ref[0:1, :])
    row = lax.broadcasted_iota(jnp.int32, (tm, 1), 0)
    z_prev = jnp.where(row == 0, prev_row, pltpu.roll(z, 1, 0))
    z_next = jnp.where(row == tm - 1, next_row, pltpu.roll(z, tm - 1, 0))
    w = w_ref[...]
    y = _silu(z_prev * w[0:1] + z * w[1:2] + z_next * w[2:3])
    bd = bd_ref[...]

    def l2norm(t):
        return t * lax.rsqrt(_mm(t * t, bd, 6) + NORM_EPS)

    q_o[...] = l2norm(y[:, :B_WIDTH]) * (HEAD_DIM ** -0.5)
    k_o[...] = l2norm(y[:, B_WIDTH:2 * B_WIDTH])
    v_o[...] = y[:, 2 * B_WIDTH:]
    r = r_ref[...]
    par = par_ref[...]
    xa = r + par[1:2]
    softplus = jnp.maximum(xa, 0.0) + jnp.log(1.0 + jnp.exp(-jnp.abs(xa)))
    g = -jnp.exp(par[0:1]) * softplus
    lane = lax.broadcasted_iota(jnp.int32, (1, LANES), 1)
    bg_o[...] = jnp.where(lane < 2 * B_HEADS, jax.nn.sigmoid(r), g)


def _dn_prep(qkvb, bdraw, conv_w, a_log, dt_bias, n_ctx):
    n = qkvb.shape[0]
    tm = ROW_TILE
    nt, nct = n // tm, n_ctx // tm
    r8 = tm // SUBLANES
    par = jnp.zeros((2, LANES), F32)
    par = par.at[0, 2 * B_HEADS:4 * B_HEADS].set(a_log.reshape(-1))
    par = par.at[1, 2 * B_HEADS:4 * B_HEADS].set(dt_bias.reshape(-1))
    bd = _block_diag_const(B_WIDTH, 1.0)
    w3 = 3 * B_WIDTH
    return pl.pallas_call(
        functools.partial(_dn_prep_kernel, seq_starts=(0, nct), seq_ends=(nct - 1, nt - 1)),
        grid=(nt,),
        in_specs=[pl.BlockSpec((tm, w3), lambda i: (i, 0)),
                  pl.BlockSpec((SUBLANES, w3), lambda i: (jnp.maximum(i * r8 - 1, 0), 0)),
                  pl.BlockSpec((SUBLANES, w3),
                               lambda i: (jnp.minimum((i + 1) * r8, n // SUBLANES - 1), 0)),
                  pl.BlockSpec((tm, LANES), lambda i: (i, 0)),
                  pl.BlockSpec(conv_w.shape, lambda i: (0, 0)),
                  pl.BlockSpec(par.shape, lambda i: (0, 0)),
                  pl.BlockSpec(bd.shape, lambda i: (0, 0))],
        out_specs=[pl.BlockSpec((tm, B_WIDTH), lambda i: (i, 0))] * 3
        + [pl.BlockSpec((tm, LANES), lambda i: (i, 0))],
        out_shape=[jax.ShapeDtypeStruct((n, B_WIDTH), F32)] * 3
        + [jax.ShapeDtypeStruct((n, LANES), F32)],
        compiler_params=_cparams("arbitrary"),
        name="dn_prep",
    )(qkvb, qkvb, qkvb, bdraw, conv_w, par, bd)


def _dn_scan_kernel(qf, kf, vf, bgf, qb, kb, vb, bgb, of_ref, ob_ref, s_ref, *, prec, cps):
    @pl.when(pl.program_id(0) == 0)
    def _():
        s_ref[...] = jnp.zeros(s_ref.shape, F32)

    c, hc = B_CHUNK, B_HEADS * B_CHUNK
    row = lax.broadcasted_iota(jnp.int32, (hc, hc), 0)
    col = lax.broadcasted_iota(jnp.int32, (hc, hc), 1)
    same_head = (row // c) == (col // c)
    eye = (row == col).astype(F32)
    level_masks = [(row // 2) == (col // 2)]
    bs = 2
    while bs < c:
        level_masks.append(jnp.logical_and((row // (2 * bs)) == (col // (2 * bs)),
                                           (row // bs) != (col // bs)))
        bs *= 2
    rc, cc = lax.broadcasted_iota(jnp.int32, (c, c), 0), lax.broadcasted_iota(jnp.int32, (c, c), 1)

    def heads_bd(x):
        return jnp.where(same_head, jnp.concatenate([x] * B_HEADS, axis=0), 0.0)

    def rows_of(slab, lane0):
        return jnp.concatenate(
            [jnp.broadcast_to(slab[:, lane0 + h:lane0 + h + 1], (c, hc)) for h in range(B_HEADS)],
            axis=0)

    def chunk_setup(d, q_ref, k_ref, v_ref, bg_ref, r0):
        rows = slice(r0, r0 + c)
        incl = (col <= row) if d == 0 else (col >= row)
        strict = (col < row) if d == 0 else (col > row)
        incl_c = (cc <= rc) if d == 0 else (cc >= rc)
        last = c - 1 if d == 0 else 0
        lb = d * B_HEADS
        lg = 2 * B_HEADS + lb
        bg = bg_ref[rows, :]
        gcum = _mm(incl_c.astype(F32), bg, 6)
        gcum_t = gcum.T
        g_row = jnp.concatenate([gcum_t[lg + h:lg + h + 1, :] for h in range(B_HEADS)], axis=1)
        g_last = gcum[last:last + 1, :]
        decay = jnp.where(jnp.logical_and(same_head, incl),
                          jnp.exp(rows_of(gcum, lg) - g_row), 0.0)
        beta = rows_of(bg, lb)
        eg = rows_of(jnp.exp(gcum), lg)
        to_last = rows_of(jnp.exp(g_last - gcum), lg)
        q_bd = heads_bd(q_ref[rows, :])
        k_bd = heads_bd(k_ref[rows, :])
        v_bd = heads_bd(v_ref[rows, :])
        k_b = k_bd.astype(BF16)
        kk = _mm(k_b, k_b, prec, _NT)
        qk = _mm(q_bd, k_b, prec, _NT)
        a_mat = jnp.where(jnp.logical_and(same_head, strict), kk * beta * decay, 0.0)
        g_state = jnp.concatenate(
            [jnp.broadcast_to(jnp.exp(g_last[:, lg + h:lg + h + 1]), (c, hc))
             for h in range(B_HEADS)], axis=0)
        return dict(a_mat=a_mat, t=eye - jnp.where(level_masks[0], a_mat, 0.0),
                    v_beta=(v_bd * beta).astype(BF16), k_beta=(k_bd * (beta * eg)).astype(BF16),
                    q_dec=(q_bd * eg).astype(BF16), attn=(qk * decay).astype(BF16),
                    k_dec=(k_bd * to_last).astype(BF16), g_state=g_state)

    dirs = ((qf, kf, vf, bgf, of_ref), (qb, kb, vb, bgb, ob_ref))
    chains = [chunk_setup(d, *dirs[d][:4], j * c) for j in range(cps) for d in range(2)]
    for lm in level_masks[1:]:
        part = [_mm(ch["t"], jnp.where(lm, ch["a_mat"], 0.0), prec) for ch in chains]
        for ch, pt in zip(chains, part):
            ch["t"] = ch["t"] - _mm(pt, ch["t"], prec)
    for ch in chains:
        t = ch["t"].astype(BF16)
        ch["w"] = _mm(t, ch["v_beta"], prec)
        ch["k_cum"] = _mm(t, ch["k_beta"], prec)

    states = [s_ref[d] for d in range(2)]
    for step in range(cps):
        sub = [step, cps - 1 - step]
        tms = [chains[sub[d] * 2 + d] for d in range(2)]
        us = [tms[d]["w"] - _mm(tms[d]["k_cum"], states[d], prec) for d in range(2)]
        for d in range(2):
            tm, u, s = tms[d], us[d], states[d]
            o_bd = _mm(tm["q_dec"], s, prec) + _mm(tm["attn"], u, prec)
            states[d] = s * tm["g_state"] + _mm(tm["k_dec"], u, prec, _TN)
            dirs[d][4][sub[d] * c:(sub[d] + 1) * c, :] = functools.reduce(
                jnp.add, [o_bd[h * c:(h + 1) * c] for h in range(B_HEADS)])
    for d in range(2):
        s_ref[d] = states[d]


def _scan_order_maps(n_chunks, n_ctx_chunks):
    fwd = lambda j: (j, 0)
    bwd = lambda j: (jnp.where(j < n_ctx_chunks, n_ctx_chunks - 1 - j,
                               n_chunks - 1 - (j - n_ctx_chunks)), 0)
    return fwd, bwd


def _dn_scan(qn, kn, vn, bg, n_ctx):
    n = qn.shape[0]
    assert B_CHUNK == HEAD_DIM
    c = B_CHUNK * DN_CHUNKS_PER_STEP
    fwd, bwd = _scan_order_maps(n // c, n_ctx // c)
    specs = lambda m: [pl.BlockSpec((c, B_WIDTH), m)] * 3 + [pl.BlockSpec((c, LANES), m)]
    return pl.pallas_call(
        functools.partial(_dn_scan_kernel, prec=DN_PREC, cps=DN_CHUNKS_PER_STEP),
        grid=(n // c,),
        in_specs=specs(fwd) + specs(bwd),
        out_specs=[pl.BlockSpec((c, B_WIDTH), fwd), pl.BlockSpec((c, B_WIDTH), bwd)],
        out_shape=[jax.ShapeDtypeStruct((n, B_WIDTH), F32)] * 2,
        scratch_shapes=[pltpu.VMEM((2, B_WIDTH, B_WIDTH), F32)],
        compiler_params=_cparams("arbitrary"),
        name="dn_scan",
    )(qn, kn, vn, bg, qn, kn, vn, bg)


def _s5_kernel(uf_ref, ub_ref, bre_ref, bim_ref, cre_ref, cimn_ref, a_ref, ap_ref,
               yf_ref, yb_ref, hre_sc, him_sc, carry_sc, *, prec):
    @pl.when(pl.program_id(0) == 0)
    def _():
        carry_sc[...] = jnp.zeros(carry_sc.shape, F32)

    steps, nsub = S5_STEPS, SUBLANES
    ts = steps * nsub
    prow = lax.broadcasted_iota(jnp.int32, (ts, ts), 0)
    pcol = lax.broadcasted_iota(jnp.int32, (ts, ts), 1)
    scan_pos = (prow % nsub) * steps + prow // nsub
    perms = [jnp.where(pcol == scan_pos, 1.0, 0.0).astype(BF16),
             jnp.where(pcol == ts - 1 - scan_pos, 1.0, 0.0).astype(BF16)]
    ndir = 2

    def permute(pm, x, dims, pieces):
        out, rest = None, x
        for _ in range(pieces):
            part = rest.astype(BF16)
            rest = rest - part.astype(F32)
            term = _dg(pm, part, dims)
            out = term if out is None else out + term
        return out

    u_perms = [permute(perms[d], u_ref[...], ((1,), (0,)), 1 if prec == 1 else 2)
               for d, u_ref in enumerate((uf_ref, ub_ref))]
    for d in range(ndir):
        hre_sc[d] = _mm(u_perms[d], bre_ref[d], prec)
    for d in range(ndir):
        him_sc[d] = _mm(u_perms[d], bim_ref[d], prec)

    def rows_at(jj):
        return pl.ds(pl.multiple_of(jj * nsub, nsub), nsub)

    def local_scan(jj, carry):
        rows = rows_at(jj)
        out = []
        for d in range(ndir):
            a_re, a_im = a_ref[d, 0], a_ref[d, 1]
            h_re, h_im = carry[d]
            n_re = a_re * h_re - a_im * h_im + hre_sc[d, rows, :]
            n_im = a_re * h_im + a_im * h_re + him_sc[d, rows, :]
            hre_sc[d, rows, :] = n_re
            him_sc[d, rows, :] = n_im
            out.append((n_re, n_im))
        return tuple(out)

    zero = jnp.zeros((nsub, C_STATES), F32)
    ends = lax.fori_loop(0, steps, local_scan, ((zero, zero),) * ndir)
    cins = []
    for d in range(ndir):
        as_re = ap_ref[d, 0, steps - 1, 0:1, :]
        as_im = ap_ref[d, 1, steps - 1, 0:1, :]
        c_re, c_im = carry_sc[2 * d:2 * d + 1, :], carry_sc[2 * d + 1:2 * d + 2, :]
        cin = []
        for s in range(nsub):
            cin.append((c_re, c_im))
            c_re, c_im = (ends[d][0][s:s + 1] + as_re * c_re - as_im * c_im,
                          ends[d][1][s:s + 1] + as_re * c_im + as_im * c_re)
        carry_sc[2 * d:2 * d + 1, :] = c_re
        carry_sc[2 * d + 1:2 * d + 2, :] = c_im
        cins.append((jnp.concatenate([x[0] for x in cin], axis=0),
                     jnp.concatenate([x[1] for x in cin], axis=0)))

    def fixup(jj, carry):
        rows = rows_at(jj)
        for d in range(ndir):
            p_re = ap_ref[d, 0, jj]
            p_im = ap_ref[d, 1, jj]
            cin_re, cin_im = cins[d]
            hre_sc[d, rows, :] += p_re * cin_re - p_im * cin_im
            him_sc[d, rows, :] += p_re * cin_im + p_im * cin_re
        return carry

    lax.fori_loop(0, steps, fixup, 0)
    y_re = [_mm(hre_sc[d], cre_ref[d], prec) for d in range(ndir)]
    y_im = [_mm(him_sc[d], cimn_ref[d], prec) for d in range(ndir)]
    for d, y_ref in enumerate((yf_ref, yb_ref)):
        y_ref[...] = permute(perms[d], y_re[d] + y_im[d], _TN, 2)


def _s5_params(a_re, a_im, log_dt, b_re, b_im, c_re, c_im):
    dt = jnp.exp(log_dt.astype(F32))[..., None]
    lam_re, lam_im = a_re.astype(F32), a_im.astype(F32)
    k = jnp.arange(1, S5_STEPS + 1, dtype=F32)[None, :, None, None]
    mag = jnp.exp(lam_re[:, None] * dt[:, None] * k)
    ang = lam_im[:, None] * dt[:, None] * k
    ap = jnp.stack([mag * jnp.cos(ang), mag * jnp.sin(ang)], axis=1)
    ap = ap.reshape(2, 2, S5_STEPS, C_STATES)
    ab_re, ab_im = ap[:, 0, 0], ap[:, 1, 0]
    abar = jnp.stack([ab_re, ab_im], axis=1)
    x, y = ab_re.reshape(lam_re.shape) - 1.0, ab_im.reshape(lam_re.shape)
    den = lam_re * lam_re + lam_im * lam_im
    cf_re, cf_im = (x * lam_re + y * lam_im) / den, (y * lam_re - x * lam_im) / den
    bb_re = cf_re[..., None] * b_re - cf_im[..., None] * b_im
    bb_im = cf_re[..., None] * b_im + cf_im[..., None] * b_re
    eye = jnp.eye(C_GROUPS, dtype=F32)
    dense_b = lambda b: jnp.einsum('dgpc,gh->dgchp', b, eye).reshape(2, C_WIDTH, C_STATES)
    dense_c = lambda c: jnp.einsum('dgcp,gh->dgphc', c, eye).reshape(2, C_STATES, C_WIDTH)
    abar = jnp.broadcast_to(abar[:, :, None, :], (2, 2, SUBLANES, C_STATES))
    ap = jnp.broadcast_to(ap[:, :, :, None, :], (2, 2, S5_STEPS, SUBLANES, C_STATES))
    return (dense_b(bb_re), dense_b(bb_im), dense_c(c_re.astype(F32)),
            dense_c(-c_im.astype(F32)), abar, ap)


def _s5_scan(uc, params, n_ctx):
    n = uc.shape[0]
    ts = ROW_TILE
    bre, bim, cre, cimn, abar, ap = params
    fwd, bwd = _scan_order_maps(n // ts, n_ctx // ts)
    full = lambda a: pl.BlockSpec(a.shape, lambda j: (0,) * a.ndim,
                                  pipeline_mode=pl.Buffered(1))
    return pl.pallas_call(
        functools.partial(_s5_kernel, prec=S5_PREC),
        grid=(n // ts,),
        in_specs=[pl.BlockSpec((ts, C_WIDTH), fwd), pl.BlockSpec((ts, C_WIDTH), bwd),
                  full(bre), full(bim), full(cre), full(cimn), full(abar), full(ap)],
        out_specs=[pl.BlockSpec((ts, C_WIDTH), fwd), pl.BlockSpec((ts, C_WIDTH), bwd)],
        out_shape=[jax.ShapeDtypeStruct((n, C_WIDTH), F32)] * 2,
        scratch_shapes=[pltpu.VMEM((2, ts, C_STATES), F32), pltpu.VMEM((2, ts, C_STATES), F32),
                        pltpu.VMEM((SUBLANES, C_STATES), F32)],
        compiler_params=_cparams("arbitrary"),
        name="s5_scan",
    )(uc, uc, bre, bim, cre, cimn, abar, ap)


def _gelu_tanh(x):
    return 0.5 * x * (1.0 + jnp.tanh(math.sqrt(2.0 / math.pi) * (x + 0.044715 * (x * x * x))))


def _out_kernel(c_ref, l_ref, mod_ref, ao_ref, dof_ref, dob_ref, gb_ref, u_ref, yf_ref, yb_ref,
                gc_ref, vec_ref, bd_ref, gluw_ref, wo_ref, lnv_ref, o_ref, *, n_ctx_tiles,
                first_tile, d_model, alpha):
    is_ctx = pl.program_id(0) + first_tile < n_ctx_tiles
    vec = vec_ref[...]
    o = dof_ref[...] + dob_ref[...]
    ms = _mm(o * o, bd_ref[...], 6)
    dn = o * lax.rsqrt(ms + NORM_EPS) * vec[0:1] * _silu(gb_ref[...])
    z = _gelu_tanh(u_ref[...] * vec[1:2] + yf_ref[...] + yb_ref[...])
    lin = jnp.dot(z.astype(BF16), gluw_ref[...], preferred_element_type=F32) + vec[2:3]
    s5 = z * jax.nn.sigmoid(lin) * _silu(gc_ref[...])
    wo = wo_ref
    proj = (jnp.dot(ao_ref[...].astype(BF16), wo[:A_WIDTH, :], preferred_element_type=F32)
            + jnp.dot(dn.astype(BF16), wo[A_WIDTH:A_WIDTH + B_WIDTH, :],
                      preferred_element_type=F32)
            + jnp.dot(s5.astype(BF16), wo[A_WIDTH + B_WIDTH:, :], preferred_element_type=F32))
    gate = _mod_row(mod_ref, is_ctx)[:, 2 * d_model:]
    lnv = lnv_ref[...]
    x = jnp.where(is_ctx, c_ref[...], l_ref[...])
    o_ref[...] = _layer_norm(alpha * x + gate * proj) * lnv[0:1] + lnv[1:2]


def _out_proj(ctx_src, lat_src, lat_off, n, first_tile, mod_l, ao, dof, dob, gb, uc, yf, yb, gc,
              vec, gluw, wo, lnv, n_ctx_tiles, alpha):
    d = lat_src.shape[1]
    tm = ROW_TILE
    bd = _block_diag_const(B_WIDTH, 1.0 / HEAD_DIM)
    row = lambda w: pl.BlockSpec((tm, w), lambda i: (i + first_tile, 0))
    full = lambda a: pl.BlockSpec(a.shape, lambda i: (0,) * a.ndim)
    return pl.pallas_call(
        functools.partial(_out_kernel, n_ctx_tiles=n_ctx_tiles, first_tile=first_tile, d_model=d,
                          alpha=alpha),
        grid=(n // tm - first_tile,),
        in_specs=_row_source_specs(n_ctx_tiles, lat_off, d, first_tile)
        + [full(mod_l), row(A_WIDTH), row(B_WIDTH), row(B_WIDTH), row(B_WIDTH),
           row(C_WIDTH), row(C_WIDTH), row(C_WIDTH), row(C_WIDTH), full(vec), full(bd),
           full(gluw), full(wo), full(lnv)],
        out_specs=pl.BlockSpec((tm, d), lambda i: (i, 0)),
        out_shape=jax.ShapeDtypeStruct((n - first_tile * tm, d), F32),
        compiler_params=_cparams("arbitrary"),
        name="out_proj",
    )(ctx_src, lat_src, mod_l, ao, dof, dob, gb, uc, yf, yb, gc, vec, bd, gluw, wo, lnv)


def _rope_tables(n_lat, n_ctx):
    rows = n_lat // GRID_W
    row = jnp.repeat(jnp.arange(rows, dtype=F32), GRID_W)
    col = jnp.tile(jnp.arange(GRID_W, dtype=F32), rows)
    axis_dim = HEAD_DIM // 2
    inv = ROPE_THETA ** (-jnp.arange(0, axis_dim, 2, dtype=F32) / axis_dim)
    ar, ac = row[:, None] * inv, col[:, None] * inv
    ang = jnp.concatenate([ar, ar, ac, ac], -1)
    cos = jnp.concatenate([jnp.ones((n_ctx, HEAD_DIM), F32), jnp.cos(ang)], 0)
    sin = jnp.concatenate([jnp.zeros((n_ctx, HEAD_DIM), F32), jnp.sin(ang)], 0)
    upper = (jnp.arange(HEAD_DIM) % (HEAD_DIM // 2)) >= HEAD_DIM // 4
    sa = jnp.where(upper, 0.0, -sin)
    sb = jnp.where(upper, sin, 0.0)
    tile = lambda t: jnp.tile(t, (1, LANES // HEAD_DIM))
    return tile(cos), tile(sa), tile(sb)


def kernel(x, c, ctx, c_ctx, w_mod, b_mod, w_in, attn_q_gain, attn_k_gain, dn_conv_w, dn_A_log, dn_dt_bias, dn_out_gain, s5_A_re, s5_A_im, s5_log_dt, s5_B_re, s5_B_im, s5_C_re, s5_C_im, s5_D, glu_w, glu_b, w_out, ln_g, ln_b):
    batch, n_lat, d = x.shape
    n_ctx = ctx.shape[1]
    depth = w_mod.shape[0]
    assert batch == 1 and n_ctx % ROW_TILE == 0 and n_lat % ROW_TILE == 0 and n_lat % GRID_W == 0
    nct = n_ctx // ROW_TILE
    alpha = (2 * depth) ** 0.25

    n = n_ctx + n_lat
    cv = jnp.zeros((SUBLANES, d), F32).at[0].set(c[0]).at[1].set(c_ctx)
    mod = _modulation(cv, w_mod, b_mod)
    cos, sa, sb = _rope_tables(n_lat, n_ctx)
    lane_tile = lambda g: jnp.tile(g.astype(F32)[None, :], (1, LANES // HEAD_DIM))
    s5_par = jax.vmap(_s5_params)(s5_A_re, s5_A_im, s5_log_dt, s5_B_re, s5_B_im, s5_C_re, s5_C_im)

    ctx_src, lat_src, lat_off = ctx[0], x[0], 0
    for l in range(depth):
        (qz, kp, vt, ga, qkvb, gb, bdraw, uc, gc) = _inproj(
            ctx_src, lat_src, lat_off, n, mod[l], _pad_w_in(w_in[l]), (cos, sa, sb),
            lane_tile(attn_q_gain[l]), lane_tile(attn_k_gain[l]), nct)
        ao = _flash(qz, kp, vt, ga, n_ctx)
        qn, kn, vn, bg = _dn_prep(qkvb, bdraw, dn_conv_w[l].astype(F32), dn_A_log[l],
                                  dn_dt_bias[l], n_ctx)
        dof, dob = _dn_scan(qn, kn, vn, bg, n_ctx)
        yf, yb = _s5_scan(uc, tuple(p[l] for p in s5_par), n_ctx)
        vec = jnp.zeros((SUBLANES, C_WIDTH), F32)
        vec = vec.at[0].set(jnp.tile(dn_out_gain[l].astype(F32), B_HEADS))
        vec = vec.at[1].set(s5_D[l].astype(F32)).at[2].set(glu_b[l].astype(F32))
        lnv = jnp.zeros((SUBLANES, d), F32).at[0].set(ln_g[l]).at[1].set(ln_b[l])
        first_tile = nct if l == depth - 1 else 0
        xall = _out_proj(ctx_src, lat_src, lat_off, n, first_tile, mod[l], ao, dof, dob, gb, uc,
                         yf, yb, gc, vec, glu_w[l].astype(BF16), w_out[l].astype(BF16), lnv, nct,
                         alpha)
        ctx_src, lat_src, lat_off = xall, xall, nct
    return xall[None]
```

```python
import functools
import math

import jax
import jax.numpy as jnp
from jax import lax
from jax.experimental import pallas as pl
from jax.experimental.pallas import tpu as pltpu

F32 = jnp.float32
BF16 = jnp.bfloat16
HIGHEST = lax.Precision.HIGHEST

GRID_W = 64
HEAD_DIM = 64
NORM_EPS = 1e-6
ROPE_THETA = 10000.0
A_Q_HEADS = 8
A_KV_HEADS = 2
A_GROUPS = A_Q_HEADS // A_KV_HEADS
A_WIDTH = A_Q_HEADS * HEAD_DIM
A_KV_WIDTH = A_KV_HEADS * HEAD_DIM
B_HEADS = 4
B_WIDTH = B_HEADS * HEAD_DIM
B_CHUNK = 64
C_WIDTH = 256
C_GROUP = 16
C_GROUPS = C_WIDTH // C_GROUP
C_STATE = 64
C_STATES = C_GROUPS * C_STATE
LANES = 128
SUBLANES = 8
ROW_TILE = 256
S5_STEPS = ROW_TILE // SUBLANES
VMEM_LIMIT = 48 * 1024 * 1024
LOG2E = math.log2(math.e)

DN_PREC = 1
S5_PREC = 1
DN_CHUNKS_PER_STEP = 4


def _cparams(*sem):
    return pltpu.CompilerParams(dimension_semantics=sem, vmem_limit_bytes=VMEM_LIMIT)


def _split(a):
    hi = a.astype(BF16)
    lo = (a - hi.astype(F32)).astype(BF16)
    return hi, lo


def _dg(a, b, dims):
    return lax.dot_general(a, b, (dims, ((), ())), preferred_element_type=F32)


def _mm(a, b, prec, dims=((1,), (0,))):
    if prec == 6:
        return lax.dot_general(a, b, (dims, ((), ())), precision=HIGHEST,
                               preferred_element_type=F32)
    if prec == 1:
        return _dg(a.astype(BF16), b.astype(BF16), dims)
    ah, al = _split(a)
    bh, bl = _split(b)
    return _dg(ah, bh, dims) + (_dg(ah, bl, dims) + _dg(al, bh, dims))


_NT = ((1,), (1,))
_TN = ((0,), (0,))


def _mm_pieces(a, b, pieces, split="a", dims=((1,), (0,))):
    exact = (b if split == "a" else a).astype(BF16)
    out, rest = None, (a if split == "a" else b)
    for _ in range(pieces):
        part = rest.astype(BF16)
        rest = rest - part.astype(F32)
        term = _dg(part, exact, dims) if split == "a" else _dg(exact, part, dims)
        out = term if out is None else out + term
    return out


def _silu(x):
    return x * jax.nn.sigmoid(x)


def _mod_kernel(cv_ref, w_ref, b_ref, o_ref):
    sv = _silu(cv_ref[...])
    o_ref[0] = _mm(sv, w_ref[0], 6) + b_ref[0]


def _modulation(cv, w_mod, b_mod):
    depth, d, d3 = w_mod.shape
    nblk = d3 // d
    return pl.pallas_call(
        _mod_kernel,
        grid=(depth, nblk),
        in_specs=[pl.BlockSpec((SUBLANES, d), lambda l, n: (0, 0)),
                  pl.BlockSpec((1, d, d), lambda l, n: (l, 0, n)),
                  pl.BlockSpec((1, 1, d), lambda l, n: (l, 0, n))],
        out_specs=pl.BlockSpec((1, SUBLANES, d), lambda l, n: (l, 0, n)),
        out_shape=jax.ShapeDtypeStruct((depth, SUBLANES, d3), F32),
        compiler_params=_cparams("arbitrary", "arbitrary"),
        name="modulation",
    )(cv, w_mod, b_mod.reshape(depth, 1, d3))


def _layer_norm(x):
    mu = jnp.mean(x, axis=-1, keepdims=True)
    xc = x - mu
    var = jnp.mean(xc * xc, axis=-1, keepdims=True)
    return xc * lax.rsqrt(var + NORM_EPS)


def _mod_row(mod_ref, is_ctx):
    mod = mod_ref[...]
    return jnp.where(is_ctx, mod[1:2], mod[0:1])


_ATTN_QKV_WIDTH = A_WIDTH + 2 * A_KV_WIDTH
_IN_SPLITS = (("ga", A_WIDTH), ("qkvb", 3 * B_WIDTH), ("gb", B_WIDTH), ("bd", LANES),
              ("uc", C_WIDTH), ("gc", C_WIDTH))


def _row_source_specs(n_ctx_tiles, lat_off, d, first_tile=0):
    tm = ROW_TILE
    return [pl.BlockSpec((tm, d), lambda i: (jnp.minimum(i + first_tile, n_ctx_tiles - 1), 0)),
            pl.BlockSpec((tm, d),
                         lambda i: (jnp.maximum(i + first_tile - n_ctx_tiles, 0) + lat_off, 0))]


def _attn_prep_tile(q, k, v, cos, sa, sb, gq, gk, bd, qo_ref, ko_ref, vo_ref):
    def norm_rope(y, gain):
        ms = _mm_pieces(y * y, bd, 2)
        yn = y * lax.rsqrt(ms + NORM_EPS) * gain
        return (yn * cos + pltpu.roll(yn, LANES - HEAD_DIM // 4, 1) * sa
                + pltpu.roll(yn, HEAD_DIM // 4, 1) * sb)

    lane = lax.broadcasted_iota(jnp.int32, (1, LANES), 1)
    lower = lane < HEAD_DIM
    for j in range(A_Q_HEADS // 2):
        y = norm_rope(q[:, j * LANES:(j + 1) * LANES], gq) * (HEAD_DIM ** -0.5 * LOG2E)
        y_sw = pltpu.roll(y, HEAD_DIM, 1)
        for half in range(2):
            h = 2 * j + half
            src = y if (h // A_GROUPS) == half else y_sw
            keep = lower if (h // A_GROUPS) == 0 else jnp.logical_not(lower)
            qo_ref[h] = jnp.where(keep, src, 0.0).astype(BF16)
    ko_ref[...] = norm_rope(k, gk).astype(BF16)
    vt = v.T
    ones = jnp.ones((HEAD_DIM, vt.shape[1]), F32)
    for h in range(A_KV_HEADS):
        vo_ref[h] = jnp.concatenate([vt[h * HEAD_DIM:(h + 1) * HEAD_DIM], ones],
                                    axis=0).astype(BF16)


def _inproj_kernel(c_ref, l_ref, mod_ref, w_ref, cos_ref, sa_ref, sb_ref, gq_ref, gk_ref, bd_ref,
                   qo_ref, ko_ref, vo_ref, *out_refs, n_ctx_tiles, d_model):
    is_ctx = pl.program_id(0) < n_ctx_tiles
    row = _mod_row(mod_ref, is_ctx)
    shift, scale = row[:, :d_model], row[:, d_model:2 * d_model]
    x = jnp.where(is_ctx, c_ref[...], l_ref[...])
    h = _layer_norm(x) * (1.0 + scale) + shift
    z = jnp.dot(h.astype(BF16), w_ref[...], preferred_element_type=F32)
    off = _ATTN_QKV_WIDTH
    for ref, (_, w) in zip(out_refs, _IN_SPLITS):
        ref[...] = z[:, off:off + w]
        off += w
    _attn_prep_tile(z[:, :A_WIDTH], z[:, A_WIDTH:A_WIDTH + A_KV_WIDTH],
                    z[:, A_WIDTH + A_KV_WIDTH:_ATTN_QKV_WIDTH], cos_ref[...], sa_ref[...],
                    sb_ref[...], gq_ref[...], gk_ref[...], bd_ref[...], qo_ref, ko_ref, vo_ref)


def _pad_w_in(w):
    d = w.shape[0]
    n_main = 2 * A_WIDTH + 2 * A_KV_WIDTH + 4 * B_WIDTH
    n_small = 4 * B_HEADS
    return jnp.concatenate(
        [w[:, :n_main], w[:, n_main:n_main + n_small], jnp.zeros((d, LANES - n_small), w.dtype),
         w[:, n_main + n_small:]], axis=1).astype(BF16)


def _block_diag_const(width, value):
    r = jnp.arange(width) // HEAD_DIM
    return jnp.where(r[:, None] == r[None, :], value, 0.0).astype(F32)


def _inproj(ctx_src, lat_src, lat_off, n, mod_l, w_pad, rope, gq, gk, n_ctx_tiles):
    d = lat_src.shape[1]
    tm = ROW_TILE
    bd = _block_diag_const(LANES, 1.0 / HEAD_DIM)
    row = lambda w: pl.BlockSpec((tm, w), lambda i: (i, 0))
    full = lambda a: pl.BlockSpec(a.shape, lambda i: (0,) * a.ndim)
    return pl.pallas_call(
        functools.partial(_inproj_kernel, n_ctx_tiles=n_ctx_tiles, d_model=d),
        grid=(n // tm,),
        in_specs=_row_source_specs(n_ctx_tiles, lat_off, d)
        + [full(mod_l), full(w_pad), row(LANES), row(LANES), row(LANES), full(gq), full(gk),
           full(bd)],
        out_specs=[pl.BlockSpec((A_Q_HEADS, tm, LANES), lambda i: (0, i, 0)),
                   row(LANES),
                   pl.BlockSpec((A_KV_HEADS, LANES, tm), lambda i: (0, 0, i))]
        + [row(w) for _, w in _IN_SPLITS],
        out_shape=[jax.ShapeDtypeStruct((A_Q_HEADS, n, LANES), BF16),
                   jax.ShapeDtypeStruct((n, LANES), BF16),
                   jax.ShapeDtypeStruct((A_KV_HEADS, LANES, n), BF16)]
        + [jax.ShapeDtypeStruct((n, w), F32) for _, w in _IN_SPLITS],
        compiler_params=_cparams("arbitrary"),
        name="inproj",
    )(ctx_src, lat_src, mod_l, w_pad, *rope, gq, gk, bd)


def _flash_kernel(q_ref, k_ref, vt_ref, g_ref, o_ref, m_sc, mblk_sc, alpha_sc, acc_sc, s_sc, p_sc,
                  *, n_ctx, n_ctx_tiles, tc, n_chunks):
    is_ctx = pl.program_id(0) < n_ctx_tiles
    tq = q_ref.shape[1]
    m_sc[...] = jnp.full(m_sc.shape, -jnp.inf, F32)
    acc_sc[...] = jnp.zeros(acc_sc.shape, F32)

    def q_group(g):
        return q_ref[g * A_GROUPS:(g + 1) * A_GROUPS].reshape(A_GROUPS * tq, LANES)

    def scores(g, c0, size):
        s = _dg(k_ref[pl.ds(c0, size), :], q_group(g), _NT)
        s_sc[g, :size, :] = s
        mblk_sc[g] = jnp.max(s, axis=0, keepdims=True)

    def exponentials(g, size):
        m_old = m_sc[g]
        m_new = jnp.maximum(m_old, mblk_sc[g])
        alpha_sc[g] = jnp.exp2(m_old - m_new)
        p_sc[g, :size, :] = jnp.exp2(s_sc[g, :size, :] - m_new).astype(BF16)
        m_sc[g] = m_new

    def weighted_values(g, c0, size):
        acc_sc[g] = alpha_sc[g] * acc_sc[g] + jnp.dot(
            vt_ref[g, :, pl.ds(c0, size)], p_sc[g, :size, :], preferred_element_type=F32)

    @pl.when(is_ctx)
    def _():
        for g in range(A_KV_HEADS):
            scores(g, 0, n_ctx)
            exponentials(g, n_ctx)
            weighted_values(g, 0, n_ctx)

    @pl.when(jnp.logical_not(is_ctx))
    def _():
        first, second = 0, A_KV_HEADS - 1
        p_sc[second] = jnp.zeros(p_sc.shape[1:], BF16)
        alpha_sc[second] = jnp.ones(alpha_sc.shape[1:], F32)
        scores(first, 0, tc)

        def body(c, carry):
            c0 = pl.multiple_of(c * tc, tc)
            c_prev = pl.multiple_of(jnp.maximum(c - 1, 0) * tc, tc)
            c_next = pl.multiple_of(jnp.minimum(c + 1, n_chunks - 1) * tc, tc)
            scores(second, c0, tc)
            exponentials(first, tc)
            weighted_values(second, c_prev, tc)
            scores(first, c_next, tc)
            exponentials(second, tc)
            weighted_values(first, c0, tc)
            return carry

        lax.fori_loop(0, n_chunks, body, 0)
        weighted_values(second, (n_chunks - 1) * tc, tc)

    gate = _silu(g_ref[...])
    for h in range(A_Q_HEADS):
        g, r = divmod(h, A_GROUPS)
        acc = acc_sc[g, :, r * tq:(r + 1) * tq]
        sl = slice(h * HEAD_DIM, (h + 1) * HEAD_DIM)
        o_ref[:, sl] = (acc[:HEAD_DIM] / acc[HEAD_DIM:HEAD_DIM + 1]).T * gate[:, sl]


def _flash_bounded_kernel(u_ref, q_ref, k_ref, vt_ref, g_ref, o_ref, acc_sc, p_sc, *, n_ctx,
                          n_ctx_tiles, tc, n_chunks):
    is_ctx = pl.program_id(0) < n_ctx_tiles
    tq = q_ref.shape[1]
    acc_sc[...] = jnp.zeros(acc_sc.shape, F32)
    bound = u_ref[0]

    def q_group(g):
        return q_ref[g * A_GROUPS:(g + 1) * A_GROUPS].reshape(A_GROUPS * tq, LANES)

    def probabilities(g, c0, size):
        s = _dg(k_ref[pl.ds(c0, size), :], q_group(g), _NT)
        p_sc[g, :size, :] = jnp.exp2(s - bound).astype(BF16)

    def weighted_values(g, c0, size):
        acc_sc[g] += jnp.dot(vt_ref[g, :, pl.ds(c0, size)], p_sc[g, :size, :],
                             preferred_element_type=F32)

    @pl.when(is_ctx)
    def _():
        for g in range(A_KV_HEADS):
            probabilities(g, 0, n_ctx)
            weighted_values(g, 0, n_ctx)

    @pl.when(jnp.logical_not(is_ctx))
    def _():
        first, second = 0, A_KV_HEADS - 1
        probabilities(first, 0, tc)

        def body(c, carry):
            c0 = pl.multiple_of(c * tc, tc)
            c_next = pl.multiple_of(jnp.minimum(c + 1, n_chunks - 1) * tc, tc)
            probabilities(second, c0, tc)
            weighted_values(first, c0, tc)
            probabilities(first, c_next, tc)
            weighted_values(second, c0, tc)
            return carry

        lax.fori_loop(0, n_chunks, body, 0)

    gate = _silu(g_ref[...])
    for h in range(A_Q_HEADS):
        g, r = divmod(h, A_GROUPS)
        acc = acc_sc[g, :, r * tq:(r + 1) * tq]
        sl = slice(h * HEAD_DIM, (h + 1) * HEAD_DIM)
        o_ref[:, sl] = (acc[:HEAD_DIM] / acc[HEAD_DIM:HEAD_DIM + 1]).T * gate[:, sl]


def _key_chunk(n):
    for tc in (1280, 1024, 640, 512, 256, 128):
        if n % tc == 0:
            return tc
    raise ValueError(f"no key chunk for {n} rows")


_SCORE_BOUND_PER_GAIN = 1.01 * HEAD_DIM * HEAD_DIM ** -0.5 * LOG2E
_MAX_FIXED_REFERENCE = 40.0


def _flash(qz, k, vt, ga, n_ctx, gain_product):
    n = k.shape[0]
    tq = ROW_TILE
    tc = _key_chunk(n)
    gq = A_GROUPS * tq
    assert A_KV_HEADS == 2 and n_ctx <= tc
    full = lambda a: pl.BlockSpec(a.shape, lambda i: (0,) * a.ndim,
                                  pipeline_mode=pl.Buffered(1))
    statics = dict(n_ctx=n_ctx, n_ctx_tiles=n_ctx // tq, tc=tc, n_chunks=n // tc)
    common = dict(
        grid=(n // tq,),
        out_specs=pl.BlockSpec((tq, A_WIDTH), lambda i: (i, 0)),
        out_shape=jax.ShapeDtypeStruct((n, A_WIDTH), F32),
        compiler_params=_cparams("arbitrary"))
    in_specs = [pl.BlockSpec((A_Q_HEADS, tq, LANES), lambda i: (0, i, 0)), full(k), full(vt),
                pl.BlockSpec((tq, A_WIDTH), lambda i: (i, 0))]
    acc_shape = pltpu.VMEM((A_KV_HEADS, LANES, gq), F32)
    p_shape = pltpu.VMEM((A_KV_HEADS, tc, gq), BF16)

    def online(bound):
        del bound
        return pl.pallas_call(
            functools.partial(_flash_kernel, **statics), in_specs=in_specs,
            scratch_shapes=[pltpu.VMEM((A_KV_HEADS, 1, gq), F32),
                            pltpu.VMEM((A_KV_HEADS, 1, gq), F32),
                            pltpu.VMEM((A_KV_HEADS, 1, gq), F32),
                            acc_shape,
                            pltpu.VMEM((A_KV_HEADS, tc, gq), F32),
                            p_shape],
            name="flash_attn_online", **common)(qz, k, vt, ga)

    def bounded(bound):
        return pl.pallas_call(
            functools.partial(_flash_bounded_kernel, **statics),
            in_specs=[pl.BlockSpec(memory_space=pltpu.SMEM)] + in_specs,
            scratch_shapes=[acc_shape, p_shape],
            name="flash_attn_bounded", **common)(bound, qz, k, vt, ga)

    bound = (_SCORE_BOUND_PER_GAIN * gain_product).astype(F32).reshape(1)
    return lax.cond(bound[0] <= _MAX_FIXED_REFERENCE, bounded, online, bound)


def _dn_prep_kernel(z_ref, zp_ref, zn_ref, r_ref, w_ref, par_ref, bd_ref,
                    q_o, k_o, v_o, bg_o, *, seq_starts, seq_ends):
    i = pl.program_id(0)
    z = z_ref[...]
    tm = z.shape[0]
    is_start = functools.reduce(jnp.logical_or, [i == s for s in seq_starts])
    is_end = functools.reduce(jnp.logical_or, [i == s for s in seq_ends])
    prev_row = jnp.where(is_start, 0.0, zp_ref[SUBLANES - 1:SUBLANES, :])
    next_row = jnp.where(is_end, 0.0, zn_ref[0:1, :])
    row = lax.broadcasted_iota(jnp.int32, (tm, 1), 0)
    z_prev = jnp.where(row == 0, prev_row, pltpu.roll(z, 1, 0))
    z_next = jnp.where(row == tm - 1, next_row, pltpu.roll(z, tm - 1, 0))
    w = w_ref[...]
    y = _silu(z_prev * w[0:1] + z * w[1:2] + z_next * w[2:3])
    bd = bd_ref[...]

    def l2norm(t):
        return t * lax.rsqrt(_mm_pieces(t * t, bd, 2) + NORM_EPS)

    q_o[...] = l2norm(y[:, :B_WIDTH]) * (HEAD_DIM ** -0.5)
    k_o[...] = l2norm(y[:, B_WIDTH:2 * B_WIDTH])
    v_o[...] = y[:, 2 * B_WIDTH:]
    r = r_ref[...]
    par = par_ref[...]
    xa = r + par[1:2]
    softplus = jnp.maximum(xa, 0.0) + jnp.log(1.0 + jnp.exp(-jnp.abs(xa)))
    g = -jnp.exp(par[0:1]) * softplus
    lane = lax.broadcasted_iota(jnp.int32, (1, LANES), 1)
    bg_o[...] = jnp.where(lane < 2 * B_HEADS, jax.nn.sigmoid(r), g)


def _dn_prep(qkvb, bdraw, conv_w, a_log, dt_bias, n_ctx):
    n = qkvb.shape[0]
    tm = ROW_TILE
    nt, nct = n // tm, n_ctx // tm
    r8 = tm // SUBLANES
    par = jnp.zeros((2, LANES), F32)
    par = par.at[0, 2 * B_HEADS:4 * B_HEADS].set(a_log.reshape(-1))
    par = par.at[1, 2 * B_HEADS:4 * B_HEADS].set(dt_bias.reshape(-1))
    bd = _block_diag_const(B_WIDTH, 1.0)
    w3 = 3 * B_WIDTH
    return pl.pallas_call(
        functools.partial(_dn_prep_kernel, seq_starts=(0, nct), seq_ends=(nct - 1, nt - 1)),
        grid=(nt,),
        in_specs=[pl.BlockSpec((tm, w3), lambda i: (i, 0)),
                  pl.BlockSpec((SUBLANES, w3), lambda i: (jnp.maximum(i * r8 - 1, 0), 0)),
                  pl.BlockSpec((SUBLANES, w3),
                               lambda i: (jnp.minimum((i + 1) * r8, n // SUBLANES - 1), 0)),
                  pl.BlockSpec((tm, LANES), lambda i: (i, 0)),
                  pl.BlockSpec(conv_w.shape, lambda i: (0, 0)),
                  pl.BlockSpec(par.shape, lambda i: (0, 0)),
                  pl.BlockSpec(bd.shape, lambda i: (0, 0))],
        out_specs=[pl.BlockSpec((tm, B_WIDTH), lambda i: (i, 0))] * 3
        + [pl.BlockSpec((tm, LANES), lambda i: (i, 0))],
        out_shape=[jax.ShapeDtypeStruct((n, B_WIDTH), F32)] * 3
        + [jax.ShapeDtypeStruct((n, LANES), F32)],
        compiler_params=_cparams("arbitrary"),
        name="dn_prep",
    )(qkvb, qkvb, qkvb, bdraw, conv_w, par, bd)


def _dn_scan_kernel(qf, kf, vf, bgf, qb, kb, vb, bgb, of_ref, ob_ref, s_ref, *, prec, cps):
    @pl.when(pl.program_id(0) == 0)
    def _():
        s_ref[...] = jnp.zeros(s_ref.shape, F32)

    c, hc = B_CHUNK, B_HEADS * B_CHUNK
    row = lax.broadcasted_iota(jnp.int32, (hc, hc), 0)
    col = lax.broadcasted_iota(jnp.int32, (hc, hc), 1)
    same_head = (row // c) == (col // c)
    eye = (row == col).astype(F32)
    level_masks = [(row // 2) == (col // 2)]
    bs = 2
    while bs < c:
        level_masks.append(jnp.logical_and((row // (2 * bs)) == (col // (2 * bs)),
                                           (row // bs) != (col // bs)))
        bs *= 2
    rc, cc = lax.broadcasted_iota(jnp.int32, (c, c), 0), lax.broadcasted_iota(jnp.int32, (c, c), 1)

    def heads_bd(x):
        return jnp.where(same_head, jnp.concatenate([x] * B_HEADS, axis=0), 0.0)

    def rows_of(slab, lane0):
        return jnp.concatenate(
            [jnp.broadcast_to(slab[:, lane0 + h:lane0 + h + 1], (c, hc)) for h in range(B_HEADS)],
            axis=0)

    def chunk_setup(d, q_ref, k_ref, v_ref, bg_ref, r0):
        rows = slice(r0, r0 + c)
        incl = (col <= row) if d == 0 else (col >= row)
        strict = (col < row) if d == 0 else (col > row)
        incl_c = (cc <= rc) if d == 0 else (cc >= rc)
        last = c - 1 if d == 0 else 0
        lb = d * B_HEADS
        lg = 2 * B_HEADS + lb
        bg = bg_ref[rows, :]
        gcum = _mm_pieces(incl_c.astype(F32), bg, 3, split="b")
        gcum_t = gcum.T
        g_row = jnp.concatenate([gcum_t[lg + h:lg + h + 1, :] for h in range(B_HEADS)], axis=1)
        g_last = gcum[last:last + 1, :]
        decay = jnp.where(jnp.logical_and(same_head, incl),
                          jnp.exp(rows_of(gcum, lg) - g_row), 0.0)
        beta = rows_of(bg, lb)
        eg = rows_of(jnp.exp(gcum), lg)
        to_last = rows_of(jnp.exp(g_last - gcum), lg)
        q_bd = heads_bd(q_ref[rows, :])
        k_bd = heads_bd(k_ref[rows, :])
        v_bd = heads_bd(v_ref[rows, :])
        k_b = k_bd.astype(BF16)
        kk = _mm(k_b, k_b, prec, _NT)
        qk = _mm(q_bd, k_b, prec, _NT)
        a_mat = jnp.where(jnp.logical_and(same_head, strict), kk * beta * decay, 0.0)
        g_state = jnp.concatenate(
            [jnp.broadcast_to(jnp.exp(g_last[:, lg + h:lg + h + 1]), (c, hc))
             for h in range(B_HEADS)], axis=0)
        return dict(a_mat=a_mat, t=eye - jnp.where(level_masks[0], a_mat, 0.0),
                    v_beta=(v_bd * beta).astype(BF16), k_beta=(k_bd * (beta * eg)).astype(BF16),
                    q_dec=(q_bd * eg).astype(BF16), attn=(qk * decay).astype(BF16),
                    k_dec=(k_bd * to_last).astype(BF16), g_state=g_state)

    dirs = ((qf, kf, vf, bgf, of_ref), (qb, kb, vb, bgb, ob_ref))
    chains = [chunk_setup(d, *dirs[d][:4], j * c) for j in range(cps) for d in range(2)]
    for lm in level_masks[1:]:
        part = [_mm(ch["t"], jnp.where(lm, ch["a_mat"], 0.0), prec) for ch in chains]
        for ch, pt in zip(chains, part):
            ch["t"] = ch["t"] - _mm(pt, ch["t"], prec)
    for ch in chains:
        t = ch["t"].astype(BF16)
        ch["w"] = _mm(t, ch["v_beta"], prec)
        ch["k_cum"] = _mm(t, ch["k_beta"], prec)

    states = [s_ref[d] for d in range(2)]
    for step in range(cps):
        sub = [step, cps - 1 - step]
        tms = [chains[sub[d] * 2 + d] for d in range(2)]
        us = [tms[d]["w"] - _mm(tms[d]["k_cum"], states[d], prec) for d in range(2)]
        for d in range(2):
            tm, u, s = tms[d], us[d], states[d]
            o_bd = _mm(tm["q_dec"], s, prec) + _mm(tm["attn"], u, prec)
            states[d] = s * tm["g_state"] + _mm(tm["k_dec"], u, prec, _TN)
            dirs[d][4][sub[d] * c:(sub[d] + 1) * c, :] = functools.reduce(
                jnp.add, [o_bd[h * c:(h + 1) * c] for h in range(B_HEADS)])
    for d in range(2):
        s_ref[d] = states[d]


def _scan_order_maps(n_chunks, n_ctx_chunks):
    fwd = lambda j: (j, 0)
    bwd = lambda j: (jnp.where(j < n_ctx_chunks, n_ctx_chunks - 1 - j,
                               n_chunks - 1 - (j - n_ctx_chunks)), 0)
    return fwd, bwd


def _dn_scan(qn, kn, vn, bg, n_ctx):
    n = qn.shape[0]
    assert B_CHUNK == HEAD_DIM
    c = B_CHUNK * DN_CHUNKS_PER_STEP
    fwd, bwd = _scan_order_maps(n // c, n_ctx // c)
    specs = lambda m: [pl.BlockSpec((c, B_WIDTH), m)] * 3 + [pl.BlockSpec((c, LANES), m)]
    return pl.pallas_call(
        functools.partial(_dn_scan_kernel, prec=DN_PREC, cps=DN_CHUNKS_PER_STEP),
        grid=(n // c,),
        in_specs=specs(fwd) + specs(bwd),
        out_specs=[pl.BlockSpec((c, B_WIDTH), fwd), pl.BlockSpec((c, B_WIDTH), bwd)],
        out_shape=[jax.ShapeDtypeStruct((n, B_WIDTH), F32)] * 2,
        scratch_shapes=[pltpu.VMEM((2, B_WIDTH, B_WIDTH), F32)],
        compiler_params=_cparams("arbitrary"),
        name="dn_scan",
    )(qn, kn, vn, bg, qn, kn, vn, bg)


def _s5_kernel(uf_ref, ub_ref, bre_ref, bim_ref, cre_ref, cimn_ref, a_ref, ap_ref,
               yf_ref, yb_ref, hre_sc, him_sc, carry_sc, *, prec):
    @pl.when(pl.program_id(0) == 0)
    def _():
        carry_sc[...] = jnp.zeros(carry_sc.shape, F32)

    steps, nsub = S5_STEPS, SUBLANES
    ts = steps * nsub
    prow = lax.broadcasted_iota(jnp.int32, (ts, ts), 0)
    pcol = lax.broadcasted_iota(jnp.int32, (ts, ts), 1)
    scan_pos = (prow % nsub) * steps + prow // nsub
    perms = [jnp.where(pcol == scan_pos, 1.0, 0.0).astype(BF16),
             jnp.where(pcol == ts - 1 - scan_pos, 1.0, 0.0).astype(BF16)]
    ndir = 2

    def permute(pm, x, dims, pieces):
        return _mm_pieces(pm, x, pieces, split="b", dims=dims)

    u_perms = [permute(perms[d], u_ref[...], ((1,), (0,)), 1 if prec == 1 else 2)
               for d, u_ref in enumerate((uf_ref, ub_ref))]
    for d in range(ndir):
        hre_sc[d] = _mm(u_perms[d], bre_ref[d], prec)
    for d in range(ndir):
        him_sc[d] = _mm(u_perms[d], bim_ref[d], prec)

    def rows_at(jj):
        return pl.ds(pl.multiple_of(jj * nsub, nsub), nsub)

    def local_scan(jj, carry):
        rows = rows_at(jj)
        out = []
        for d in range(ndir):
            a_re, a_im = a_ref[d, 0], a_ref[d, 1]
            h_re, h_im = carry[d]
            n_re = a_re * h_re - a_im * h_im + hre_sc[d, rows, :]
            n_im = a_re * h_im + a_im * h_re + him_sc[d, rows, :]
            hre_sc[d, rows, :] = n_re
            him_sc[d, rows, :] = n_im
            out.append((n_re, n_im))
        return tuple(out)

    zero = jnp.zeros((nsub, C_STATES), F32)
    ends = lax.fori_loop(0, steps, local_scan, ((zero, zero),) * ndir)
    cins = []
    for d in range(ndir):
        as_re = ap_ref[d, 0, steps - 1, 0:1, :]
        as_im = ap_ref[d, 1, steps - 1, 0:1, :]
        c_re, c_im = carry_sc[2 * d:2 * d + 1, :], carry_sc[2 * d + 1:2 * d + 2, :]
        cin = []
        for s in range(nsub):
            cin.append((c_re, c_im))
            c_re, c_im = (ends[d][0][s:s + 1] + as_re * c_re - as_im * c_im,
                          ends[d][1][s:s + 1] + as_re * c_im + as_im * c_re)
        carry_sc[2 * d:2 * d + 1, :] = c_re
        carry_sc[2 * d + 1:2 * d + 2, :] = c_im
        cins.append((jnp.concatenate([x[0] for x in cin], axis=0),
                     jnp.concatenate([x[1] for x in cin], axis=0)))

    def fixup(jj, carry):
        rows = rows_at(jj)
        for d in range(ndir):
            p_re = ap_ref[d, 0, jj]
            p_im = ap_ref[d, 1, jj]
            cin_re, cin_im = cins[d]
            hre_sc[d, rows, :] += p_re * cin_re - p_im * cin_im
            him_sc[d, rows, :] += p_re * cin_im + p_im * cin_re
        return carry

    lax.fori_loop(0, steps, fixup, 0)
    y_re = [_mm(hre_sc[d], cre_ref[d], prec) for d in range(ndir)]
    y_im = [_mm(him_sc[d], cimn_ref[d], prec) for d in range(ndir)]
    for d, y_ref in enumerate((yf_ref, yb_ref)):
        y_ref[...] = permute(perms[d], y_re[d] + y_im[d], _TN, 2)


def _s5_params(a_re, a_im, log_dt, b_re, b_im, c_re, c_im):
    dt = jnp.exp(log_dt.astype(F32))[..., None]
    lam_re, lam_im = a_re.astype(F32), a_im.astype(F32)
    k = jnp.arange(1, S5_STEPS + 1, dtype=F32)[None, :, None, None]
    mag = jnp.exp(lam_re[:, None] * dt[:, None] * k)
    ang = lam_im[:, None] * dt[:, None] * k
    ap = jnp.stack([mag * jnp.cos(ang), mag * jnp.sin(ang)], axis=1)
    ap = ap.reshape(2, 2, S5_STEPS, C_STATES)
    ab_re, ab_im = ap[:, 0, 0], ap[:, 1, 0]
    abar = jnp.stack([ab_re, ab_im], axis=1)
    x, y = ab_re.reshape(lam_re.shape) - 1.0, ab_im.reshape(lam_re.shape)
    den = lam_re * lam_re + lam_im * lam_im
    cf_re, cf_im = (x * lam_re + y * lam_im) / den, (y * lam_re - x * lam_im) / den
    bb_re = cf_re[..., None] * b_re - cf_im[..., None] * b_im
    bb_im = cf_re[..., None] * b_im + cf_im[..., None] * b_re
    eye = jnp.eye(C_GROUPS, dtype=F32)
    dense_b = lambda b: jnp.einsum('dgpc,gh->dgchp', b, eye).reshape(2, C_WIDTH, C_STATES)
    dense_c = lambda c: jnp.einsum('dgcp,gh->dgphc', c, eye).reshape(2, C_STATES, C_WIDTH)
    abar = jnp.broadcast_to(abar[:, :, None, :], (2, 2, SUBLANES, C_STATES))
    ap = jnp.broadcast_to(ap[:, :, :, None, :], (2, 2, S5_STEPS, SUBLANES, C_STATES))
    return (dense_b(bb_re), dense_b(bb_im), dense_c(c_re.astype(F32)),
            dense_c(-c_im.astype(F32)), abar, ap)


def _s5_scan(uc, params, n_ctx):
    n = uc.shape[0]
    ts = ROW_TILE
    bre, bim, cre, cimn, abar, ap = params
    fwd, bwd = _scan_order_maps(n // ts, n_ctx // ts)
    full = lambda a: pl.BlockSpec(a.shape, lambda j: (0,) * a.ndim,
                                  pipeline_mode=pl.Buffered(1))
    return pl.pallas_call(
        functools.partial(_s5_kernel, prec=S5_PREC),
        grid=(n // ts,),
        in_specs=[pl.BlockSpec((ts, C_WIDTH), fwd), pl.BlockSpec((ts, C_WIDTH), bwd),
                  full(bre), full(bim), full(cre), full(cimn), full(abar), full(ap)],
        out_specs=[pl.BlockSpec((ts, C_WIDTH), fwd), pl.BlockSpec((ts, C_WIDTH), bwd)],
        out_shape=[jax.ShapeDtypeStruct((n, C_WIDTH), F32)] * 2,
        scratch_shapes=[pltpu.VMEM((2, ts, C_STATES), F32), pltpu.VMEM((2, ts, C_STATES), F32),
                        pltpu.VMEM((SUBLANES, C_STATES), F32)],
        compiler_params=_cparams("arbitrary"),
        name="s5_scan",
    )(uc, uc, bre, bim, cre, cimn, abar, ap)


def _gelu_tanh(x):
    return 0.5 * x * (1.0 + jnp.tanh(math.sqrt(2.0 / math.pi) * (x + 0.044715 * (x * x * x))))


def _out_kernel(c_ref, l_ref, mod_ref, ao_ref, dof_ref, dob_ref, gb_ref, u_ref, yf_ref, yb_ref,
                gc_ref, vec_ref, bd_ref, gluw_ref, wo_ref, lnv_ref, o_ref, *, n_ctx_tiles,
                first_tile, d_model, alpha):
    is_ctx = pl.program_id(0) + first_tile < n_ctx_tiles
    vec = vec_ref[...]
    o = dof_ref[...] + dob_ref[...]
    ms = _mm_pieces(o * o, bd_ref[...], 2)
    dn = o * lax.rsqrt(ms + NORM_EPS) * vec[0:1] * _silu(gb_ref[...])
    z = _gelu_tanh(u_ref[...] * vec[1:2] + yf_ref[...] + yb_ref[...])
    lin = jnp.dot(z.astype(BF16), gluw_ref[...], preferred_element_type=F32) + vec[2:3]
    s5 = z * jax.nn.sigmoid(lin) * _silu(gc_ref[...])
    wo = wo_ref
    proj = (jnp.dot(ao_ref[...].astype(BF16), wo[:A_WIDTH, :], preferred_element_type=F32)
            + jnp.dot(dn.astype(BF16), wo[A_WIDTH:A_WIDTH + B_WIDTH, :],
                      preferred_element_type=F32)
            + jnp.dot(s5.astype(BF16), wo[A_WIDTH + B_WIDTH:, :], preferred_element_type=F32))
    gate = _mod_row(mod_ref, is_ctx)[:, 2 * d_model:]
    lnv = lnv_ref[...]
    x = jnp.where(is_ctx, c_ref[...], l_ref[...])
    o_ref[...] = _layer_norm(alpha * x + gate * proj) * lnv[0:1] + lnv[1:2]


def _out_proj(ctx_src, lat_src, lat_off, n, first_tile, mod_l, ao, dof, dob, gb, uc, yf, yb, gc,
              vec, gluw, wo, lnv, n_ctx_tiles, alpha):
    d = lat_src.shape[1]
    tm = ROW_TILE
    bd = _block_diag_const(B_WIDTH, 1.0 / HEAD_DIM)
    row = lambda w: pl.BlockSpec((tm, w), lambda i: (i + first_tile, 0))
    full = lambda a: pl.BlockSpec(a.shape, lambda i: (0,) * a.ndim)
    return pl.pallas_call(
        functools.partial(_out_kernel, n_ctx_tiles=n_ctx_tiles, first_tile=first_tile, d_model=d,
                          alpha=alpha),
        grid=(n // tm - first_tile,),
        in_specs=_row_source_specs(n_ctx_tiles, lat_off, d, first_tile)
        + [full(mod_l), row(A_WIDTH), row(B_WIDTH), row(B_WIDTH), row(B_WIDTH),
           row(C_WIDTH), row(C_WIDTH), row(C_WIDTH), row(C_WIDTH), full(vec), full(bd),
           full(gluw), full(wo), full(lnv)],
        out_specs=pl.BlockSpec((tm, d), lambda i: (i, 0)),
        out_shape=jax.ShapeDtypeStruct((n - first_tile * tm, d), F32),
        compiler_params=_cparams("arbitrary"),
        name="out_proj",
    )(ctx_src, lat_src, mod_l, ao, dof, dob, gb, uc, yf, yb, gc, vec, bd, gluw, wo, lnv)


def _rope_tables(n_lat, n_ctx):
    rows = n_lat // GRID_W
    row = jnp.repeat(jnp.arange(rows, dtype=F32), GRID_W)
    col = jnp.tile(jnp.arange(GRID_W, dtype=F32), rows)
    axis_dim = HEAD_DIM // 2
    inv = ROPE_THETA ** (-jnp.arange(0, axis_dim, 2, dtype=F32) / axis_dim)
    ar, ac = row[:, None] * inv, col[:, None] * inv
    ang = jnp.concatenate([ar, ar, ac, ac], -1)
    cos = jnp.concatenate([jnp.ones((n_ctx, HEAD_DIM), F32), jnp.cos(ang)], 0)
    sin = jnp.concatenate([jnp.zeros((n_ctx, HEAD_DIM), F32), jnp.sin(ang)], 0)
    upper = (jnp.arange(HEAD_DIM) % (HEAD_DIM // 2)) >= HEAD_DIM // 4
    sa = jnp.where(upper, 0.0, -sin)
    sb = jnp.where(upper, sin, 0.0)
    tile = lambda t: jnp.tile(t, (1, LANES // HEAD_DIM))
    return tile(cos), tile(sa), tile(sb)


def kernel(x, c, ctx, c_ctx, w_mod, b_mod, w_in, attn_q_gain, attn_k_gain, dn_conv_w, dn_A_log, dn_dt_bias, dn_out_gain, s5_A_re, s5_A_im, s5_log_dt, s5_B_re, s5_B_im, s5_C_re, s5_C_im, s5_D, glu_w, glu_b, w_out, ln_g, ln_b):
    batch, n_lat, d = x.shape
    n_ctx = ctx.shape[1]
    depth = w_mod.shape[0]
    assert batch == 1 and n_ctx % ROW_TILE == 0 and n_lat % ROW_TILE == 0 and n_lat % GRID_W == 0
    nct = n_ctx // ROW_TILE
    alpha = (2 * depth) ** 0.25

    n = n_ctx + n_lat
    cv = jnp.zeros((SUBLANES, d), F32).at[0].set(c[0]).at[1].set(c_ctx)
    mod = _modulation(cv, w_mod, b_mod)
    cos, sa, sb = _rope_tables(n_lat, n_ctx)
    lane_tile = lambda g: jnp.tile(g.astype(F32)[None, :], (1, LANES // HEAD_DIM))
    s5_par = jax.vmap(_s5_params)(s5_A_re, s5_A_im, s5_log_dt, s5_B_re, s5_B_im, s5_C_re, s5_C_im)

    ctx_src, lat_src, lat_off = ctx[0], x[0], 0
    for l in range(depth):
        (qz, kp, vt, ga, qkvb, gb, bdraw, uc, gc) = _inproj(
            ctx_src, lat_src, lat_off, n, mod[l], _pad_w_in(w_in[l]), (cos, sa, sb),
            lane_tile(attn_q_gain[l]), lane_tile(attn_k_gain[l]), nct)
        gain_product = jnp.max(jnp.abs(attn_q_gain[l])) * jnp.max(jnp.abs(attn_k_gain[l]))
        ao = _flash(qz, kp, vt, ga, n_ctx, gain_product)
        qn, kn, vn, bg = _dn_prep(qkvb, bdraw, dn_conv_w[l].astype(F32), dn_A_log[l],
                                  dn_dt_bias[l], n_ctx)
        dof, dob = _dn_scan(qn, kn, vn, bg, n_ctx)
        yf, yb = _s5_scan(uc, tuple(p[l] for p in s5_par), n_ctx)
        vec = jnp.zeros((SUBLANES, C_WIDTH), F32)
        vec = vec.at[0].set(jnp.tile(dn_out_gain[l].astype(F32), B_HEADS))
        vec = vec.at[1].set(s5_D[l].astype(F32)).at[2].set(glu_b[l].astype(F32))
        lnv = jnp.zeros((SUBLANES, d), F32).at[0].set(ln_g[l]).at[1].set(ln_b[l])
        first_tile = nct if l == depth - 1 else 0
        xall = _out_proj(ctx_src, lat_src, lat_off, n, first_tile, mod[l], ao, dof, dob, gb, uc,
                         yf, yb, gc, vec, glu_w[l].astype(BF16), w_out[l].astype(BF16), lnv, nct,
                         alpha)
        ctx_src, lat_src, lat_off = xall, xall, nct
    return xall[None]
```

```python
import functools
import math

import jax
import jax.numpy as jnp
from jax import lax
from jax.experimental import pallas as pl
from jax.experimental.pallas import tpu as pltpu

F32 = jnp.float32
BF16 = jnp.bfloat16
HIGHEST = lax.Precision.HIGHEST

GRID_W = 64
HEAD_DIM = 64
NORM_EPS = 1e-6
ROPE_THETA = 10000.0
A_Q_HEADS = 8
A_KV_HEADS = 2
A_GROUPS = A_Q_HEADS // A_KV_HEADS
A_WIDTH = A_Q_HEADS * HEAD_DIM
A_KV_WIDTH = A_KV_HEADS * HEAD_DIM
B_HEADS = 4
B_WIDTH = B_HEADS * HEAD_DIM
B_CHUNK = 64
C_WIDTH = 256
C_GROUP = 16
C_GROUPS = C_WIDTH // C_GROUP
C_STATE = 64
C_STATES = C_GROUPS * C_STATE
LANES = 128
SUBLANES = 8
ROW_TILE = 256
S5_STEPS = ROW_TILE // SUBLANES
VMEM_LIMIT = 48 * 1024 * 1024
LOG2E = math.log2(math.e)

DN_PREC = 1
S5_PREC = 1
DN_CHUNKS_PER_STEP = 4


def _cparams(*sem):
    return pltpu.CompilerParams(dimension_semantics=sem, vmem_limit_bytes=VMEM_LIMIT)


def _split(a):
    hi = a.astype(BF16)
    lo = (a - hi.astype(F32)).astype(BF16)
    return hi, lo


def _dg(a, b, dims):
    return lax.dot_general(a, b, (dims, ((), ())), preferred_element_type=F32)


def _mm(a, b, prec, dims=((1,), (0,))):
    if prec == 6:
        return lax.dot_general(a, b, (dims, ((), ())), precision=HIGHEST,
                               preferred_element_type=F32)
    if prec == 1:
        return _dg(a.astype(BF16), b.astype(BF16), dims)
    ah, al = _split(a)
    bh, bl = _split(b)
    return _dg(ah, bh, dims) + (_dg(ah, bl, dims) + _dg(al, bh, dims))


_NT = ((1,), (1,))
_TN = ((0,), (0,))


def _mm_pieces(a, b, pieces, split="a", dims=((1,), (0,))):
    exact = (b if split == "a" else a).astype(BF16)
    out, rest = None, (a if split == "a" else b)
    for _ in range(pieces):
        part = rest.astype(BF16)
        rest = rest - part.astype(F32)
        term = _dg(part, exact, dims) if split == "a" else _dg(exact, part, dims)
        out = term if out is None else out + term
    return out


def _silu(x):
    return x * jax.nn.sigmoid(x)


def _mod_kernel(cv_ref, w_ref, b_ref, o_ref):
    sv = _silu(cv_ref[...])
    o_ref[0] = _mm(sv, w_ref[0], 3) + b_ref[0]


def _modulation(cv, w_mod, b_mod):
    depth, d, d3 = w_mod.shape
    nblk = d3 // d
    return pl.pallas_call(
        _mod_kernel,
        grid=(depth, nblk),
        in_specs=[pl.BlockSpec((SUBLANES, d), lambda l, n: (0, 0)),
                  pl.BlockSpec((1, d, d), lambda l, n: (l, 0, n)),
                  pl.BlockSpec((1, 1, d), lambda l, n: (l, 0, n))],
        out_specs=pl.BlockSpec((1, SUBLANES, d), lambda l, n: (l, 0, n)),
        out_shape=jax.ShapeDtypeStruct((depth, SUBLANES, d3), F32),
        compiler_params=_cparams("arbitrary", "arbitrary"),
        name="modulation",
    )(cv, w_mod, b_mod.reshape(depth, 1, d3))


def _layer_norm(x):
    mu = jnp.mean(x, axis=-1, keepdims=True)
    xc = x - mu
    var = jnp.mean(xc * xc, axis=-1, keepdims=True)
    return xc * lax.rsqrt(var + NORM_EPS)


def _mod_row(mod_ref, is_ctx):
    mod = mod_ref[...]
    return jnp.where(is_ctx, mod[1:2], mod[0:1])


_ATTN_QKV_WIDTH = A_WIDTH + 2 * A_KV_WIDTH
_IN_SPLITS = (("ga", A_WIDTH), ("qkvb", 3 * B_WIDTH), ("gb", B_WIDTH), ("bd", LANES),
              ("uc", C_WIDTH), ("gc", C_WIDTH))


def _row_source_specs(n_ctx_tiles, lat_off, d, first_tile=0):
    tm = ROW_TILE
    return [pl.BlockSpec((tm, d), lambda i: (jnp.minimum(i + first_tile, n_ctx_tiles - 1), 0)),
            pl.BlockSpec((tm, d),
                         lambda i: (jnp.maximum(i + first_tile - n_ctx_tiles, 0) + lat_off, 0))]


def _attn_prep_tile(q, k, v, cos, sa, sb, gq, gk, bd, qo_ref, ko_ref, vo_ref):
    def norm_rope(y, gain):
        ms = _mm_pieces(y * y, bd, 2)
        yn = y * lax.rsqrt(ms + NORM_EPS) * gain
        return (yn * cos + pltpu.roll(yn, LANES - HEAD_DIM // 4, 1) * sa
                + pltpu.roll(yn, HEAD_DIM // 4, 1) * sb)

    lane = lax.broadcasted_iota(jnp.int32, (1, LANES), 1)
    lower = lane < HEAD_DIM
    for j in range(A_Q_HEADS // 2):
        y = norm_rope(q[:, j * LANES:(j + 1) * LANES], gq) * (HEAD_DIM ** -0.5 * LOG2E)
        y_sw = pltpu.roll(y, HEAD_DIM, 1)
        for half in range(2):
            h = 2 * j + half
            src = y if (h // A_GROUPS) == half else y_sw
            keep = lower if (h // A_GROUPS) == 0 else jnp.logical_not(lower)
            qo_ref[h] = jnp.where(keep, src, 0.0).astype(BF16)
    ko_ref[...] = norm_rope(k, gk).astype(BF16)
    vt = v.T
    ones = jnp.ones((HEAD_DIM, vt.shape[1]), F32)
    for h in range(A_KV_HEADS):
        vo_ref[h] = jnp.concatenate([vt[h * HEAD_DIM:(h + 1) * HEAD_DIM], ones],
                                    axis=0).astype(BF16)


def _inproj_kernel(c_ref, l_ref, mod_ref, w_ref, cos_ref, sa_ref, sb_ref, gq_ref, gk_ref, bd_ref,
                   qo_ref, ko_ref, vo_ref, *out_refs, n_ctx_tiles, d_model):
    is_ctx = pl.program_id(0) < n_ctx_tiles
    row = _mod_row(mod_ref, is_ctx)
    shift, scale = row[:, :d_model], row[:, d_model:2 * d_model]
    x = jnp.where(is_ctx, c_ref[...], l_ref[...])
    h = _layer_norm(x) * (1.0 + scale) + shift
    z = jnp.dot(h.astype(BF16), w_ref[...], preferred_element_type=F32)
    off = _ATTN_QKV_WIDTH
    for ref, (_, w) in zip(out_refs, _IN_SPLITS):
        ref[...] = z[:, off:off + w]
        off += w
    _attn_prep_tile(z[:, :A_WIDTH], z[:, A_WIDTH:A_WIDTH + A_KV_WIDTH],
                    z[:, A_WIDTH + A_KV_WIDTH:_ATTN_QKV_WIDTH], cos_ref[...], sa_ref[...],
                    sb_ref[...], gq_ref[...], gk_ref[...], bd_ref[...], qo_ref, ko_ref, vo_ref)


def _pad_w_in(w):
    d = w.shape[0]
    n_main = 2 * A_WIDTH + 2 * A_KV_WIDTH + 4 * B_WIDTH
    n_small = 4 * B_HEADS
    return jnp.concatenate(
        [w[:, :n_main], w[:, n_main:n_main + n_small], jnp.zeros((d, LANES - n_small), w.dtype),
         w[:, n_main + n_small:]], axis=1).astype(BF16)


def _block_diag_const(width, value):
    r = jnp.arange(width) // HEAD_DIM
    return jnp.where(r[:, None] == r[None, :], value, 0.0).astype(F32)


def _inproj(ctx_src, lat_src, lat_off, n, mod_l, w_pad, rope, gq, gk, n_ctx_tiles):
    d = lat_src.shape[1]
    tm = ROW_TILE
    bd = _block_diag_const(LANES, 1.0 / HEAD_DIM)
    row = lambda w: pl.BlockSpec((tm, w), lambda i: (i, 0))
    full = lambda a: pl.BlockSpec(a.shape, lambda i: (0,) * a.ndim)
    return pl.pallas_call(
        functools.partial(_inproj_kernel, n_ctx_tiles=n_ctx_tiles, d_model=d),
        grid=(n // tm,),
        in_specs=_row_source_specs(n_ctx_tiles, lat_off, d)
        + [full(mod_l), full(w_pad), row(LANES), row(LANES), row(LANES), full(gq), full(gk),
           full(bd)],
        out_specs=[pl.BlockSpec((A_Q_HEADS, tm, LANES), lambda i: (0, i, 0)),
                   row(LANES),
                   pl.BlockSpec((A_KV_HEADS, LANES, tm), lambda i: (0, 0, i))]
        + [row(w) for _, w in _IN_SPLITS],
        out_shape=[jax.ShapeDtypeStruct((A_Q_HEADS, n, LANES), BF16),
                   jax.ShapeDtypeStruct((n, LANES), BF16),
                   jax.ShapeDtypeStruct((A_KV_HEADS, LANES, n), BF16)]
        + [jax.ShapeDtypeStruct((n, w), F32) for _, w in _IN_SPLITS],
        compiler_params=_cparams("arbitrary"),
        name="inproj",
    )(ctx_src, lat_src, mod_l, w_pad, *rope, gq, gk, bd)


def _flash_kernel(q_ref, k_ref, vt_ref, g_ref, o_ref, m_sc, mblk_sc, alpha_sc, acc_sc, s_sc, p_sc,
                  *, n_ctx, n_ctx_tiles, tc, n_chunks):
    is_ctx = pl.program_id(0) < n_ctx_tiles
    tq = q_ref.shape[1]
    m_sc[...] = jnp.full(m_sc.shape, -jnp.inf, F32)
    acc_sc[...] = jnp.zeros(acc_sc.shape, F32)

    def q_group(g):
        return q_ref[g * A_GROUPS:(g + 1) * A_GROUPS].reshape(A_GROUPS * tq, LANES)

    def scores(g, c0, size):
        s = _dg(k_ref[pl.ds(c0, size), :], q_group(g), _NT)
        s_sc[g, :size, :] = s
        mblk_sc[g] = jnp.max(s, axis=0, keepdims=True)

    def exponentials(g, size):
        m_old = m_sc[g]
        m_new = jnp.maximum(m_old, mblk_sc[g])
        alpha_sc[g] = jnp.exp2(m_old - m_new)
        p_sc[g, :size, :] = jnp.exp2(s_sc[g, :size, :] - m_new).astype(BF16)
        m_sc[g] = m_new

    def weighted_values(g, c0, size):
        acc_sc[g] = alpha_sc[g] * acc_sc[g] + jnp.dot(
            vt_ref[g, :, pl.ds(c0, size)], p_sc[g, :size, :], preferred_element_type=F32)

    @pl.when(is_ctx)
    def _():
        for g in range(A_KV_HEADS):
            scores(g, 0, n_ctx)
            exponentials(g, n_ctx)
            weighted_values(g, 0, n_ctx)

    @pl.when(jnp.logical_not(is_ctx))
    def _():
        first, second = 0, A_KV_HEADS - 1
        p_sc[second] = jnp.zeros(p_sc.shape[1:], BF16)
        alpha_sc[second] = jnp.ones(alpha_sc.shape[1:], F32)
        scores(first, 0, tc)

        def body(c, carry):
            c0 = pl.multiple_of(c * tc, tc)
            c_prev = pl.multiple_of(jnp.maximum(c - 1, 0) * tc, tc)
            c_next = pl.multiple_of(jnp.minimum(c + 1, n_chunks - 1) * tc, tc)
            scores(second, c0, tc)
            exponentials(first, tc)
            weighted_values(second, c_prev, tc)
            scores(first, c_next, tc)
            exponentials(second, tc)
            weighted_values(first, c0, tc)
            return carry

        lax.fori_loop(0, n_chunks, body, 0)
        weighted_values(second, (n_chunks - 1) * tc, tc)

    gate = _silu(g_ref[...])
    for h in range(A_Q_HEADS):
        g, r = divmod(h, A_GROUPS)
        acc = acc_sc[g, :, r * tq:(r + 1) * tq]
        sl = slice(h * HEAD_DIM, (h + 1) * HEAD_DIM)
        o_ref[:, sl] = (acc[:HEAD_DIM] / acc[HEAD_DIM:HEAD_DIM + 1]).T * gate[:, sl]


def _flash_bounded_kernel(u_ref, q_ref, k_ref, vt_ref, g_ref, o_ref, acc_sc, p_sc, *, n_ctx,
                          n_ctx_tiles, tc, n_chunks):
    is_ctx = pl.program_id(0) < n_ctx_tiles
    tq = q_ref.shape[1]
    acc_sc[...] = jnp.zeros(acc_sc.shape, F32)
    bound = u_ref[0]

    def q_group(g):
        return q_ref[g * A_GROUPS:(g + 1) * A_GROUPS].reshape(A_GROUPS * tq, LANES)

    def probabilities(g, c0, size):
        s = _dg(k_ref[pl.ds(c0, size), :], q_group(g), _NT)
        p_sc[g, :size, :] = jnp.exp2(s - bound).astype(BF16)

    def weighted_values(g, c0, size):
        acc_sc[g] += jnp.dot(vt_ref[g, :, pl.ds(c0, size)], p_sc[g, :size, :],
                             preferred_element_type=F32)

    @pl.when(is_ctx)
    def _():
        for g in range(A_KV_HEADS):
            probabilities(g, 0, n_ctx)
            weighted_values(g, 0, n_ctx)

    @pl.when(jnp.logical_not(is_ctx))
    def _():
        first, second = 0, A_KV_HEADS - 1
        probabilities(first, 0, tc)

        def body(c, carry):
            c0 = pl.multiple_of(c * tc, tc)
            probabilities(second, c0, tc)
            weighted_values(first, c0, tc)
            probabilities(first, pl.multiple_of(c0 + tc, tc), tc)
            weighted_values(second, c0, tc)
            return carry

        lax.fori_loop(0, n_chunks - 1, body, 0)
        last = (n_chunks - 1) * tc
        probabilities(second, last, tc)
        weighted_values(first, last, tc)
        weighted_values(second, last, tc)

    gate = _silu(g_ref[...])
    for h in range(A_Q_HEADS):
        g, r = divmod(h, A_GROUPS)
        acc = acc_sc[g, :, r * tq:(r + 1) * tq]
        sl = slice(h * HEAD_DIM, (h + 1) * HEAD_DIM)
        o_ref[:, sl] = (acc[:HEAD_DIM] / acc[HEAD_DIM:HEAD_DIM + 1]).T * gate[:, sl]


def _key_chunk(n):
    for tc in (1280, 1024, 640, 512, 256, 128):
        if n % tc == 0:
            return tc
    raise ValueError(f"no key chunk for {n} rows")


_SCORE_BOUND_PER_GAIN = 1.01 * HEAD_DIM * HEAD_DIM ** -0.5 * LOG2E
_MAX_FIXED_REFERENCE = 40.0


def _flash(qz, k, vt, ga, n_ctx, gain_product):
    n = k.shape[0]
    tq = ROW_TILE
    tc = _key_chunk(n)
    gq = A_GROUPS * tq
    assert A_KV_HEADS == 2 and n_ctx <= tc
    full = lambda a: pl.BlockSpec(a.shape, lambda i: (0,) * a.ndim,
                                  pipeline_mode=pl.Buffered(1))
    statics = dict(n_ctx=n_ctx, n_ctx_tiles=n_ctx // tq, tc=tc, n_chunks=n // tc)
    common = dict(
        grid=(n // tq,),
        out_specs=pl.BlockSpec((tq, A_WIDTH), lambda i: (i, 0)),
        out_shape=jax.ShapeDtypeStruct((n, A_WIDTH), F32),
        compiler_params=_cparams("arbitrary"))
    in_specs = [pl.BlockSpec((A_Q_HEADS, tq, LANES), lambda i: (0, i, 0)), full(k), full(vt),
                pl.BlockSpec((tq, A_WIDTH), lambda i: (i, 0))]
    acc_shape = pltpu.VMEM((A_KV_HEADS, LANES, gq), F32)
    p_shape = pltpu.VMEM((A_KV_HEADS, tc, gq), BF16)

    def online(bound):
        del bound
        return pl.pallas_call(
            functools.partial(_flash_kernel, **statics), in_specs=in_specs,
            scratch_shapes=[pltpu.VMEM((A_KV_HEADS, 1, gq), F32),
                            pltpu.VMEM((A_KV_HEADS, 1, gq), F32),
                            pltpu.VMEM((A_KV_HEADS, 1, gq), F32),
                            acc_shape,
                            pltpu.VMEM((A_KV_HEADS, tc, gq), F32),
                            p_shape],
            name="flash_attn_online", **common)(qz, k, vt, ga)

    def bounded(bound):
        return pl.pallas_call(
            functools.partial(_flash_bounded_kernel, **statics),
            in_specs=[pl.BlockSpec(memory_space=pltpu.SMEM)] + in_specs,
            scratch_shapes=[acc_shape, p_shape],
            name="flash_attn_bounded", **common)(bound, qz, k, vt, ga)

    bound = (_SCORE_BOUND_PER_GAIN * gain_product).astype(F32).reshape(1)
    return lax.cond(bound[0] <= _MAX_FIXED_REFERENCE, bounded, online, bound)


def _dn_prep_kernel(z_ref, zp_ref, zn_ref, r_ref, w_ref, par_ref, bd_ref,
                    q_o, k_o, v_o, bg_o, *, seq_starts, seq_ends):
    i = pl.program_id(0)
    z = z_ref[...]
    tm = z.shape[0]
    is_start = functools.reduce(jnp.logical_or, [i == s for s in seq_starts])
    is_end = functools.reduce(jnp.logical_or, [i == s for s in seq_ends])
    prev_row = jnp.where(is_start, 0.0, zp_ref[SUBLANES - 1:SUBLANES, :])
    next_row = jnp.where(is_end, 0.0, zn_ref[0:1, :])
    row = lax.broadcasted_iota(jnp.int32, (tm, 1), 0)
    z_prev = jnp.where(row == 0, prev_row, pltpu.roll(z, 1, 0))
    z_next = jnp.where(row == tm - 1, next_row, pltpu.roll(z, tm - 1, 0))
    w = w_ref[...]
    y = _silu(z_prev * w[0:1] + z * w[1:2] + z_next * w[2:3])
    bd = bd_ref[...]

    def l2norm(t):
        return t * lax.rsqrt(_mm_pieces(t * t, bd, 2) + NORM_EPS)

    q_o[...] = l2norm(y[:, :B_WIDTH]) * (HEAD_DIM ** -0.5)
    k_o[...] = l2norm(y[:, B_WIDTH:2 * B_WIDTH])
    v_o[...] = y[:, 2 * B_WIDTH:]
    r = r_ref[...]
    par = par_ref[...]
    xa = r + par[1:2]
    softplus = jnp.maximum(xa, 0.0) + jnp.log(1.0 + jnp.exp(-jnp.abs(xa)))
    g = -jnp.exp(par[0:1]) * softplus
    lane = lax.broadcasted_iota(jnp.int32, (1, LANES), 1)
    bg_o[...] = jnp.where(lane < 2 * B_HEADS, jax.nn.sigmoid(r), g)


def _dn_prep(qkvb, bdraw, conv_w, a_log, dt_bias, n_ctx):
    n = qkvb.shape[0]
    tm = ROW_TILE
    nt, nct = n // tm, n_ctx // tm
    r8 = tm // SUBLANES
    par = jnp.zeros((2, LANES), F32)
    par = par.at[0, 2 * B_HEADS:4 * B_HEADS].set(a_log.reshape(-1))
    par = par.at[1, 2 * B_HEADS:4 * B_HEADS].set(dt_bias.reshape(-1))
    bd = _block_diag_const(B_WIDTH, 1.0)
    w3 = 3 * B_WIDTH
    return pl.pallas_call(
        functools.partial(_dn_prep_kernel, seq_starts=(0, nct), seq_ends=(nct - 1, nt - 1)),
        grid=(nt,),
        in_specs=[pl.BlockSpec((tm, w3), lambda i: (i, 0)),
                  pl.BlockSpec((SUBLANES, w3), lambda i: (jnp.maximum(i * r8 - 1, 0), 0)),
                  pl.BlockSpec((SUBLANES, w3),
                               lambda i: (jnp.minimum((i + 1) * r8, n // SUBLANES - 1), 0)),
                  pl.BlockSpec((tm, LANES), lambda i: (i, 0)),
                  pl.BlockSpec(conv_w.shape, lambda i: (0, 0)),
                  pl.BlockSpec(par.shape, lambda i: (0, 0)),
                  pl.BlockSpec(bd.shape, lambda i: (0, 0))],
        out_specs=[pl.BlockSpec((tm, B_WIDTH), lambda i: (i, 0))] * 3
        + [pl.BlockSpec((tm, LANES), lambda i: (i, 0))],
        out_shape=[jax.ShapeDtypeStruct((n, B_WIDTH), F32)] * 3
        + [jax.ShapeDtypeStruct((n, LANES), F32)],
        compiler_params=_cparams("arbitrary"),
        name="dn_prep",
    )(qkvb, qkvb, qkvb, bdraw, conv_w, par, bd)


def _dn_scan_kernel(qf, kf, vf, bgf, qb, kb, vb, bgb, of_ref, ob_ref, s_ref, *, prec, cps):
    @pl.when(pl.program_id(0) == 0)
    def _():
        s_ref[...] = jnp.zeros(s_ref.shape, F32)

    c, hc = B_CHUNK, B_HEADS * B_CHUNK
    row = lax.broadcasted_iota(jnp.int32, (hc, hc), 0)
    col = lax.broadcasted_iota(jnp.int32, (hc, hc), 1)
    same_head = (row // c) == (col // c)
    eye = (row == col).astype(F32)
    level_masks = [(row // 2) == (col // 2)]
    bs = 2
    while bs < c:
        level_masks.append(jnp.logical_and((row // (2 * bs)) == (col // (2 * bs)),
                                           (row // bs) != (col // bs)))
        bs *= 2
    rc, cc = lax.broadcasted_iota(jnp.int32, (c, c), 0), lax.broadcasted_iota(jnp.int32, (c, c), 1)

    def heads_bd(x):
        return jnp.where(same_head, jnp.concatenate([x] * B_HEADS, axis=0), 0.0)

    def rows_of(slab, lane0):
        return jnp.concatenate(
            [jnp.broadcast_to(slab[:, lane0 + h:lane0 + h + 1], (c, hc)) for h in range(B_HEADS)],
            axis=0)

    def chunk_setup(d, q_ref, k_ref, v_ref, bg_ref, r0):
        rows = slice(r0, r0 + c)
        incl = (col <= row) if d == 0 else (col >= row)
        strict = (col < row) if d == 0 else (col > row)
        incl_c = (cc <= rc) if d == 0 else (cc >= rc)
        last = c - 1 if d == 0 else 0
        lb = d * B_HEADS
        lg = 2 * B_HEADS + lb
        bg = bg_ref[rows, :]
        gcum = _mm_pieces(incl_c.astype(F32), bg, 3, split="b")
        gcum_t = gcum.T
        g_row = jnp.concatenate([gcum_t[lg + h:lg + h + 1, :] for h in range(B_HEADS)], axis=1)
        g_last = gcum[last:last + 1, :]
        decay = jnp.where(jnp.logical_and(same_head, incl),
                          jnp.exp(rows_of(gcum, lg) - g_row), 0.0)
        beta = rows_of(bg, lb)
        eg = rows_of(jnp.exp(gcum), lg)
        to_last = rows_of(jnp.exp(g_last - gcum), lg)
        q_bd = heads_bd(q_ref[rows, :])
        k_bd = heads_bd(k_ref[rows, :])
        v_bd = heads_bd(v_ref[rows, :])
        k_b = k_bd.astype(BF16)
        kk = _mm(k_b, k_b, prec, _NT)
        qk = _mm(q_bd, k_b, prec, _NT)
        a_mat = jnp.where(jnp.logical_and(same_head, strict), kk * beta * decay, 0.0)
        g_state = jnp.concatenate(
            [jnp.broadcast_to(jnp.exp(g_last[:, lg + h:lg + h + 1]), (c, hc))
             for h in range(B_HEADS)], axis=0)
        return dict(a_mat=a_mat.astype(BF16),
                    t=(eye - jnp.where(level_masks[0], a_mat, 0.0)).astype(BF16),
                    v_beta=(v_bd * beta).astype(BF16), k_beta=(k_bd * (beta * eg)).astype(BF16),
                    q_dec=(q_bd * eg).astype(BF16), attn=(qk * decay).astype(BF16),
                    k_dec=(k_bd * to_last).astype(BF16), g_state=g_state)

    dirs = ((qf, kf, vf, bgf, of_ref), (qb, kb, vb, bgb, ob_ref))
    chains = [chunk_setup(d, *dirs[d][:4], j * c) for j in range(cps) for d in range(2)]
    zero_b = jnp.zeros((), BF16)
    for lm in level_masks[1:]:
        part = [_mm(ch["t"], jnp.where(lm, ch["a_mat"], zero_b), prec) for ch in chains]
        for ch, pt in zip(chains, part):
            ch["t"] = ch["t"] - _mm(pt, ch["t"], prec).astype(BF16)
    for ch in chains:
        ch["w"] = _mm(ch["t"], ch["v_beta"], prec)
        ch["k_cum"] = _mm(ch["t"], ch["k_beta"], prec)

    states = [s_ref[d] for d in range(2)]
    for step in range(cps):
        sub = [step, cps - 1 - step]
        tms = [chains[sub[d] * 2 + d] for d in range(2)]
        us = [tms[d]["w"] - _mm(tms[d]["k_cum"], states[d], prec) for d in range(2)]
        for d in range(2):
            tm, u, s = tms[d], us[d], states[d]
            o_bd = _mm(tm["q_dec"], s, prec) + _mm(tm["attn"], u, prec)
            states[d] = s * tm["g_state"] + _mm(tm["k_dec"], u, prec, _TN)
            dirs[d][4][sub[d] * c:(sub[d] + 1) * c, :] = functools.reduce(
                jnp.add, [o_bd[h * c:(h + 1) * c] for h in range(B_HEADS)])
    for d in range(2):
        s_ref[d] = states[d]


def _scan_order_maps(n_chunks, n_ctx_chunks):
    fwd = lambda j: (j, 0)
    bwd = lambda j: (jnp.where(j < n_ctx_chunks, n_ctx_chunks - 1 - j,
                               n_chunks - 1 - (j - n_ctx_chunks)), 0)
    return fwd, bwd


def _dn_scan(qn, kn, vn, bg, n_ctx):
    n = qn.shape[0]
    assert B_CHUNK == HEAD_DIM
    c = B_CHUNK * DN_CHUNKS_PER_STEP
    fwd, bwd = _scan_order_maps(n // c, n_ctx // c)
    specs = lambda m: [pl.BlockSpec((c, B_WIDTH), m)] * 3 + [pl.BlockSpec((c, LANES), m)]
    return pl.pallas_call(
        functools.partial(_dn_scan_kernel, prec=DN_PREC, cps=DN_CHUNKS_PER_STEP),
        grid=(n // c,),
        in_specs=specs(fwd) + specs(bwd),
        out_specs=[pl.BlockSpec((c, B_WIDTH), fwd), pl.BlockSpec((c, B_WIDTH), bwd)],
        out_shape=[jax.ShapeDtypeStruct((n, B_WIDTH), F32)] * 2,
        scratch_shapes=[pltpu.VMEM((2, B_WIDTH, B_WIDTH), F32)],
        compiler_params=_cparams("arbitrary"),
        name="dn_scan",
    )(qn, kn, vn, bg, qn, kn, vn, bg)


def _s5_kernel(uf_ref, ub_ref, bre_ref, bim_ref, cre_ref, cimn_ref, a_ref, ap_ref,
               yf_ref, yb_ref, hre_sc, him_sc, carry_sc, *, prec):
    @pl.when(pl.program_id(0) == 0)
    def _():
        carry_sc[...] = jnp.zeros(carry_sc.shape, F32)

    steps, nsub = S5_STEPS, SUBLANES
    ts = steps * nsub
    prow = lax.broadcasted_iota(jnp.int32, (ts, ts), 0)
    pcol = lax.broadcasted_iota(jnp.int32, (ts, ts), 1)
    scan_pos = (prow % nsub) * steps + prow // nsub
    perms = [jnp.where(pcol == scan_pos, 1.0, 0.0).astype(BF16),
             jnp.where(pcol == ts - 1 - scan_pos, 1.0, 0.0).astype(BF16)]
    ndir = 2

    def permute(pm, x, dims, pieces):
        return _mm_pieces(pm, x, pieces, split="b", dims=dims)

    u_perms = [permute(perms[d], u_ref[...], ((1,), (0,)), 1 if prec == 1 else 2)
               for d, u_ref in enumerate((uf_ref, ub_ref))]
    for d in range(ndir):
        hre_sc[d] = _mm(u_perms[d], bre_ref[d], prec)
    for d in range(ndir):
        him_sc[d] = _mm(u_perms[d], bim_ref[d], prec)

    def rows_at(jj):
        return pl.ds(pl.multiple_of(jj * nsub, nsub), nsub)

    def local_scan(jj, carry):
        rows = rows_at(jj)
        out = []
        for d in range(ndir):
            a_re, a_im = a_ref[d, 0], a_ref[d, 1]
            h_re, h_im = carry[d]
            n_re = a_re * h_re - a_im * h_im + hre_sc[d, rows, :]
            n_im = a_re * h_im + a_im * h_re + him_sc[d, rows, :]
            hre_sc[d, rows, :] = n_re
            him_sc[d, rows, :] = n_im
            out.append((n_re, n_im))
        return tuple(out)

    zero = jnp.zeros((nsub, C_STATES), F32)
    ends = lax.fori_loop(0, steps, local_scan, ((zero, zero),) * ndir)
    cins = []
    for d in range(ndir):
        as_re = ap_ref[d, 0, steps - 1, 0:1, :]
        as_im = ap_ref[d, 1, steps - 1, 0:1, :]
        c_re, c_im = carry_sc[2 * d:2 * d + 1, :], carry_sc[2 * d + 1:2 * d + 2, :]
        cin = []
        for s in range(nsub):
            cin.append((c_re, c_im))
            c_re, c_im = (ends[d][0][s:s + 1] + as_re * c_re - as_im * c_im,
                          ends[d][1][s:s + 1] + as_re * c_im + as_im * c_re)
        carry_sc[2 * d:2 * d + 1, :] = c_re
        carry_sc[2 * d + 1:2 * d + 2, :] = c_im
        cins.append((jnp.concatenate([x[0] for x in cin], axis=0),
                     jnp.concatenate([x[1] for x in cin], axis=0)))

    def fixup(jj, carry):
        rows = rows_at(jj)
        for d in range(ndir):
            p_re = ap_ref[d, 0, jj]
            p_im = ap_ref[d, 1, jj]
            cin_re, cin_im = cins[d]
            hre_sc[d, rows, :] += p_re * cin_re - p_im * cin_im
            him_sc[d, rows, :] += p_re * cin_im + p_im * cin_re
        return carry

    lax.fori_loop(0, steps, fixup, 0)
    y_re = [_mm(hre_sc[d], cre_ref[d], prec) for d in range(ndir)]
    y_im = [_mm(him_sc[d], cimn_ref[d], prec) for d in range(ndir)]
    for d, y_ref in enumerate((yf_ref, yb_ref)):
        y_ref[...] = permute(perms[d], y_re[d] + y_im[d], _TN, 2)


def _s5_params(a_re, a_im, log_dt, b_re, b_im, c_re, c_im):
    dt = jnp.exp(log_dt.astype(F32))[..., None]
    lam_re, lam_im = a_re.astype(F32), a_im.astype(F32)
    k = jnp.arange(1, S5_STEPS + 1, dtype=F32)[None, :, None, None]
    mag = jnp.exp(lam_re[:, None] * dt[:, None] * k)
    ang = lam_im[:, None] * dt[:, None] * k
    ap = jnp.stack([mag * jnp.cos(ang), mag * jnp.sin(ang)], axis=1)
    ap = ap.reshape(2, 2, S5_STEPS, C_STATES)
    ab_re, ab_im = ap[:, 0, 0], ap[:, 1, 0]
    abar = jnp.stack([ab_re, ab_im], axis=1)
    x, y = ab_re.reshape(lam_re.shape) - 1.0, ab_im.reshape(lam_re.shape)
    den = lam_re * lam_re + lam_im * lam_im
    cf_re, cf_im = (x * lam_re + y * lam_im) / den, (y * lam_re - x * lam_im) / den
    bb_re = cf_re[..., None] * b_re - cf_im[..., None] * b_im
    bb_im = cf_re[..., None] * b_im + cf_im[..., None] * b_re
    eye = jnp.eye(C_GROUPS, dtype=F32)
    dense_b = lambda b: jnp.einsum('dgpc,gh->dgchp', b, eye).reshape(2, C_WIDTH, C_STATES)
    dense_c = lambda c: jnp.einsum('dgcp,gh->dgphc', c, eye).reshape(2, C_STATES, C_WIDTH)
    abar = jnp.broadcast_to(abar[:, :, None, :], (2, 2, SUBLANES, C_STATES))
    ap = jnp.broadcast_to(ap[:, :, :, None, :], (2, 2, S5_STEPS, SUBLANES, C_STATES))
    return (dense_b(bb_re), dense_b(bb_im), dense_c(c_re.astype(F32)),
            dense_c(-c_im.astype(F32)), abar, ap)


def _s5_scan(uc, params, n_ctx):
    n = uc.shape[0]
    ts = ROW_TILE
    bre, bim, cre, cimn, abar, ap = params
    fwd, bwd = _scan_order_maps(n // ts, n_ctx // ts)
    full = lambda a: pl.BlockSpec(a.shape, lambda j: (0,) * a.ndim,
                                  pipeline_mode=pl.Buffered(1))
    return pl.pallas_call(
        functools.partial(_s5_kernel, prec=S5_PREC),
        grid=(n // ts,),
        in_specs=[pl.BlockSpec((ts, C_WIDTH), fwd), pl.BlockSpec((ts, C_WIDTH), bwd),
                  full(bre), full(bim), full(cre), full(cimn), full(abar), full(ap)],
        out_specs=[pl.BlockSpec((ts, C_WIDTH), fwd), pl.BlockSpec((ts, C_WIDTH), bwd)],
        out_shape=[jax.ShapeDtypeStruct((n, C_WIDTH), F32)] * 2,
        scratch_shapes=[pltpu.VMEM((2, ts, C_STATES), F32), pltpu.VMEM((2, ts, C_STATES), F32),
                        pltpu.VMEM((SUBLANES, C_STATES), F32)],
        compiler_params=_cparams("arbitrary"),
        name="s5_scan",
    )(uc, uc, bre, bim, cre, cimn, abar, ap)


def _gelu_tanh(x):
    return 0.5 * x * (1.0 + jnp.tanh(math.sqrt(2.0 / math.pi) * (x + 0.044715 * (x * x * x))))


def _out_kernel(c_ref, l_ref, mod_ref, ao_ref, dof_ref, dob_ref, gb_ref, u_ref, yf_ref, yb_ref,
                gc_ref, vec_ref, bd_ref, gluw_ref, wo_ref, lnv_ref, o_ref, *, n_ctx_tiles,
                first_tile, d_model, alpha):
    is_ctx = pl.program_id(0) + first_tile < n_ctx_tiles
    vec = vec_ref[...]
    o = dof_ref[...] + dob_ref[...]
    ms = _mm_pieces(o * o, bd_ref[...], 2)
    dn = o * lax.rsqrt(ms + NORM_EPS) * vec[0:1] * _silu(gb_ref[...])
    z = _gelu_tanh(u_ref[...] * vec[1:2] + yf_ref[...] + yb_ref[...])
    lin = jnp.dot(z.astype(BF16), gluw_ref[...], preferred_element_type=F32) + vec[2:3]
    s5 = z * jax.nn.sigmoid(lin) * _silu(gc_ref[...])
    wo = wo_ref
    proj = (jnp.dot(ao_ref[...].astype(BF16), wo[:A_WIDTH, :], preferred_element_type=F32)
            + jnp.dot(dn.astype(BF16), wo[A_WIDTH:A_WIDTH + B_WIDTH, :],
                      preferred_element_type=F32)
            + jnp.dot(s5.astype(BF16), wo[A_WIDTH + B_WIDTH:, :], preferred_element_type=F32))
    gate = _mod_row(mod_ref, is_ctx)[:, 2 * d_model:]
    lnv = lnv_ref[...]
    x = jnp.where(is_ctx, c_ref[...], l_ref[...])
    o_ref[...] = _layer_norm(alpha * x + gate * proj) * lnv[0:1] + lnv[1:2]


def _out_proj(ctx_src, lat_src, lat_off, n, first_tile, mod_l, ao, dof, dob, gb, uc, yf, yb, gc,
              vec, gluw, wo, lnv, n_ctx_tiles, alpha):
    d = lat_src.shape[1]
    tm = ROW_TILE
    bd = _block_diag_const(B_WIDTH, 1.0 / HEAD_DIM)
    row = lambda w: pl.BlockSpec((tm, w), lambda i: (i + first_tile, 0))
    full = lambda a: pl.BlockSpec(a.shape, lambda i: (0,) * a.ndim)
    return pl.pallas_call(
        functools.partial(_out_kernel, n_ctx_tiles=n_ctx_tiles, first_tile=first_tile, d_model=d,
                          alpha=alpha),
        grid=(n // tm - first_tile,),
        in_specs=_row_source_specs(n_ctx_tiles, lat_off, d, first_tile)
        + [full(mod_l), row(A_WIDTH), row(B_WIDTH), row(B_WIDTH), row(B_WIDTH),
           row(C_WIDTH), row(C_WIDTH), row(C_WIDTH), row(C_WIDTH), full(vec), full(bd),
           full(gluw), full(wo), full(lnv)],
        out_specs=pl.BlockSpec((tm, d), lambda i: (i, 0)),
        out_shape=jax.ShapeDtypeStruct((n - first_tile * tm, d), F32),
        compiler_params=_cparams("arbitrary"),
        name="out_proj",
    )(ctx_src, lat_src, mod_l, ao, dof, dob, gb, uc, yf, yb, gc, vec, bd, gluw, wo, lnv)


def _rope_tables(n_lat, n_ctx):
    rows = n_lat // GRID_W
    row = jnp.repeat(jnp.arange(rows, dtype=F32), GRID_W)
    col = jnp.tile(jnp.arange(GRID_W, dtype=F32), rows)
    axis_dim = HEAD_DIM // 2
    inv = ROPE_THETA ** (-jnp.arange(0, axis_dim, 2, dtype=F32) / axis_dim)
    ar, ac = row[:, None] * inv, col[:, None] * inv
    ang = jnp.concatenate([ar, ar, ac, ac], -1)
    cos = jnp.concatenate([jnp.ones((n_ctx, HEAD_DIM), F32), jnp.cos(ang)], 0)
    sin = jnp.concatenate([jnp.zeros((n_ctx, HEAD_DIM), F32), jnp.sin(ang)], 0)
    upper = (jnp.arange(HEAD_DIM) % (HEAD_DIM // 2)) >= HEAD_DIM // 4
    sa = jnp.where(upper, 0.0, -sin)
    sb = jnp.where(upper, sin, 0.0)
    tile = lambda t: jnp.tile(t, (1, LANES // HEAD_DIM))
    return tile(cos), tile(sa), tile(sb)


def kernel(x, c, ctx, c_ctx, w_mod, b_mod, w_in, attn_q_gain, attn_k_gain, dn_conv_w, dn_A_log, dn_dt_bias, dn_out_gain, s5_A_re, s5_A_im, s5_log_dt, s5_B_re, s5_B_im, s5_C_re, s5_C_im, s5_D, glu_w, glu_b, w_out, ln_g, ln_b):
    batch, n_lat, d = x.shape
    n_ctx = ctx.shape[1]
    depth = w_mod.shape[0]
    assert batch == 1 and n_ctx % ROW_TILE == 0 and n_lat % ROW_TILE == 0 and n_lat % GRID_W == 0
    nct = n_ctx // ROW_TILE
    alpha = (2 * depth) ** 0.25

    n = n_ctx + n_lat
    cv = jnp.zeros((SUBLANES, d), F32).at[0].set(c[0]).at[1].set(c_ctx)
    mod = _modulation(cv, w_mod, b_mod)
    cos, sa, sb = _rope_tables(n_lat, n_ctx)
    lane_tile = lambda g: jnp.tile(g.astype(F32)[None, :], (1, LANES // HEAD_DIM))
    s5_par = jax.vmap(_s5_params)(s5_A_re, s5_A_im, s5_log_dt, s5_B_re, s5_B_im, s5_C_re, s5_C_im)

    ctx_src, lat_src, lat_off = ctx[0], x[0], 0
    for l in range(depth):
        (qz, kp, vt, ga, qkvb, gb, bdraw, uc, gc) = _inproj(
            ctx_src, lat_src, lat_off, n, mod[l], _pad_w_in(w_in[l]), (cos, sa, sb),
            lane_tile(attn_q_gain[l]), lane_tile(attn_k_gain[l]), nct)
        gain_product = jnp.max(jnp.abs(attn_q_gain[l])) * jnp.max(jnp.abs(attn_k_gain[l]))
        ao = _flash(qz, kp, vt, ga, n_ctx, gain_product)
        qn, kn, vn, bg = _dn_prep(qkvb, bdraw, dn_conv_w[l].astype(F32), dn_A_log[l],
                                  dn_dt_bias[l], n_ctx)
        dof, dob = _dn_scan(qn, kn, vn, bg, n_ctx)
        yf, yb = _s5_scan(uc, tuple(p[l] for p in s5_par), n_ctx)
        vec = jnp.zeros((SUBLANES, C_WIDTH), F32)
        vec = vec.at[0].set(jnp.tile(dn_out_gain[l].astype(F32), B_HEADS))
        vec = vec.at[1].set(s5_D[l].astype(F32)).at[2].set(glu_b[l].astype(F32))
        lnv = jnp.zeros((SUBLANES, d), F32).at[0].set(ln_g[l]).at[1].set(ln_b[l])
        first_tile = nct if l == depth - 1 else 0
        xall = _out_proj(ctx_src, lat_src, lat_off, n, first_tile, mod[l], ao, dof, dob, gb, uc,
                         yf, yb, gc, vec, glu_w[l].astype(BF16), w_out[l].astype(BF16), lnv, nct,
                         alpha)
        ctx_src, lat_src, lat_off = xall, xall, nct
    return xall[None]
```

```python
import functools
import math

import jax
import jax.numpy as jnp
from jax import lax
from jax.experimental import pallas as pl
from jax.experimental.pallas import tpu as pltpu

F32 = jnp.float32
BF16 = jnp.bfloat16
HIGHEST = lax.Precision.HIGHEST

GRID_W = 64
HEAD_DIM = 64
NORM_EPS = 1e-6
ROPE_THETA = 10000.0
A_Q_HEADS = 8
A_KV_HEADS = 2
A_GROUPS = A_Q_HEADS // A_KV_HEADS
A_WIDTH = A_Q_HEADS * HEAD_DIM
A_KV_WIDTH = A_KV_HEADS * HEAD_DIM
B_HEADS = 4
B_WIDTH = B_HEADS * HEAD_DIM
B_CHUNK = 64
C_WIDTH = 256
C_GROUP = 16
C_GROUPS = C_WIDTH // C_GROUP
C_STATE = 64
C_STATES = C_GROUPS * C_STATE
LANES = 128
SUBLANES = 8
ROW_TILE = 256
S5_STEPS = ROW_TILE // SUBLANES
VMEM_LIMIT = 48 * 1024 * 1024
LOG2E = math.log2(math.e)

DN_PREC = 1
S5_PREC = 1
DN_CHUNKS_PER_STEP = 4


def _cparams(*sem):
    return pltpu.CompilerParams(dimension_semantics=sem, vmem_limit_bytes=VMEM_LIMIT)


def _split(a):
    hi = a.astype(BF16)
    lo = (a - hi.astype(F32)).astype(BF16)
    return hi, lo


def _dg(a, b, dims):
    return lax.dot_general(a, b, (dims, ((), ())), preferred_element_type=F32)


def _mm(a, b, prec, dims=((1,), (0,))):
    if prec == 6:
        return lax.dot_general(a, b, (dims, ((), ())), precision=HIGHEST,
                               preferred_element_type=F32)
    if prec == 1:
        return _dg(a.astype(BF16), b.astype(BF16), dims)
    ah, al = _split(a)
    bh, bl = _split(b)
    return _dg(ah, bh, dims) + (_dg(ah, bl, dims) + _dg(al, bh, dims))


_NT = ((1,), (1,))
_TN = ((0,), (0,))


def _mm_pieces(a, b, pieces, split="a", dims=((1,), (0,))):
    exact = (b if split == "a" else a).astype(BF16)
    out, rest = None, (a if split == "a" else b)
    for _ in range(pieces):
        part = rest.astype(BF16)
        rest = rest - part.astype(F32)
        term = _dg(part, exact, dims) if split == "a" else _dg(exact, part, dims)
        out = term if out is None else out + term
    return out


def _silu(x):
    return x * jax.nn.sigmoid(x)


def _mod_kernel(cv_ref, w_ref, b_ref, o_ref):
    sv = _silu(cv_ref[...])
    o_ref[0] = _mm(sv, w_ref[0], 3) + b_ref[0]


def _modulation(cv, w_mod, b_mod):
    depth, d, d3 = w_mod.shape
    nblk = d3 // d
    return pl.pallas_call(
        _mod_kernel,
        grid=(depth, nblk),
        in_specs=[pl.BlockSpec((SUBLANES, d), lambda l, n: (0, 0)),
                  pl.BlockSpec((1, d, d), lambda l, n: (l, 0, n)),
                  pl.BlockSpec((1, 1, d), lambda l, n: (l, 0, n))],
        out_specs=pl.BlockSpec((1, SUBLANES, d), lambda l, n: (l, 0, n)),
        out_shape=jax.ShapeDtypeStruct((depth, SUBLANES, d3), F32),
        compiler_params=_cparams("arbitrary", "arbitrary"),
        name="modulation",
    )(cv, w_mod, b_mod.reshape(depth, 1, d3))


def _layer_norm(x):
    mu = jnp.mean(x, axis=-1, keepdims=True)
    xc = x - mu
    var = jnp.mean(xc * xc, axis=-1, keepdims=True)
    return xc * lax.rsqrt(var + NORM_EPS)


def _mod_row(mod_ref, is_ctx):
    mod = mod_ref[...]
    return jnp.where(is_ctx, mod[1:2], mod[0:1])


_ATTN_QKV_WIDTH = A_WIDTH + 2 * A_KV_WIDTH
_IN_SPLITS = (("ga", A_WIDTH), ("qkvb", 3 * B_WIDTH), ("gb", B_WIDTH), ("bd", LANES),
              ("uc", C_WIDTH), ("gc", C_WIDTH))


def _row_source_specs(n_ctx_tiles, lat_off, d, first_tile=0):
    tm = ROW_TILE
    return [pl.BlockSpec((tm, d), lambda i: (jnp.minimum(i + first_tile, n_ctx_tiles - 1), 0)),
            pl.BlockSpec((tm, d),
                         lambda i: (jnp.maximum(i + first_tile - n_ctx_tiles, 0) + lat_off, 0))]


def _attn_prep_tile(q, k, v, cos, sa, sb, gq, gk, bd, qo_ref, ko_ref, vo_ref):
    def norm_rope(y, gain):
        ms = _mm_pieces(y * y, bd, 2)
        yn = y * lax.rsqrt(ms + NORM_EPS) * gain
        return (yn * cos + pltpu.roll(yn, LANES - HEAD_DIM // 4, 1) * sa
                + pltpu.roll(yn, HEAD_DIM // 4, 1) * sb)

    lane = lax.broadcasted_iota(jnp.int32, (1, LANES), 1)
    lower = lane < HEAD_DIM
    for j in range(A_Q_HEADS // 2):
        y = norm_rope(q[:, j * LANES:(j + 1) * LANES], gq) * (HEAD_DIM ** -0.5 * LOG2E)
        y_sw = pltpu.roll(y, HEAD_DIM, 1)
        for half in range(2):
            h = 2 * j + half
            src = y if (h // A_GROUPS) == half else y_sw
            keep = lower if (h // A_GROUPS) == 0 else jnp.logical_not(lower)
            qo_ref[h] = jnp.where(keep, src, 0.0).astype(BF16)
    ko_ref[...] = norm_rope(k, gk).astype(BF16)
    vt = v.T
    ones = jnp.ones((HEAD_DIM, vt.shape[1]), F32)
    for h in range(A_KV_HEADS):
        vo_ref[h] = jnp.concatenate([vt[h * HEAD_DIM:(h + 1) * HEAD_DIM], ones],
                                    axis=0).astype(BF16)


def _inproj_kernel(c_ref, l_ref, mod_ref, w_ref, cos_ref, sa_ref, sb_ref, gq_ref, gk_ref, bd_ref,
                   qo_ref, ko_ref, vo_ref, *out_refs, n_ctx_tiles, d_model):
    is_ctx = pl.program_id(0) < n_ctx_tiles
    row = _mod_row(mod_ref, is_ctx)
    shift, scale = row[:, :d_model], row[:, d_model:2 * d_model]
    x = jnp.where(is_ctx, c_ref[...], l_ref[...])
    h = _layer_norm(x) * (1.0 + scale) + shift
    hb = h.astype(BF16)
    za = jnp.dot(hb, w_ref[:, :_ATTN_QKV_WIDTH], preferred_element_type=F32)
    _attn_prep_tile(za[:, :A_WIDTH], za[:, A_WIDTH:A_WIDTH + A_KV_WIDTH],
                    za[:, A_WIDTH + A_KV_WIDTH:], cos_ref[...], sa_ref[...],
                    sb_ref[...], gq_ref[...], gk_ref[...], bd_ref[...], qo_ref, ko_ref, vo_ref)
    off = _ATTN_QKV_WIDTH
    for ref, (_, w) in zip(out_refs, _IN_SPLITS):
        ref[...] = jnp.dot(hb, w_ref[:, off:off + w], preferred_element_type=F32)
        off += w


def _pad_w_in(w):
    d = w.shape[0]
    n_main = 2 * A_WIDTH + 2 * A_KV_WIDTH + 4 * B_WIDTH
    n_small = 4 * B_HEADS
    return jnp.concatenate(
        [w[:, :n_main], w[:, n_main:n_main + n_small], jnp.zeros((d, LANES - n_small), w.dtype),
         w[:, n_main + n_small:]], axis=1).astype(BF16)


def _block_diag_const(width, value):
    r = jnp.arange(width) // HEAD_DIM
    return jnp.where(r[:, None] == r[None, :], value, 0.0).astype(F32)


def _inproj(ctx_src, lat_src, lat_off, n, mod_l, w_pad, rope, gq, gk, n_ctx_tiles):
    d = lat_src.shape[1]
    tm = ROW_TILE
    bd = _block_diag_const(LANES, 1.0 / HEAD_DIM)
    row = lambda w: pl.BlockSpec((tm, w), lambda i: (i, 0))
    full = lambda a: pl.BlockSpec(a.shape, lambda i: (0,) * a.ndim)
    return pl.pallas_call(
        functools.partial(_inproj_kernel, n_ctx_tiles=n_ctx_tiles, d_model=d),
        grid=(n // tm,),
        in_specs=_row_source_specs(n_ctx_tiles, lat_off, d)
        + [full(mod_l), full(w_pad), row(LANES), row(LANES), row(LANES), full(gq), full(gk),
           full(bd)],
        out_specs=[pl.BlockSpec((A_Q_HEADS, tm, LANES), lambda i: (0, i, 0)),
                   row(LANES),
                   pl.BlockSpec((A_KV_HEADS, LANES, tm), lambda i: (0, 0, i))]
        + [row(w) for _, w in _IN_SPLITS],
        out_shape=[jax.ShapeDtypeStruct((A_Q_HEADS, n, LANES), BF16),
                   jax.ShapeDtypeStruct((n, LANES), BF16),
                   jax.ShapeDtypeStruct((A_KV_HEADS, LANES, n), BF16)]
        + [jax.ShapeDtypeStruct((n, w), F32) for _, w in _IN_SPLITS],
        compiler_params=_cparams("arbitrary"),
        name="inproj",
    )(ctx_src, lat_src, mod_l, w_pad, *rope, gq, gk, bd)


def _flash_kernel(q_ref, k_ref, vt_ref, g_ref, o_ref, m_sc, mblk_sc, alpha_sc, acc_sc, s_sc, p_sc,
                  *, n_ctx, n_ctx_tiles, tc, n_chunks):
    is_ctx = pl.program_id(0) < n_ctx_tiles
    tq = q_ref.shape[1]
    m_sc[...] = jnp.full(m_sc.shape, -jnp.inf, F32)
    acc_sc[...] = jnp.zeros(acc_sc.shape, F32)

    def q_group(g):
        return q_ref[g * A_GROUPS:(g + 1) * A_GROUPS].reshape(A_GROUPS * tq, LANES)

    def scores(g, c0, size):
        s = _dg(k_ref[pl.ds(c0, size), :], q_group(g), _NT)
        s_sc[g, :size, :] = s
        mblk_sc[g] = jnp.max(s, axis=0, keepdims=True)

    def exponentials(g, size):
        m_old = m_sc[g]
        m_new = jnp.maximum(m_old, mblk_sc[g])
        alpha_sc[g] = jnp.exp2(m_old - m_new)
        p_sc[g, :size, :] = jnp.exp2(s_sc[g, :size, :] - m_new).astype(BF16)
        m_sc[g] = m_new

    def weighted_values(g, c0, size):
        acc_sc[g] = alpha_sc[g] * acc_sc[g] + jnp.dot(
            vt_ref[g, :, pl.ds(c0, size)], p_sc[g, :size, :], preferred_element_type=F32)

    @pl.when(is_ctx)
    def _():
        for g in range(A_KV_HEADS):
            scores(g, 0, n_ctx)
            exponentials(g, n_ctx)
            weighted_values(g, 0, n_ctx)

    @pl.when(jnp.logical_not(is_ctx))
    def _():
        first, second = 0, A_KV_HEADS - 1
        p_sc[second] = jnp.zeros(p_sc.shape[1:], BF16)
        alpha_sc[second] = jnp.ones(alpha_sc.shape[1:], F32)
        scores(first, 0, tc)

        def body(c, carry):
            c0 = pl.multiple_of(c * tc, tc)
            c_prev = pl.multiple_of(jnp.maximum(c - 1, 0) * tc, tc)
            c_next = pl.multiple_of(jnp.minimum(c + 1, n_chunks - 1) * tc, tc)
            scores(second, c0, tc)
            exponentials(first, tc)
            weighted_values(second, c_prev, tc)
            scores(first, c_next, tc)
            exponentials(second, tc)
            weighted_values(first, c0, tc)
            return carry

        lax.fori_loop(0, n_chunks, body, 0)
        weighted_values(second, (n_chunks - 1) * tc, tc)

    _flash_finalize(acc_sc, g_ref, o_ref, tq)


def _flash_finalize(acc_sc, g_ref, o_ref, tq):
    heads = []
    for h in range(A_Q_HEADS):
        g, r = divmod(h, A_GROUPS)
        acc = acc_sc[g, :, r * tq:(r + 1) * tq]
        heads.append((acc[:HEAD_DIM] / acc[HEAD_DIM:HEAD_DIM + 1]).T)
    o_ref[...] = (jnp.concatenate(heads, axis=1) * _silu(g_ref[...])).astype(o_ref.dtype)


def _flash_bounded_kernel(u_ref, q_ref, k_ref, vt_ref, g_ref, o_ref, acc_sc, p_sc, *, n_ctx,
                          n_ctx_tiles, tc, n_chunks):
    is_ctx = pl.program_id(0) < n_ctx_tiles
    tq = q_ref.shape[1]
    acc_sc[...] = jnp.zeros(acc_sc.shape, F32)
    bound = u_ref[0]

    def q_group(g):
        return q_ref[g * A_GROUPS:(g + 1) * A_GROUPS].reshape(A_GROUPS * tq, LANES)

    def probabilities(g, c0, size):
        s = _dg(k_ref[pl.ds(c0, size), :], q_group(g), _NT)
        p_sc[g, :size, :] = jnp.exp2(s - bound).astype(BF16)

    def weighted_values(g, c0, size):
        acc_sc[g] += jnp.dot(vt_ref[g, :, pl.ds(c0, size)], p_sc[g, :size, :],
                             preferred_element_type=F32)

    @pl.when(is_ctx)
    def _():
        for g in range(A_KV_HEADS):
            probabilities(g, 0, n_ctx)
            weighted_values(g, 0, n_ctx)

    @pl.when(jnp.logical_not(is_ctx))
    def _():
        first, second = 0, A_KV_HEADS - 1
        probabilities(first, 0, tc)

        def body(c, carry):
            c0 = pl.multiple_of(c * tc, tc)
            probabilities(second, c0, tc)
            weighted_values(first, c0, tc)
            probabilities(first, pl.multiple_of(c0 + tc, tc), tc)
            weighted_values(second, c0, tc)
            return carry

        lax.fori_loop(0, n_chunks - 1, body, 0)
        last = (n_chunks - 1) * tc
        probabilities(second, last, tc)
        weighted_values(first, last, tc)
        weighted_values(second, last, tc)

    _flash_finalize(acc_sc, g_ref, o_ref, tq)


def _key_chunk(n):
    for tc in (1280, 1024, 640, 512, 256, 128):
        if n % tc == 0:
            return tc
    raise ValueError(f"no key chunk for {n} rows")


_SCORE_BOUND_PER_GAIN = 1.01 * HEAD_DIM * HEAD_DIM ** -0.5 * LOG2E
_MAX_FIXED_REFERENCE = 40.0


def _flash(qz, k, vt, ga, n_ctx, gain_product):
    n = k.shape[0]
    tq = ROW_TILE
    tc = _key_chunk(n)
    gq = A_GROUPS * tq
    assert A_KV_HEADS == 2 and n_ctx <= tc
    full = lambda a: pl.BlockSpec(a.shape, lambda i: (0,) * a.ndim,
                                  pipeline_mode=pl.Buffered(1))
    statics = dict(n_ctx=n_ctx, n_ctx_tiles=n_ctx // tq, tc=tc, n_chunks=n // tc)
    common = dict(
        grid=(n // tq,),
        out_specs=pl.BlockSpec((tq, A_WIDTH), lambda i: (i, 0)),
        out_shape=jax.ShapeDtypeStruct((n, A_WIDTH), BF16),
        compiler_params=_cparams("arbitrary"))
    in_specs = [pl.BlockSpec((A_Q_HEADS, tq, LANES), lambda i: (0, i, 0)), full(k), full(vt),
                pl.BlockSpec((tq, A_WIDTH), lambda i: (i, 0))]
    acc_shape = pltpu.VMEM((A_KV_HEADS, LANES, gq), F32)
    p_shape = pltpu.VMEM((A_KV_HEADS, tc, gq), BF16)

    def online(bound):
        del bound
        return pl.pallas_call(
            functools.partial(_flash_kernel, **statics), in_specs=in_specs,
            scratch_shapes=[pltpu.VMEM((A_KV_HEADS, 1, gq), F32),
                            pltpu.VMEM((A_KV_HEADS, 1, gq), F32),
                            pltpu.VMEM((A_KV_HEADS, 1, gq), F32),
                            acc_shape,
                            pltpu.VMEM((A_KV_HEADS, tc, gq), F32),
                            p_shape],
            name="flash_attn_online", **common)(qz, k, vt, ga)

    def bounded(bound):
        return pl.pallas_call(
            functools.partial(_flash_bounded_kernel, **statics),
            in_specs=[pl.BlockSpec(memory_space=pltpu.SMEM)] + in_specs,
            scratch_shapes=[acc_shape, p_shape],
            name="flash_attn_bounded", **common)(bound, qz, k, vt, ga)

    bound = (_SCORE_BOUND_PER_GAIN * gain_product).astype(F32).reshape(1)
    return lax.cond(bound[0] <= _MAX_FIXED_REFERENCE, bounded, online, bound)


def _dn_prep_kernel(z_ref, zp_ref, zn_ref, r_ref, w_ref, par_ref, bd_ref,
                    q_o, k_o, v_o, bg_o, *, seq_starts, seq_ends):
    i = pl.program_id(0)
    z = z_ref[...]
    tm = z.shape[0]
    is_start = functools.reduce(jnp.logical_or, [i == s for s in seq_starts])
    is_end = functools.reduce(jnp.logical_or, [i == s for s in seq_ends])
    prev_row = jnp.where(is_start, 0.0, zp_ref[SUBLANES - 1:SUBLANES, :])
    next_row = jnp.where(is_end, 0.0, zn_ref[0:1, :])
    row = lax.broadcasted_iota(jnp.int32, (tm, 1), 0)
    z_prev = jnp.where(row == 0, prev_row, pltpu.roll(z, 1, 0))
    z_next = jnp.where(row == tm - 1, next_row, pltpu.roll(z, tm - 1, 0))
    w = w_ref[...]
    y = _silu(z_prev * w[0:1] + z * w[1:2] + z_next * w[2:3])
    bd = bd_ref[...]

    def l2norm(t):
        return t * lax.rsqrt(_mm_pieces(t * t, bd, 2) + NORM_EPS)

    q_o[...] = l2norm(y[:, :B_WIDTH]) * (HEAD_DIM ** -0.5)
    k_o[...] = l2norm(y[:, B_WIDTH:2 * B_WIDTH])
    v_o[...] = y[:, 2 * B_WIDTH:]
    r = r_ref[...]
    par = par_ref[...]
    xa = r + par[1:2]
    softplus = jnp.maximum(xa, 0.0) + jnp.log(1.0 + jnp.exp(-jnp.abs(xa)))
    g = -jnp.exp(par[0:1]) * softplus
    lane = lax.broadcasted_iota(jnp.int32, (1, LANES), 1)
    bg_o[...] = jnp.where(lane < 2 * B_HEADS, jax.nn.sigmoid(r), g)


def _dn_prep(qkvb, bdraw, conv_w, a_log, dt_bias, n_ctx):
    n = qkvb.shape[0]
    tm = ROW_TILE
    nt, nct = n // tm, n_ctx // tm
    r8 = tm // SUBLANES
    par = jnp.zeros((2, LANES), F32)
    par = par.at[0, 2 * B_HEADS:4 * B_HEADS].set(a_log.reshape(-1))
    par = par.at[1, 2 * B_HEADS:4 * B_HEADS].set(dt_bias.reshape(-1))
    bd = _block_diag_const(B_WIDTH, 1.0)
    w3 = 3 * B_WIDTH
    return pl.pallas_call(
        functools.partial(_dn_prep_kernel, seq_starts=(0, nct), seq_ends=(nct - 1, nt - 1)),
        grid=(nt,),
        in_specs=[pl.BlockSpec((tm, w3), lambda i: (i, 0)),
                  pl.BlockSpec((SUBLANES, w3), lambda i: (jnp.maximum(i * r8 - 1, 0), 0)),
                  pl.BlockSpec((SUBLANES, w3),
                               lambda i: (jnp.minimum((i + 1) * r8, n // SUBLANES - 1), 0)),
                  pl.BlockSpec((tm, LANES), lambda i: (i, 0)),
                  pl.BlockSpec(conv_w.shape, lambda i: (0, 0)),
                  pl.BlockSpec(par.shape, lambda i: (0, 0)),
                  pl.BlockSpec(bd.shape, lambda i: (0, 0))],
        out_specs=[pl.BlockSpec((tm, B_WIDTH), lambda i: (i, 0))] * 3
        + [pl.BlockSpec((tm, LANES), lambda i: (i, 0))],
        out_shape=[jax.ShapeDtypeStruct((n, B_WIDTH), F32)] * 3
        + [jax.ShapeDtypeStruct((n, LANES), F32)],
        compiler_params=_cparams("arbitrary"),
        name="dn_prep",
    )(qkvb, qkvb, qkvb, bdraw, conv_w, par, bd)


def _dn_scan_kernel(qf, kf, vf, bgf, qb, kb, vb, bgb, of_ref, ob_ref, s_ref, *, prec, cps):
    @pl.when(pl.program_id(0) == 0)
    def _():
        s_ref[...] = jnp.zeros(s_ref.shape, F32)

    c, hc = B_CHUNK, B_HEADS * B_CHUNK
    row = lax.broadcasted_iota(jnp.int32, (hc, hc), 0)
    col = lax.broadcasted_iota(jnp.int32, (hc, hc), 1)
    same_head = (row // c) == (col // c)
    eye = (row == col).astype(F32)
    level_masks = [(row // 2) == (col // 2)]
    bs = 2
    while bs < c:
        level_masks.append(jnp.logical_and((row // (2 * bs)) == (col // (2 * bs)),
                                           (row // bs) != (col // bs)))
        bs *= 2
    rc, cc = lax.broadcasted_iota(jnp.int32, (c, c), 0), lax.broadcasted_iota(jnp.int32, (c, c), 1)

    def heads_bd(x):
        return jnp.where(same_head, jnp.concatenate([x] * B_HEADS, axis=0), 0.0)

    def rows_of(slab, lane0):
        return jnp.concatenate(
            [jnp.broadcast_to(slab[:, lane0 + h:lane0 + h + 1], (c, hc)) for h in range(B_HEADS)],
            axis=0)

    def chunk_setup(d, q_ref, k_ref, v_ref, bg_ref, r0):
        rows = slice(r0, r0 + c)
        incl = (col <= row) if d == 0 else (col >= row)
        strict = (col < row) if d == 0 else (col > row)
        incl_c = (cc <= rc) if d == 0 else (cc >= rc)
        last = c - 1 if d == 0 else 0
        lb = d * B_HEADS
        lg = 2 * B_HEADS + lb
        bg = bg_ref[rows, :]
        gcum = _mm_pieces(incl_c.astype(F32), bg, 3, split="b")
        gcum_t = gcum.T
        g_row = jnp.concatenate([gcum_t[lg + h:lg + h + 1, :] for h in range(B_HEADS)], axis=1)
        g_last = gcum[last:last + 1, :]
        decay = jnp.where(jnp.logical_and(same_head, incl),
                          jnp.exp(rows_of(gcum, lg) - g_row), 0.0)
        beta = rows_of(bg, lb)
        eg = rows_of(jnp.exp(gcum), lg)
        to_last = rows_of(jnp.exp(g_last - gcum), lg)
        q_bd = heads_bd(q_ref[rows, :])
        k_bd = heads_bd(k_ref[rows, :])
        v_bd = heads_bd(v_ref[rows, :])
        k_b = k_bd.astype(BF16)
        kk = _mm(k_b, k_b, prec, _NT)
        qk = _mm(q_bd, k_b, prec, _NT)
        a_mat = jnp.where(jnp.logical_and(same_head, strict), kk * beta * decay, 0.0)
        g_state = jnp.concatenate(
            [jnp.broadcast_to(jnp.exp(g_last[:, lg + h:lg + h + 1]), (c, hc))
             for h in range(B_HEADS)], axis=0)
        return dict(a_mat=a_mat.astype(BF16),
                    t=(eye - jnp.where(level_masks[0], a_mat, 0.0)).astype(BF16),
                    v_beta=(v_bd * beta).astype(BF16), k_beta=(k_bd * (beta * eg)).astype(BF16),
                    q_dec=(q_bd * eg).astype(BF16), attn=(qk * decay).astype(BF16),
                    k_dec=(k_bd * to_last).astype(BF16), g_state=g_state)

    dirs = ((qf, kf, vf, bgf, of_ref), (qb, kb, vb, bgb, ob_ref))
    chains = [chunk_setup(d, *dirs[d][:4], j * c) for j in range(cps) for d in range(2)]
    zero_b = jnp.zeros((), BF16)
    for lm in level_masks[1:]:
        part = [_mm(ch["t"], jnp.where(lm, ch["a_mat"], zero_b), prec) for ch in chains]
        for ch, pt in zip(chains, part):
            ch["t"] = ch["t"] - _mm(pt, ch["t"], prec).astype(BF16)
    for ch in chains:
        ch["w"] = _mm(ch["t"], ch["v_beta"], prec)
        ch["k_cum"] = _mm(ch["t"], ch["k_beta"], prec)

    states = [s_ref[d] for d in range(2)]
    for step in range(cps):
        sub = [step, cps - 1 - step]
        tms = [chains[sub[d] * 2 + d] for d in range(2)]
        us = [tms[d]["w"] - _mm(tms[d]["k_cum"], states[d], prec) for d in range(2)]
        for d in range(2):
            tm, u, s = tms[d], us[d], states[d]
            o_bd = _mm(tm["q_dec"], s, prec) + _mm(tm["attn"], u, prec)
            states[d] = s * tm["g_state"] + _mm(tm["k_dec"], u, prec, _TN)
            dirs[d][4][sub[d] * c:(sub[d] + 1) * c, :] = functools.reduce(
                jnp.add, [o_bd[h * c:(h + 1) * c] for h in range(B_HEADS)])
    for d in range(2):
        s_ref[d] = states[d]


def _scan_order_maps(n_chunks, n_ctx_chunks):
    fwd = lambda j: (j, 0)
    bwd = lambda j: (jnp.where(j < n_ctx_chunks, n_ctx_chunks - 1 - j,
                               n_chunks - 1 - (j - n_ctx_chunks)), 0)
    return fwd, bwd


def _dn_scan(qn, kn, vn, bg, n_ctx):
    n = qn.shape[0]
    assert B_CHUNK == HEAD_DIM
    c = B_CHUNK * DN_CHUNKS_PER_STEP
    fwd, bwd = _scan_order_maps(n // c, n_ctx // c)
    specs = lambda m: [pl.BlockSpec((c, B_WIDTH), m)] * 3 + [pl.BlockSpec((c, LANES), m)]
    return pl.pallas_call(
        functools.partial(_dn_scan_kernel, prec=DN_PREC, cps=DN_CHUNKS_PER_STEP),
        grid=(n // c,),
        in_specs=specs(fwd) + specs(bwd),
        out_specs=[pl.BlockSpec((c, B_WIDTH), fwd), pl.BlockSpec((c, B_WIDTH), bwd)],
        out_shape=[jax.ShapeDtypeStruct((n, B_WIDTH), F32)] * 2,
        scratch_shapes=[pltpu.VMEM((2, B_WIDTH, B_WIDTH), F32)],
        compiler_params=_cparams("arbitrary"),
        name="dn_scan",
    )(qn, kn, vn, bg, qn, kn, vn, bg)


def _s5_kernel(uf_ref, ub_ref, bre_ref, bim_ref, cre_ref, cimn_ref, a_ref, ap_ref,
               yf_ref, yb_ref, hre_sc, him_sc, carry_sc, *, prec):
    @pl.when(pl.program_id(0) == 0)
    def _():
        carry_sc[...] = jnp.zeros(carry_sc.shape, F32)

    steps, nsub = S5_STEPS, SUBLANES
    ts = steps * nsub
    prow = lax.broadcasted_iota(jnp.int32, (ts, ts), 0)
    pcol = lax.broadcasted_iota(jnp.int32, (ts, ts), 1)
    scan_pos = (prow % nsub) * steps + prow // nsub
    perms = [jnp.where(pcol == scan_pos, 1.0, 0.0).astype(BF16),
             jnp.where(pcol == ts - 1 - scan_pos, 1.0, 0.0).astype(BF16)]
    ndir = 2

    def permute(pm, x, dims, pieces):
        return _mm_pieces(pm, x, pieces, split="b", dims=dims)

    u_perms = [permute(perms[d], u_ref[...], ((1,), (0,)), 1 if prec == 1 else 2)
               for d, u_ref in enumerate((uf_ref, ub_ref))]
    for d in range(ndir):
        hre_sc[d] = _mm(u_perms[d], bre_ref[d], prec)
    for d in range(ndir):
        him_sc[d] = _mm(u_perms[d], bim_ref[d], prec)

    def rows_at(jj):
        return pl.ds(pl.multiple_of(jj * nsub, nsub), nsub)

    def local_scan(jj, carry):
        rows = rows_at(jj)
        out = []
        for d in range(ndir):
            a_re, a_im = a_ref[d, 0], a_ref[d, 1]
            h_re, h_im = carry[d]
            n_re = a_re * h_re - a_im * h_im + hre_sc[d, rows, :]
            n_im = a_re * h_im + a_im * h_re + him_sc[d, rows, :]
            hre_sc[d, rows, :] = n_re
            him_sc[d, rows, :] = n_im
            out.append((n_re, n_im))
        return tuple(out)

    zero = jnp.zeros((nsub, C_STATES), F32)
    ends = lax.fori_loop(0, steps, local_scan, ((zero, zero),) * ndir)
    cins = []
    for d in range(ndir):
        as_re = ap_ref[d, 0, steps - 1, 0:1, :]
        as_im = ap_ref[d, 1, steps - 1, 0:1, :]
        c_re, c_im = carry_sc[2 * d:2 * d + 1, :], carry_sc[2 * d + 1:2 * d + 2, :]
        cin = []
        for s in range(nsub):
            cin.append((c_re, c_im))
            c_re, c_im = (ends[d][0][s:s + 1] + as_re * c_re - as_im * c_im,
                          ends[d][1][s:s + 1] + as_re * c_im + as_im * c_re)
        carry_sc[2 * d:2 * d + 1, :] = c_re
        carry_sc[2 * d + 1:2 * d + 2, :] = c_im
        cins.append((jnp.concatenate([x[0] for x in cin], axis=0),
                     jnp.concatenate([x[1] for x in cin], axis=0)))

    def fixup(jj, carry):
        rows = rows_at(jj)
        for d in range(ndir):
            p_re = ap_ref[d, 0, jj]
            p_im = ap_ref[d, 1, jj]
            cin_re, cin_im = cins[d]
            hre_sc[d, rows, :] += p_re * cin_re - p_im * cin_im
            him_sc[d, rows, :] += p_re * cin_im + p_im * cin_re
        return carry

    lax.fori_loop(0, steps, fixup, 0)
    y_re = [_mm(hre_sc[d], cre_ref[d], prec) for d in range(ndir)]
    y_im = [_mm(him_sc[d], cimn_ref[d], prec) for d in range(ndir)]
    for d, y_ref in enumerate((yf_ref, yb_ref)):
        y_ref[...] = permute(perms[d], y_re[d] + y_im[d], _TN, 2)


def _s5_params(a_re, a_im, log_dt, b_re, b_im, c_re, c_im):
    dt = jnp.exp(log_dt.astype(F32))[..., None]
    lam_re, lam_im = a_re.astype(F32), a_im.astype(F32)
    k = jnp.arange(1, S5_STEPS + 1, dtype=F32)[None, :, None, None]
    mag = jnp.exp(lam_re[:, None] * dt[:, None] * k)
    ang = lam_im[:, None] * dt[:, None] * k
    ap = jnp.stack([mag * jnp.cos(ang), mag * jnp.sin(ang)], axis=1)
    ap = ap.reshape(2, 2, S5_STEPS, C_STATES)
    ab_re, ab_im = ap[:, 0, 0], ap[:, 1, 0]
    abar = jnp.stack([ab_re, ab_im], axis=1)
    x, y = ab_re.reshape(lam_re.shape) - 1.0, ab_im.reshape(lam_re.shape)
    den = lam_re * lam_re + lam_im * lam_im
    cf_re, cf_im = (x * lam_re + y * lam_im) / den, (y * lam_re - x * lam_im) / den
    bb_re = cf_re[..., None] * b_re - cf_im[..., None] * b_im
    bb_im = cf_re[..., None] * b_im + cf_im[..., None] * b_re
    eye = jnp.eye(C_GROUPS, dtype=F32)
    dense_b = lambda b: jnp.einsum('dgpc,gh->dgchp', b, eye).reshape(2, C_WIDTH, C_STATES)
    dense_c = lambda c: jnp.einsum('dgcp,gh->dgphc', c, eye).reshape(2, C_STATES, C_WIDTH)
    abar = jnp.broadcast_to(abar[:, :, None, :], (2, 2, SUBLANES, C_STATES))
    ap = jnp.broadcast_to(ap[:, :, :, None, :], (2, 2, S5_STEPS, SUBLANES, C_STATES))
    return (dense_b(bb_re), dense_b(bb_im), dense_c(c_re.astype(F32)),
            dense_c(-c_im.astype(F32)), abar, ap)


def _s5_scan(uc, params, n_ctx):
    n = uc.shape[0]
    ts = ROW_TILE
    bre, bim, cre, cimn, abar, ap = params
    fwd, bwd = _scan_order_maps(n // ts, n_ctx // ts)
    full = lambda a: pl.BlockSpec(a.shape, lambda j: (0,) * a.ndim,
                                  pipeline_mode=pl.Buffered(1))
    return pl.pallas_call(
        functools.partial(_s5_kernel, prec=S5_PREC),
        grid=(n // ts,),
        in_specs=[pl.BlockSpec((ts, C_WIDTH), fwd), pl.BlockSpec((ts, C_WIDTH), bwd),
                  full(bre), full(bim), full(cre), full(cimn), full(abar), full(ap)],
        out_specs=[pl.BlockSpec((ts, C_WIDTH), fwd), pl.BlockSpec((ts, C_WIDTH), bwd)],
        out_shape=[jax.ShapeDtypeStruct((n, C_WIDTH), F32)] * 2,
        scratch_shapes=[pltpu.VMEM((2, ts, C_STATES), F32), pltpu.VMEM((2, ts, C_STATES), F32),
                        pltpu.VMEM((SUBLANES, C_STATES), F32)],
        compiler_params=_cparams("arbitrary"),
        name="s5_scan",
    )(uc, uc, bre, bim, cre, cimn, abar, ap)


def _gelu_tanh(x):
    return 0.5 * x * (1.0 + jnp.tanh(math.sqrt(2.0 / math.pi) * (x + 0.044715 * (x * x * x))))


def _out_kernel(c_ref, l_ref, mod_ref, ao_ref, dof_ref, dob_ref, gb_ref, u_ref, yf_ref, yb_ref,
                gc_ref, vec_ref, bd_ref, gluw_ref, wo_ref, lnv_ref, o_ref, *, n_ctx_tiles,
                first_tile, d_model, alpha):
    is_ctx = pl.program_id(0) + first_tile < n_ctx_tiles
    vec = vec_ref[...]
    o = dof_ref[...] + dob_ref[...]
    ms = _mm_pieces(o * o, bd_ref[...], 2)
    dn = o * lax.rsqrt(ms + NORM_EPS) * vec[0:1] * _silu(gb_ref[...])
    z = _gelu_tanh(u_ref[...] * vec[1:2] + yf_ref[...] + yb_ref[...])
    lin = jnp.dot(z.astype(BF16), gluw_ref[...], preferred_element_type=F32) + vec[2:3]
    s5 = z * jax.nn.sigmoid(lin) * _silu(gc_ref[...])
    wo = wo_ref
    proj = (jnp.dot(ao_ref[...], wo[:A_WIDTH, :], preferred_element_type=F32)
            + jnp.dot(dn.astype(BF16), wo[A_WIDTH:A_WIDTH + B_WIDTH, :],
                      preferred_element_type=F32)
            + jnp.dot(s5.astype(BF16), wo[A_WIDTH + B_WIDTH:, :], preferred_element_type=F32))
    gate = _mod_row(mod_ref, is_ctx)[:, 2 * d_model:]
    lnv = lnv_ref[...]
    x = jnp.where(is_ctx, c_ref[...], l_ref[...])
    o_ref[...] = _layer_norm(alpha * x + gate * proj) * lnv[0:1] + lnv[1:2]


def _out_proj(ctx_src, lat_src, lat_off, n, first_tile, mod_l, ao, dof, dob, gb, uc, yf, yb, gc,
              vec, gluw, wo, lnv, n_ctx_tiles, alpha):
    d = lat_src.shape[1]
    tm = ROW_TILE
    bd = _block_diag_const(B_WIDTH, 1.0 / HEAD_DIM)
    row = lambda w: pl.BlockSpec((tm, w), lambda i: (i + first_tile, 0))
    full = lambda a: pl.BlockSpec(a.shape, lambda i: (0,) * a.ndim)
    return pl.pallas_call(
        functools.partial(_out_kernel, n_ctx_tiles=n_ctx_tiles, first_tile=first_tile, d_model=d,
                          alpha=alpha),
        grid=(n // tm - first_tile,),
        in_specs=_row_source_specs(n_ctx_tiles, lat_off, d, first_tile)
        + [full(mod_l), row(A_WIDTH), row(B_WIDTH), row(B_WIDTH), row(B_WIDTH),
           row(C_WIDTH), row(C_WIDTH), row(C_WIDTH), row(C_WIDTH), full(vec), full(bd),
           full(gluw), full(wo), full(lnv)],
        out_specs=pl.BlockSpec((tm, d), lambda i: (i, 0)),
        out_shape=jax.ShapeDtypeStruct((n - first_tile * tm, d), F32),
        compiler_params=_cparams("arbitrary"),
        name="out_proj",
    )(ctx_src, lat_src, mod_l, ao, dof, dob, gb, uc, yf, yb, gc, vec, bd, gluw, wo, lnv)


def _rope_tables(n_lat, n_ctx):
    rows = n_lat // GRID_W
    axis_dim = HEAD_DIM // 2
    inv = ROPE_THETA ** (-jnp.arange(0, axis_dim, 2, dtype=F32) / axis_dim)
    ar = jnp.arange(rows, dtype=F32)[:, None] * inv
    ac = jnp.arange(GRID_W, dtype=F32)[:, None] * inv

    def expand(fr, fc):
        r = jnp.broadcast_to(jnp.tile(fr, (1, 2))[:, None, :], (rows, GRID_W, axis_dim))
        c = jnp.broadcast_to(jnp.tile(fc, (1, 2))[None, :, :], (rows, GRID_W, axis_dim))
        return jnp.concatenate([r, c], -1).reshape(n_lat, HEAD_DIM)

    cos = jnp.concatenate([jnp.ones((n_ctx, HEAD_DIM), F32),
                           expand(jnp.cos(ar), jnp.cos(ac))], 0)
    sin = jnp.concatenate([jnp.zeros((n_ctx, HEAD_DIM), F32),
                           expand(jnp.sin(ar), jnp.sin(ac))], 0)
    upper = (jnp.arange(HEAD_DIM) % (HEAD_DIM // 2)) >= HEAD_DIM // 4
    sa = jnp.where(upper, 0.0, -sin)
    sb = jnp.where(upper, sin, 0.0)
    tile = lambda t: jnp.tile(t, (1, LANES // HEAD_DIM))
    return tile(cos), tile(sa), tile(sb)


def kernel(x, c, ctx, c_ctx, w_mod, b_mod, w_in, attn_q_gain, attn_k_gain, dn_conv_w, dn_A_log, dn_dt_bias, dn_out_gain, s5_A_re, s5_A_im, s5_log_dt, s5_B_re, s5_B_im, s5_C_re, s5_C_im, s5_D, glu_w, glu_b, w_out, ln_g, ln_b):
    batch, n_lat, d = x.shape
    n_ctx = ctx.shape[1]
    depth = w_mod.shape[0]
    assert batch == 1 and n_ctx % ROW_TILE == 0 and n_lat % ROW_TILE == 0 and n_lat % GRID_W == 0
    nct = n_ctx // ROW_TILE
    alpha = (2 * depth) ** 0.25

    n = n_ctx + n_lat
    cv = jnp.zeros((SUBLANES, d), F32).at[0].set(c[0]).at[1].set(c_ctx)
    mod = _modulation(cv, w_mod, b_mod)
    cos, sa, sb = _rope_tables(n_lat, n_ctx)
    lane_tile = lambda g: jnp.tile(g.astype(F32)[None, :], (1, LANES // HEAD_DIM))
    s5_par = jax.vmap(_s5_params)(s5_A_re, s5_A_im, s5_log_dt, s5_B_re, s5_B_im, s5_C_re, s5_C_im)

    ctx_src, lat_src, lat_off = ctx[0], x[0], 0
    for l in range(depth):
        (qz, kp, vt, ga, qkvb, gb, bdraw, uc, gc) = _inproj(
            ctx_src, lat_src, lat_off, n, mod[l], _pad_w_in(w_in[l]), (cos, sa, sb),
            lane_tile(attn_q_gain[l]), lane_tile(attn_k_gain[l]), nct)
        gain_product = jnp.max(jnp.abs(attn_q_gain[l])) * jnp.max(jnp.abs(attn_k_gain[l]))
        ao = _flash(qz, kp, vt, ga, n_ctx, gain_product)
        qn, kn, vn, bg = _dn_prep(qkvb, bdraw, dn_conv_w[l].astype(F32), dn_A_log[l],
                                  dn_dt_bias[l], n_ctx)
        dof, dob = _dn_scan(qn, kn, vn, bg, n_ctx)
        yf, yb = _s5_scan(uc, tuple(p[l] for p in s5_par), n_ctx)
        vec = jnp.zeros((SUBLANES, C_WIDTH), F32)
        vec = vec.at[0].set(jnp.tile(dn_out_gain[l].astype(F32), B_HEADS))
        vec = vec.at[1].set(s5_D[l].astype(F32)).at[2].set(glu_b[l].astype(F32))
        lnv = jnp.zeros((SUBLANES, d), F32).at[0].set(ln_g[l]).at[1].set(ln_b[l])
        first_tile = nct if l == depth - 1 else 0
        xall = _out_proj(ctx_src, lat_src, lat_off, n, first_tile, mod[l], ao, dof, dob, gb, uc,
                         yf, yb, gc, vec, glu_w[l].astype(BF16), w_out[l].astype(BF16), lnv, nct,
                         alpha)
        ctx_src, lat_src, lat_off = xall, xall, nct
    return xall[None]
```

```python
import functools
import math

import jax
import jax.numpy as jnp
from jax import lax
from jax.experimental import pallas as pl
from jax.experimental.pallas import tpu as pltpu

F32 = jnp.float32
BF16 = jnp.bfloat16
HIGHEST = lax.Precision.HIGHEST

GRID_W = 64
HEAD_DIM = 64
NORM_EPS = 1e-6
ROPE_THETA = 10000.0
A_Q_HEADS = 8
A_KV_HEADS = 2
A_GROUPS = A_Q_HEADS // A_KV_HEADS
A_WIDTH = A_Q_HEADS * HEAD_DIM
A_KV_WIDTH = A_KV_HEADS * HEAD_DIM
B_HEADS = 4
B_WIDTH = B_HEADS * HEAD_DIM
B_CHUNK = 64
C_WIDTH = 256
C_GROUP = 16
C_GROUPS = C_WIDTH // C_GROUP
C_STATE = 64
C_STATES = C_GROUPS * C_STATE
LANES = 128
SUBLANES = 8
ROW_TILE = 256
S5_STEPS = ROW_TILE // SUBLANES
VMEM_LIMIT = 48 * 1024 * 1024
LOG2E = math.log2(math.e)

DN_PREC = 1
S5_PREC = 1
DN_CHUNKS_PER_STEP = 4


def _cparams(*sem):
    return pltpu.CompilerParams(dimension_semantics=sem, vmem_limit_bytes=VMEM_LIMIT)


def _split(a):
    hi = a.astype(BF16)
    lo = (a - hi.astype(F32)).astype(BF16)
    return hi, lo


def _dg(a, b, dims):
    return lax.dot_general(a, b, (dims, ((), ())), preferred_element_type=F32)


def _mm(a, b, prec, dims=((1,), (0,))):
    if prec == 6:
        return lax.dot_general(a, b, (dims, ((), ())), precision=HIGHEST,
                               preferred_element_type=F32)
    if prec == 1:
        return _dg(a.astype(BF16), b.astype(BF16), dims)
    ah, al = _split(a)
    bh, bl = _split(b)
    return _dg(ah, bh, dims) + (_dg(ah, bl, dims) + _dg(al, bh, dims))


_NT = ((1,), (1,))
_TN = ((0,), (0,))


def _mm_pieces(a, b, pieces, split="a", dims=((1,), (0,))):
    exact = (b if split == "a" else a).astype(BF16)
    out, rest = None, (a if split == "a" else b)
    for _ in range(pieces):
        part = rest.astype(BF16)
        rest = rest - part.astype(F32)
        term = _dg(part, exact, dims) if split == "a" else _dg(exact, part, dims)
        out = term if out is None else out + term
    return out


def _silu(x):
    return x * jax.nn.sigmoid(x)


def _mod_kernel(cv_ref, w_ref, b_ref, o_ref):
    sv = _silu(cv_ref[...])
    o_ref[0] = _mm(sv, w_ref[0], 3) + b_ref[0]


def _modulation(cv, w_mod, b_mod):
    depth, d, d3 = w_mod.shape
    nblk = d3 // d
    return pl.pallas_call(
        _mod_kernel,
        grid=(depth, nblk),
        in_specs=[pl.BlockSpec((SUBLANES, d), lambda l, n: (0, 0)),
                  pl.BlockSpec((1, d, d), lambda l, n: (l, 0, n)),
                  pl.BlockSpec((1, 1, d), lambda l, n: (l, 0, n))],
        out_specs=pl.BlockSpec((1, SUBLANES, d), lambda l, n: (l, 0, n)),
        out_shape=jax.ShapeDtypeStruct((depth, SUBLANES, d3), F32),
        compiler_params=_cparams("arbitrary", "arbitrary"),
        name="modulation",
    )(cv, w_mod, b_mod.reshape(depth, 1, d3))


def _layer_norm(x):
    mu = jnp.mean(x, axis=-1, keepdims=True)
    xc = x - mu
    var = jnp.mean(xc * xc, axis=-1, keepdims=True)
    return xc * lax.rsqrt(var + NORM_EPS)


def _mod_row(mod_ref, is_ctx):
    mod = mod_ref[...]
    return jnp.where(is_ctx, mod[1:2], mod[0:1])


def _column_ranges(widths):
    out, off = {}, 0
    for name, w in widths:
        out[name] = (off, off + w)
        off += w
    return out


_IN_COLS = _column_ranges((("qkva", A_WIDTH + 2 * A_KV_WIDTH), ("ga", A_WIDTH),
                           ("qkvb", 3 * B_WIDTH), ("gb", B_WIDTH), ("bd", LANES),
                           ("uc", C_WIDTH), ("gc", C_WIDTH)))


def _row_source_specs(n_ctx_tiles, lat_off, d, first_tile=0):
    tm = ROW_TILE
    return [pl.BlockSpec((tm, d), lambda i: (jnp.minimum(i + first_tile, n_ctx_tiles - 1), 0)),
            pl.BlockSpec((tm, d),
                         lambda i: (jnp.maximum(i + first_tile - n_ctx_tiles, 0) + lat_off, 0))]


def _attn_prep_tile(q, k, v, cos, sa, sb, gq, gk, bd, qo_ref, ko_ref, vo_ref):
    def norm_rope(y, gain):
        ms = _mm_pieces(y * y, bd, 2)
        yn = y * lax.rsqrt(ms + NORM_EPS) * gain
        return (yn * cos + pltpu.roll(yn, LANES - HEAD_DIM // 4, 1) * sa
                + pltpu.roll(yn, HEAD_DIM // 4, 1) * sb)

    lane = lax.broadcasted_iota(jnp.int32, (1, LANES), 1)
    lower = lane < HEAD_DIM
    for j in range(A_Q_HEADS // 2):
        y = norm_rope(q[:, j * LANES:(j + 1) * LANES], gq) * (HEAD_DIM ** -0.5 * LOG2E)
        y_sw = pltpu.roll(y, HEAD_DIM, 1)
        for half in range(2):
            h = 2 * j + half
            src = y if (h // A_GROUPS) == half else y_sw
            keep = lower if (h // A_GROUPS) == 0 else jnp.logical_not(lower)
            qo_ref[h] = jnp.where(keep, src, 0.0).astype(BF16)
    ko_ref[...] = norm_rope(k, gk).astype(BF16)
    vt = v.T
    ones = jnp.ones((HEAD_DIM, vt.shape[1]), F32)
    for h in range(A_KV_HEADS):
        vo_ref[h] = jnp.concatenate([vt[h * HEAD_DIM:(h + 1) * HEAD_DIM], ones],
                                    axis=0).astype(BF16)


def _dn_prep_tile(z, prev_row, next_row, r, w, par, bd, q_o, k_o, v_o, bg_o):
    tm = z.shape[0]
    row = lax.broadcasted_iota(jnp.int32, (tm, 1), 0)
    z_prev = jnp.where(row == 0, prev_row, pltpu.roll(z, 1, 0))
    z_next = jnp.where(row == tm - 1, next_row, pltpu.roll(z, tm - 1, 0))
    y = _silu(z_prev * w[0:1] + z * w[1:2] + z_next * w[2:3])

    def l2norm(t):
        return t * lax.rsqrt(_mm_pieces(t * t, bd, 2) + NORM_EPS)

    q_o[...] = l2norm(y[:, :B_WIDTH]) * (HEAD_DIM ** -0.5)
    k_o[...] = l2norm(y[:, B_WIDTH:2 * B_WIDTH])
    v_o[...] = y[:, 2 * B_WIDTH:]
    xa = r + par[1:2]
    softplus = jnp.maximum(xa, 0.0) + jnp.log(1.0 + jnp.exp(-jnp.abs(xa)))
    g = -jnp.exp(par[0:1]) * softplus
    lane = lax.broadcasted_iota(jnp.int32, (1, LANES), 1)
    bg_o[...] = jnp.where(lane < 2 * B_HEADS, jax.nn.sigmoid(r), g)


def _inproj_kernel(c_ref, l_ref, cp_ref, cn_ref, lp_ref, ln_ref, mod_ref, w_ref, cos_ref, sa_ref,
                   sb_ref, gq_ref, gk_ref, bd_ref, cw_ref, par_ref, bdb_ref,
                   qo_ref, ko_ref, vo_ref, ga_ref, qn_ref, kn_ref, vn_ref, bg_ref, gb_ref, uc_ref,
                   gc_ref, *, n_ctx_tiles, n_tiles, d_model):
    i = pl.program_id(0)
    is_ctx = i < n_ctx_tiles
    row = _mod_row(mod_ref, is_ctx)
    shift, scale = row[:, :d_model], row[:, d_model:2 * d_model]
    modulate = lambda t: (_layer_norm(t) * (1.0 + scale) + shift).astype(BF16)
    hb = modulate(jnp.where(is_ctx, c_ref[...], l_ref[...]))
    proj = lambda lhs, name: jnp.dot(lhs, w_ref[:, _IN_COLS[name][0]:_IN_COLS[name][1]],
                                     preferred_element_type=F32)
    za = proj(hb, "qkva")
    cos = jnp.where(is_ctx, 1.0, cos_ref[...])
    sa, sb = jnp.where(is_ctx, 0.0, sa_ref[...]), jnp.where(is_ctx, 0.0, sb_ref[...])
    _attn_prep_tile(za[:, :A_WIDTH], za[:, A_WIDTH:A_WIDTH + A_KV_WIDTH],
                    za[:, A_WIDTH + A_KV_WIDTH:], cos, sa, sb, gq_ref[...], gk_ref[...],
                    bd_ref[...], qo_ref, ko_ref, vo_ref)
    ga_ref[...] = proj(hb, "ga")
    halo = modulate(jnp.where(is_ctx, jnp.concatenate([cp_ref[...], cn_ref[...]], axis=0),
                              jnp.concatenate([lp_ref[...], ln_ref[...]], axis=0)))
    zh = proj(halo, "qkvb")
    seg_start = jnp.logical_or(i == 0, i == n_ctx_tiles)
    seg_end = jnp.logical_or(i == n_ctx_tiles - 1, i == n_tiles - 1)
    _dn_prep_tile(proj(hb, "qkvb"),
                  jnp.where(seg_start, 0.0, zh[SUBLANES - 1:SUBLANES]),
                  jnp.where(seg_end, 0.0, zh[SUBLANES:SUBLANES + 1]),
                  proj(hb, "bd"), cw_ref[...], par_ref[...], bdb_ref[...],
                  qn_ref, kn_ref, vn_ref, bg_ref)
    gb_ref[...] = proj(hb, "gb")
    uc_ref[...] = proj(hb, "uc")
    gc_ref[...] = proj(hb, "gc")


def _pad_w_in(w):
    d = w.shape[0]
    n_main = 2 * A_WIDTH + 2 * A_KV_WIDTH + 4 * B_WIDTH
    n_small = 4 * B_HEADS
    return jnp.concatenate(
        [w[:, :n_main], w[:, n_main:n_main + n_small], jnp.zeros((d, LANES - n_small), w.dtype),
         w[:, n_main + n_small:]], axis=1).astype(BF16)


def _block_diag_const(width, value):
    r = jnp.arange(width) // HEAD_DIM
    return jnp.where(r[:, None] == r[None, :], value, 0.0).astype(F32)


def _halo_specs(n_ctx_tiles, lat_off, d, n_ctx_rows, n_lat_rows):
    r8 = ROW_TILE // SUBLANES
    c_last, l_last = n_ctx_rows // SUBLANES - 1, n_lat_rows // SUBLANES - 1
    lo8 = lat_off * r8
    blk = lambda f: pl.BlockSpec((SUBLANES, d), f)
    return [blk(lambda i: (jnp.clip(i * r8 - 1, 0, c_last), 0)),
            blk(lambda i: (jnp.clip((i + 1) * r8, 0, c_last), 0)),
            blk(lambda i: (jnp.clip((i - n_ctx_tiles) * r8 - 1, 0, l_last) + lo8, 0)),
            blk(lambda i: (jnp.clip((i - n_ctx_tiles + 1) * r8, 0, l_last) + lo8, 0))]


def _inproj(ctx_src, lat_src, lat_off, n, n_ctx, mod_l, w_pad, rope, gq, gk, conv_w, a_log,
            dt_bias):
    d = lat_src.shape[1]
    tm = ROW_TILE
    nct = n_ctx // tm
    bd = _block_diag_const(LANES, 1.0 / HEAD_DIM)
    bdb = _block_diag_const(B_WIDTH, 1.0)
    par = jnp.zeros((2, LANES), F32)
    par = par.at[0, 2 * B_HEADS:4 * B_HEADS].set(a_log.reshape(-1))
    par = par.at[1, 2 * B_HEADS:4 * B_HEADS].set(dt_bias.reshape(-1))
    row = lambda w: pl.BlockSpec((tm, w), lambda i: (i, 0))
    lat_row = lambda w: pl.BlockSpec((tm, w), lambda i: (jnp.maximum(i - nct, 0), 0))
    full = lambda a: pl.BlockSpec(a.shape, lambda i: (0,) * a.ndim)
    outs = [("ga", A_WIDTH), ("qn", B_WIDTH), ("kn", B_WIDTH), ("vn", B_WIDTH), ("bg", LANES),
            ("gb", B_WIDTH), ("uc", C_WIDTH), ("gc", C_WIDTH)]
    return pl.pallas_call(
        functools.partial(_inproj_kernel, n_ctx_tiles=nct, n_tiles=n // tm, d_model=d),
        grid=(n // tm,),
        in_specs=_row_source_specs(nct, lat_off, d)
        + _halo_specs(nct, lat_off, d, n_ctx, n - n_ctx)
        + [full(mod_l), full(w_pad), lat_row(LANES), lat_row(LANES), lat_row(LANES), full(gq),
           full(gk), full(bd), full(conv_w), full(par), full(bdb)],
        out_specs=[pl.BlockSpec((A_Q_HEADS, tm, LANES), lambda i: (0, i, 0)),
                   row(LANES),
                   pl.BlockSpec((A_KV_HEADS, LANES, tm), lambda i: (0, 0, i))]
        + [row(w) for _, w in outs],
        out_shape=[jax.ShapeDtypeStruct((A_Q_HEADS, n, LANES), BF16),
                   jax.ShapeDtypeStruct((n, LANES), BF16),
                   jax.ShapeDtypeStruct((A_KV_HEADS, LANES, n), BF16)]
        + [jax.ShapeDtypeStruct((n, w), F32) for _, w in outs],
        compiler_params=_cparams("arbitrary"),
        name="inproj",
    )(ctx_src, lat_src, ctx_src, ctx_src, lat_src, lat_src, mod_l, w_pad, *rope, gq, gk, bd,
      conv_w, par, bdb)


def _flash_kernel(q_ref, k_ref, vt_ref, g_ref, o_ref, m_sc, mblk_sc, alpha_sc, acc_sc, s_sc, p_sc,
                  *, n_ctx, n_ctx_tiles, tc, n_chunks):
    is_ctx = pl.program_id(0) < n_ctx_tiles
    tq = q_ref.shape[1]
    m_sc[...] = jnp.full(m_sc.shape, -jnp.inf, F32)
    acc_sc[...] = jnp.zeros(acc_sc.shape, F32)

    def q_group(g):
        return q_ref[g * A_GROUPS:(g + 1) * A_GROUPS].reshape(A_GROUPS * tq, LANES)

    def scores(g, c0, size):
        s = _dg(k_ref[pl.ds(c0, size), :], q_group(g), _NT)
        s_sc[g, :size, :] = s
        mblk_sc[g] = jnp.max(s, axis=0, keepdims=True)

    def exponentials(g, size):
        m_old = m_sc[g]
        m_new = jnp.maximum(m_old, mblk_sc[g])
        alpha_sc[g] = jnp.exp2(m_old - m_new)
        p_sc[g, :size, :] = jnp.exp2(s_sc[g, :size, :] - m_new).astype(BF16)
        m_sc[g] = m_new

    def weighted_values(g, c0, size):
        acc_sc[g] = alpha_sc[g] * acc_sc[g] + jnp.dot(
            vt_ref[g, :, pl.ds(c0, size)], p_sc[g, :size, :], preferred_element_type=F32)

    @pl.when(is_ctx)
    def _():
        for g in range(A_KV_HEADS):
            scores(g, 0, n_ctx)
            exponentials(g, n_ctx)
            weighted_values(g, 0, n_ctx)

    @pl.when(jnp.logical_not(is_ctx))
    def _():
        first, second = 0, A_KV_HEADS - 1
        p_sc[second] = jnp.zeros(p_sc.shape[1:], BF16)
        alpha_sc[second] = jnp.ones(alpha_sc.shape[1:], F32)
        scores(first, 0, tc)

        def body(c, carry):
            c0 = pl.multiple_of(c * tc, tc)
            c_prev = pl.multiple_of(jnp.maximum(c - 1, 0) * tc, tc)
            c_next = pl.multiple_of(jnp.minimum(c + 1, n_chunks - 1) * tc, tc)
            scores(second, c0, tc)
            exponentials(first, tc)
            weighted_values(second, c_prev, tc)
            scores(first, c_next, tc)
            exponentials(second, tc)
            weighted_values(first, c0, tc)
            return carry

        lax.fori_loop(0, n_chunks, body, 0)
        weighted_values(second, (n_chunks - 1) * tc, tc)

    _flash_finalize(acc_sc, g_ref, o_ref, tq)


def _flash_finalize(acc_sc, g_ref, o_ref, tq):
    heads = []
    for h in range(A_Q_HEADS):
        g, r = divmod(h, A_GROUPS)
        acc = acc_sc[g, :, r * tq:(r + 1) * tq]
        heads.append((acc[:HEAD_DIM] / acc[HEAD_DIM:HEAD_DIM + 1]).T)
    o_ref[...] = (jnp.concatenate(heads, axis=1) * _silu(g_ref[...])).astype(o_ref.dtype)


def _flash_bounded_kernel(u_ref, q_ref, k_ref, vt_ref, g_ref, o_ref, acc_sc, p_sc, *, n_ctx,
                          n_ctx_tiles, tc, n_chunks):
    is_ctx = pl.program_id(0) < n_ctx_tiles
    tq = q_ref.shape[1]
    acc_sc[...] = jnp.zeros(acc_sc.shape, F32)
    bound = u_ref[0]

    def q_group(g):
        return q_ref[g * A_GROUPS:(g + 1) * A_GROUPS].reshape(A_GROUPS * tq, LANES)

    def probabilities(g, c0, size):
        s = _dg(k_ref[pl.ds(c0, size), :], q_group(g), _NT)
        p_sc[g, :size, :] = jnp.exp2(s - bound).astype(BF16)

    def weighted_values(g, c0, size):
        acc_sc[g] += jnp.dot(vt_ref[g, :, pl.ds(c0, size)], p_sc[g, :size, :],
                             preferred_element_type=F32)

    @pl.when(is_ctx)
    def _():
        for g in range(A_KV_HEADS):
            probabilities(g, 0, n_ctx)
            weighted_values(g, 0, n_ctx)

    @pl.when(jnp.logical_not(is_ctx))
    def _():
        first, second = 0, A_KV_HEADS - 1
        probabilities(first, 0, tc)

        def body(c, carry):
            c0 = pl.multiple_of(c * tc, tc)
            probabilities(second, c0, tc)
            weighted_values(first, c0, tc)
            probabilities(first, pl.multiple_of(c0 + tc, tc), tc)
            weighted_values(second, c0, tc)
            return carry

        lax.fori_loop(0, n_chunks - 1, body, 0)
        last = (n_chunks - 1) * tc
        probabilities(second, last, tc)
        weighted_values(first, last, tc)
        weighted_values(second, last, tc)

    _flash_finalize(acc_sc, g_ref, o_ref, tq)


def _key_chunk(n):
    for tc in (1280, 1024, 640, 512, 256, 128):
        if n % tc == 0:
            return tc
    raise ValueError(f"no key chunk for {n} rows")


_SCORE_BOUND_PER_GAIN = 1.01 * HEAD_DIM * HEAD_DIM ** -0.5 * LOG2E
_MAX_FIXED_REFERENCE = 40.0


def _flash(qz, k, vt, ga, n_ctx, gain_product):
    n = k.shape[0]
    tq = ROW_TILE
    tc = _key_chunk(n)
    gq = A_GROUPS * tq
    assert A_KV_HEADS == 2 and n_ctx <= tc
    full = lambda a: pl.BlockSpec(a.shape, lambda i: (0,) * a.ndim,
                                  pipeline_mode=pl.Buffered(1))
    statics = dict(n_ctx=n_ctx, n_ctx_tiles=n_ctx // tq, tc=tc, n_chunks=n // tc)
    common = dict(
        grid=(n // tq,),
        out_specs=pl.BlockSpec((tq, A_WIDTH), lambda i: (i, 0)),
        out_shape=jax.ShapeDtypeStruct((n, A_WIDTH), BF16),
        compiler_params=_cparams("arbitrary"))
    in_specs = [pl.BlockSpec((A_Q_HEADS, tq, LANES), lambda i: (0, i, 0)), full(k), full(vt),
                pl.BlockSpec((tq, A_WIDTH), lambda i: (i, 0))]
    acc_shape = pltpu.VMEM((A_KV_HEADS, LANES, gq), F32)
    p_shape = pltpu.VMEM((A_KV_HEADS, tc, gq), BF16)

    def online(bound):
        del bound
        return pl.pallas_call(
            functools.partial(_flash_kernel, **statics), in_specs=in_specs,
            scratch_shapes=[pltpu.VMEM((A_KV_HEADS, 1, gq), F32),
                            pltpu.VMEM((A_KV_HEADS, 1, gq), F32),
                            pltpu.VMEM((A_KV_HEADS, 1, gq), F32),
                            acc_shape,
                            pltpu.VMEM((A_KV_HEADS, tc, gq), F32),
                            p_shape],
            name="flash_attn_online", **common)(qz, k, vt, ga)

    def bounded(bound):
        return pl.pallas_call(
            functools.partial(_flash_bounded_kernel, **statics),
            in_specs=[pl.BlockSpec(memory_space=pltpu.SMEM)] + in_specs,
            scratch_shapes=[acc_shape, p_shape],
            name="flash_attn_bounded", **common)(bound, qz, k, vt, ga)

    bound = (_SCORE_BOUND_PER_GAIN * gain_product).astype(F32).reshape(1)
    return lax.cond(bound[0] <= _MAX_FIXED_REFERENCE, bounded, online, bound)


def _dn_scan_kernel(qf, kf, vf, bgf, qb, kb, vb, bgb, of_ref, ob_ref, s_ref, *, prec, cps):
    @pl.when(pl.program_id(0) == 0)
    def _():
        s_ref[...] = jnp.zeros(s_ref.shape, F32)

    c, hc = B_CHUNK, B_HEADS * B_CHUNK
    row = lax.broadcasted_iota(jnp.int32, (hc, hc), 0)
    col = lax.broadcasted_iota(jnp.int32, (hc, hc), 1)
    same_head = (row // c) == (col // c)
    eye = (row == col).astype(F32)
    level_masks = [(row // 2) == (col // 2)]
    bs = 2
    while bs < c:
        level_masks.append(jnp.logical_and((row // (2 * bs)) == (col // (2 * bs)),
                                           (row // bs) != (col // bs)))
        bs *= 2
    rc, cc = lax.broadcasted_iota(jnp.int32, (c, c), 0), lax.broadcasted_iota(jnp.int32, (c, c), 1)

    def heads_bd(x):
        return jnp.where(same_head, jnp.concatenate([x] * B_HEADS, axis=0), 0.0)

    def rows_of(slab, lane0):
        return jnp.concatenate(
            [jnp.broadcast_to(slab[:, lane0 + h:lane0 + h + 1], (c, hc)) for h in range(B_HEADS)],
            axis=0)

    def chunk_setup(d, q_ref, k_ref, v_ref, bg_ref, r0):
        rows = slice(r0, r0 + c)
        incl = (col <= row) if d == 0 else (col >= row)
        strict = (col < row) if d == 0 else (col > row)
        incl_c = (cc <= rc) if d == 0 else (cc >= rc)
        last = c - 1 if d == 0 else 0
        lb = d * B_HEADS
        lg = 2 * B_HEADS + lb
        bg = bg_ref[rows, :]
        gcum = _mm_pieces(incl_c.astype(F32), bg, 3, split="b")
        gcum_t = gcum.T
        g_row = jnp.concatenate([gcum_t[lg + h:lg + h + 1, :] for h in range(B_HEADS)], axis=1)
        g_last = gcum[last:last + 1, :]
        decay = jnp.where(jnp.logical_and(same_head, incl),
                          jnp.exp(rows_of(gcum, lg) - g_row), 0.0)
        beta = rows_of(bg, lb)
        eg = rows_of(jnp.exp(gcum), lg)
        to_last = rows_of(jnp.exp(g_last - gcum), lg)
        q_bd = heads_bd(q_ref[rows, :])
        k_bd = heads_bd(k_ref[rows, :])
        v_bd = heads_bd(v_ref[rows, :])
        k_b = k_bd.astype(BF16)
        kk = _mm(k_b, k_b, prec, _NT)
        qk = _mm(q_bd, k_b, prec, _NT)
        a_mat = jnp.where(jnp.logical_and(same_head, strict), kk * beta * decay, 0.0)
        g_state = jnp.concatenate(
            [jnp.broadcast_to(jnp.exp(g_last[:, lg + h:lg + h + 1]), (c, hc))
             for h in range(B_HEADS)], axis=0)
        return dict(a_mat=a_mat.astype(BF16),
                    t=(eye - jnp.where(level_masks[0], a_mat, 0.0)).astype(BF16),
                    v_beta=(v_bd * beta).astype(BF16), k_beta=(k_bd * (beta * eg)).astype(BF16),
                    q_dec=(q_bd * eg).astype(BF16), attn=(qk * decay).astype(BF16),
                    k_dec=(k_bd * to_last).astype(BF16), g_state=g_state)

    dirs = ((qf, kf, vf, bgf, of_ref), (qb, kb, vb, bgb, ob_ref))
    chains = [chunk_setup(d, *dirs[d][:4], j * c) for j in range(cps) for d in range(2)]
    zero_b = jnp.zeros((), BF16)
    for lm in level_masks[1:]:
        part = [_mm(ch["t"], jnp.where(lm, ch["a_mat"], zero_b), prec) for ch in chains]
        for ch, pt in zip(chains, part):
            ch["t"] = ch["t"] - _mm(pt, ch["t"], prec).astype(BF16)
    for ch in chains:
        ch["w"] = _mm(ch["t"], ch["v_beta"], prec)
        ch["k_cum"] = _mm(ch["t"], ch["k_beta"], prec)

    states = [s_ref[d] for d in range(2)]
    for step in range(cps):
        sub = [step, cps - 1 - step]
        tms = [chains[sub[d] * 2 + d] for d in range(2)]
        us = [tms[d]["w"] - _mm(tms[d]["k_cum"], states[d], prec) for d in range(2)]
        for d in range(2):
            tm, u, s = tms[d], us[d], states[d]
            o_bd = _mm(tm["q_dec"], s, prec) + _mm(tm["attn"], u, prec)
            states[d] = s * tm["g_state"] + _mm(tm["k_dec"], u, prec, _TN)
            dirs[d][4][sub[d] * c:(sub[d] + 1) * c, :] = functools.reduce(
                jnp.add, [o_bd[h * c:(h + 1) * c] for h in range(B_HEADS)])
    for d in range(2):
        s_ref[d] = states[d]


def _scan_order_maps(n_chunks, n_ctx_chunks):
    fwd = lambda j: (j, 0)
    bwd = lambda j: (jnp.where(j < n_ctx_chunks, n_ctx_chunks - 1 - j,
                               n_chunks - 1 - (j - n_ctx_chunks)), 0)
    return fwd, bwd


def _dn_scan(qn, kn, vn, bg, n_ctx):
    n = qn.shape[0]
    assert B_CHUNK == HEAD_DIM
    c = B_CHUNK * DN_CHUNKS_PER_STEP
    fwd, bwd = _scan_order_maps(n // c, n_ctx // c)
    specs = lambda m: [pl.BlockSpec((c, B_WIDTH), m)] * 3 + [pl.BlockSpec((c, LANES), m)]
    return pl.pallas_call(
        functools.partial(_dn_scan_kernel, prec=DN_PREC, cps=DN_CHUNKS_PER_STEP),
        grid=(n // c,),
        in_specs=specs(fwd) + specs(bwd),
        out_specs=[pl.BlockSpec((c, B_WIDTH), fwd), pl.BlockSpec((c, B_WIDTH), bwd)],
        out_shape=[jax.ShapeDtypeStruct((n, B_WIDTH), F32)] * 2,
        scratch_shapes=[pltpu.VMEM((2, B_WIDTH, B_WIDTH), F32)],
        compiler_params=_cparams("arbitrary"),
        name="dn_scan",
    )(qn, kn, vn, bg, qn, kn, vn, bg)


def _s5_kernel(uf_ref, ub_ref, bre_ref, bim_ref, cre_ref, cimn_ref, a_ref, ap_ref,
               yf_ref, yb_ref, hre_sc, him_sc, carry_sc, *, prec):
    @pl.when(pl.program_id(0) == 0)
    def _():
        carry_sc[...] = jnp.zeros(carry_sc.shape, F32)

    steps, nsub = S5_STEPS, SUBLANES
    ts = steps * nsub
    prow = lax.broadcasted_iota(jnp.int32, (ts, ts), 0)
    pcol = lax.broadcasted_iota(jnp.int32, (ts, ts), 1)
    scan_pos = (prow % nsub) * steps + prow // nsub
    perms = [jnp.where(pcol == scan_pos, 1.0, 0.0).astype(BF16),
             jnp.where(pcol == ts - 1 - scan_pos, 1.0, 0.0).astype(BF16)]
    ndir = 2

    def permute(pm, x, dims, pieces):
        return _mm_pieces(pm, x, pieces, split="b", dims=dims)

    u_perms = [permute(perms[d], u_ref[...], ((1,), (0,)), 1 if prec == 1 else 2)
               for d, u_ref in enumerate((uf_ref, ub_ref))]
    for d in range(ndir):
        hre_sc[d] = _mm(u_perms[d], bre_ref[d], prec)
    for d in range(ndir):
        him_sc[d] = _mm(u_perms[d], bim_ref[d], prec)

    def rows_at(jj):
        return pl.ds(pl.multiple_of(jj * nsub, nsub), nsub)

    def local_scan(jj, carry):
        rows = rows_at(jj)
        out = []
        for d in range(ndir):
            a_re, a_im = a_ref[d, 0], a_ref[d, 1]
            h_re, h_im = carry[d]
            n_re = a_re * h_re - a_im * h_im + hre_sc[d, rows, :]
            n_im = a_re * h_im + a_im * h_re + him_sc[d, rows, :]
            hre_sc[d, rows, :] = n_re
            him_sc[d, rows, :] = n_im
            out.append((n_re, n_im))
        return tuple(out)

    zero = jnp.zeros((nsub, C_STATES), F32)
    ends = lax.fori_loop(0, steps, local_scan, ((zero, zero),) * ndir)
    cins = []
    for d in range(ndir):
        as_re = ap_ref[d, 0, steps - 1, 0:1, :]
        as_im = ap_ref[d, 1, steps - 1, 0:1, :]
        c_re, c_im = carry_sc[2 * d:2 * d + 1, :], carry_sc[2 * d + 1:2 * d + 2, :]
        cin = []
        for s in range(nsub):
            cin.append((c_re, c_im))
            c_re, c_im = (ends[d][0][s:s + 1] + as_re * c_re - as_im * c_im,
                          ends[d][1][s:s + 1] + as_re * c_im + as_im * c_re)
        carry_sc[2 * d:2 * d + 1, :] = c_re
        carry_sc[2 * d + 1:2 * d + 2, :] = c_im
        cins.append((jnp.concatenate([x[0] for x in cin], axis=0),
                     jnp.concatenate([x[1] for x in cin], axis=0)))

    def fixup(jj, carry):
        rows = rows_at(jj)
        for d in range(ndir):
            p_re = ap_ref[d, 0, jj]
            p_im = ap_ref[d, 1, jj]
            cin_re, cin_im = cins[d]
            hre_sc[d, rows, :] += p_re * cin_re - p_im * cin_im
            him_sc[d, rows, :] += p_re * cin_im + p_im * cin_re
        return carry

    lax.fori_loop(0, steps, fixup, 0)
    y_re = [_mm(hre_sc[d], cre_ref[d], prec) for d in range(ndir)]
    y_im = [_mm(him_sc[d], cimn_ref[d], prec) for d in range(ndir)]
    for d, y_ref in enumerate((yf_ref, yb_ref)):
        y_ref[...] = permute(perms[d], y_re[d] + y_im[d], _TN, 2)


def _s5_params(a_re, a_im, log_dt, b_re, b_im, c_re, c_im):
    dt = jnp.exp(log_dt.astype(F32))[..., None]
    lam_re, lam_im = a_re.astype(F32), a_im.astype(F32)
    k = jnp.arange(1, S5_STEPS + 1, dtype=F32)[None, :, None, None]
    mag = jnp.exp(lam_re[:, None] * dt[:, None] * k)
    ang = lam_im[:, None] * dt[:, None] * k
    ap = jnp.stack([mag * jnp.cos(ang), mag * jnp.sin(ang)], axis=1)
    ap = ap.reshape(2, 2, S5_STEPS, C_STATES)
    ab_re, ab_im = ap[:, 0, 0], ap[:, 1, 0]
    abar = jnp.stack([ab_re, ab_im], axis=1)
    x, y = ab_re.reshape(lam_re.shape) - 1.0, ab_im.reshape(lam_re.shape)
    den = lam_re * lam_re + lam_im * lam_im
    cf_re, cf_im = (x * lam_re + y * lam_im) / den, (y * lam_re - x * lam_im) / den
    bb_re = cf_re[..., None] * b_re - cf_im[..., None] * b_im
    bb_im = cf_re[..., None] * b_im + cf_im[..., None] * b_re
    eye = jnp.eye(C_GROUPS, dtype=F32)
    dense_b = lambda b: jnp.einsum('dgpc,gh->dgchp', b, eye).reshape(2, C_WIDTH, C_STATES)
    dense_c = lambda c: jnp.einsum('dgcp,gh->dgphc', c, eye).reshape(2, C_STATES, C_WIDTH)
    abar = jnp.broadcast_to(abar[:, :, None, :], (2, 2, SUBLANES, C_STATES))
    ap = jnp.broadcast_to(ap[:, :, :, None, :], (2, 2, S5_STEPS, SUBLANES, C_STATES))
    return (dense_b(bb_re), dense_b(bb_im), dense_c(c_re.astype(F32)),
            dense_c(-c_im.astype(F32)), abar, ap)


def _s5_scan(uc, params, n_ctx):
    n = uc.shape[0]
    ts = ROW_TILE
    bre, bim, cre, cimn, abar, ap = params
    fwd, bwd = _scan_order_maps(n // ts, n_ctx // ts)
    full = lambda a: pl.BlockSpec(a.shape, lambda j: (0,) * a.ndim,
                                  pipeline_mode=pl.Buffered(1))
    return pl.pallas_call(
        functools.partial(_s5_kernel, prec=S5_PREC),
        grid=(n // ts,),
        in_specs=[pl.BlockSpec((ts, C_WIDTH), fwd), pl.BlockSpec((ts, C_WIDTH), bwd),
                  full(bre), full(bim), full(cre), full(cimn), full(abar), full(ap)],
        out_specs=[pl.BlockSpec((ts, C_WIDTH), fwd), pl.BlockSpec((ts, C_WIDTH), bwd)],
        out_shape=[jax.ShapeDtypeStruct((n, C_WIDTH), F32)] * 2,
        scratch_shapes=[pltpu.VMEM((2, ts, C_STATES), F32), pltpu.VMEM((2, ts, C_STATES), F32),
                        pltpu.VMEM((SUBLANES, C_STATES), F32)],
        compiler_params=_cparams("arbitrary"),
        name="s5_scan",
    )(uc, uc, bre, bim, cre, cimn, abar, ap)


def _gelu_tanh(x):
    return 0.5 * x * (1.0 + jnp.tanh(math.sqrt(2.0 / math.pi) * (x + 0.044715 * (x * x * x))))


def _out_kernel(c_ref, l_ref, mod_ref, ao_ref, dof_ref, dob_ref, gb_ref, u_ref, yf_ref, yb_ref,
                gc_ref, vec_ref, bd_ref, gluw_ref, wo_ref, lnv_ref, o_ref, *, n_ctx_tiles,
                first_tile, d_model, alpha):
    is_ctx = pl.program_id(0) + first_tile < n_ctx_tiles
    vec = vec_ref[...]
    o = dof_ref[...] + dob_ref[...]
    ms = _mm_pieces(o * o, bd_ref[...], 2)
    dn = o * lax.rsqrt(ms + NORM_EPS) * vec[0:1] * _silu(gb_ref[...])
    z = _gelu_tanh(u_ref[...] * vec[1:2] + yf_ref[...] + yb_ref[...])
    lin = jnp.dot(z.astype(BF16), gluw_ref[...], preferred_element_type=F32) + vec[2:3]
    s5 = z * jax.nn.sigmoid(lin) * _silu(gc_ref[...])
    wo = wo_ref
    proj = (jnp.dot(ao_ref[...], wo[:A_WIDTH, :], preferred_element_type=F32)
            + jnp.dot(dn.astype(BF16), wo[A_WIDTH:A_WIDTH + B_WIDTH, :],
                      preferred_element_type=F32)
            + jnp.dot(s5.astype(BF16), wo[A_WIDTH + B_WIDTH:, :], preferred_element_type=F32))
    gate = _mod_row(mod_ref, is_ctx)[:, 2 * d_model:]
    lnv = lnv_ref[...]
    x = jnp.where(is_ctx, c_ref[...], l_ref[...])
    o_ref[...] = _layer_norm(alpha * x + gate * proj) * lnv[0:1] + lnv[1:2]


def _out_proj(ctx_src, lat_src, lat_off, n, first_tile, mod_l, ao, dof, dob, gb, uc, yf, yb, gc,
              vec, gluw, wo, lnv, n_ctx_tiles, alpha):
    d = lat_src.shape[1]
    tm = ROW_TILE
    bd = _block_diag_const(B_WIDTH, 1.0 / HEAD_DIM)
    row = lambda w: pl.BlockSpec((tm, w), lambda i: (i + first_tile, 0))
    full = lambda a: pl.BlockSpec(a.shape, lambda i: (0,) * a.ndim)
    return pl.pallas_call(
        functools.partial(_out_kernel, n_ctx_tiles=n_ctx_tiles, first_tile=first_tile, d_model=d,
                          alpha=alpha),
        grid=(n // tm - first_tile,),
        in_specs=_row_source_specs(n_ctx_tiles, lat_off, d, first_tile)
        + [full(mod_l), row(A_WIDTH), row(B_WIDTH), row(B_WIDTH), row(B_WIDTH),
           row(C_WIDTH), row(C_WIDTH), row(C_WIDTH), row(C_WIDTH), full(vec), full(bd),
           full(gluw), full(wo), full(lnv)],
        out_specs=pl.BlockSpec((tm, d), lambda i: (i, 0)),
        out_shape=jax.ShapeDtypeStruct((n - first_tile * tm, d), F32),
        compiler_params=_cparams("arbitrary"),
        name="out_proj",
    )(ctx_src, lat_src, mod_l, ao, dof, dob, gb, uc, yf, yb, gc, vec, bd, gluw, wo, lnv)


def _rope_tables(n_lat):
    rows = n_lat // GRID_W
    axis_dim = HEAD_DIM // 2
    inv = ROPE_THETA ** (-jnp.arange(0, axis_dim, 2, dtype=F32) / axis_dim)
    ar = jnp.arange(rows, dtype=F32)[:, None] * inv
    ac = jnp.arange(GRID_W, dtype=F32)[:, None] * inv
    upper = jnp.arange(axis_dim) >= HEAD_DIM // 4

    def expand(fr, fc):
        r = jnp.broadcast_to(fr[:, None, :], (rows, GRID_W, axis_dim))
        c = jnp.broadcast_to(fc[None, :, :], (rows, GRID_W, axis_dim))
        head = jnp.concatenate([r, c], -1)
        return jnp.concatenate([head] * (LANES // HEAD_DIM), -1).reshape(n_lat, LANES)

    two = lambda t: jnp.tile(t, (1, 2))
    sin_r, sin_c = two(jnp.sin(ar)), two(jnp.sin(ac))
    cos = expand(two(jnp.cos(ar)), two(jnp.cos(ac)))
    sa = expand(jnp.where(upper, 0.0, -sin_r), jnp.where(upper, 0.0, -sin_c))
    sb = expand(jnp.where(upper, sin_r, 0.0), jnp.where(upper, sin_c, 0.0))
    return cos, sa, sb


def kernel(x, c, ctx, c_ctx, w_mod, b_mod, w_in, attn_q_gain, attn_k_gain, dn_conv_w, dn_A_log, dn_dt_bias, dn_out_gain, s5_A_re, s5_A_im, s5_log_dt, s5_B_re, s5_B_im, s5_C_re, s5_C_im, s5_D, glu_w, glu_b, w_out, ln_g, ln_b):
    batch, n_lat, d = x.shape
    n_ctx = ctx.shape[1]
    depth = w_mod.shape[0]
    assert batch == 1 and n_ctx % ROW_TILE == 0 and n_lat % ROW_TILE == 0 and n_lat % GRID_W == 0
    nct = n_ctx // ROW_TILE
    alpha = (2 * depth) ** 0.25

    n = n_ctx + n_lat
    cv = jnp.zeros((SUBLANES, d), F32).at[0].set(c[0]).at[1].set(c_ctx)
    mod = _modulation(cv, w_mod, b_mod)
    cos, sa, sb = _rope_tables(n_lat)
    lane_tile = lambda g: jnp.tile(g.astype(F32)[None, :], (1, LANES // HEAD_DIM))
    s5_par = jax.vmap(_s5_params)(s5_A_re, s5_A_im, s5_log_dt, s5_B_re, s5_B_im, s5_C_re, s5_C_im)

    ctx_src, lat_src, lat_off = ctx[0], x[0], 0
    for l in range(depth):
        (qz, kp, vt, ga, qn, kn, vn, bg, gb, uc, gc) = _inproj(
            ctx_src, lat_src, lat_off, n, n_ctx, mod[l], _pad_w_in(w_in[l]), (cos, sa, sb),
            lane_tile(attn_q_gain[l]), lane_tile(attn_k_gain[l]), dn_conv_w[l].astype(F32),
            dn_A_log[l], dn_dt_bias[l])
        gain_product = jnp.max(jnp.abs(attn_q_gain[l])) * jnp.max(jnp.abs(attn_k_gain[l]))
        ao = _flash(qz, kp, vt, ga, n_ctx, gain_product)
        dof, dob = _dn_scan(qn, kn, vn, bg, n_ctx)
        yf, yb = _s5_scan(uc, tuple(p[l] for p in s5_par), n_ctx)
        vec = jnp.zeros((SUBLANES, C_WIDTH), F32)
        vec = vec.at[0].set(jnp.tile(dn_out_gain[l].astype(F32), B_HEADS))
        vec = vec.at[1].set(s5_D[l].astype(F32)).at[2].set(glu_b[l].astype(F32))
        lnv = jnp.zeros((SUBLANES, d), F32).at[0].set(ln_g[l]).at[1].set(ln_b[l])
        first_tile = nct if l == depth - 1 else 0
        xall = _out_proj(ctx_src, lat_src, lat_off, n, first_tile, mod[l], ao, dof, dob, gb, uc,
                         yf, yb, gc, vec, glu_w[l].astype(BF16), w_out[l].astype(BF16), lnv, nct,
                         alpha)
        ctx_src, lat_src, lat_off = xall, xall, nct
    return xall[None]
```

```python
import functools
import math

import jax
import jax.numpy as jnp
from jax import lax
from jax.experimental import pallas as pl
from jax.experimental.pallas import tpu as pltpu

F32 = jnp.float32
BF16 = jnp.bfloat16
HIGHEST = lax.Precision.HIGHEST

GRID_W = 64
HEAD_DIM = 64
NORM_EPS = 1e-6
ROPE_THETA = 10000.0
A_Q_HEADS = 8
A_KV_HEADS = 2
A_GROUPS = A_Q_HEADS // A_KV_HEADS
A_WIDTH = A_Q_HEADS * HEAD_DIM
A_KV_WIDTH = A_KV_HEADS * HEAD_DIM
B_HEADS = 4
B_WIDTH = B_HEADS * HEAD_DIM
B_CHUNK = 64
C_WIDTH = 256
C_GROUP = 16
C_GROUPS = C_WIDTH // C_GROUP
C_STATE = 64
C_STATES = C_GROUPS * C_STATE
LANES = 128
SUBLANES = 8
ROW_TILE = 256
S5_STEPS = ROW_TILE // SUBLANES
VMEM_LIMIT = 48 * 1024 * 1024
LOG2E = math.log2(math.e)

DN_PREC = 1
S5_PREC = 1
DN_CHUNKS_PER_STEP = 4


def _cparams(*sem):
    return pltpu.CompilerParams(dimension_semantics=sem, vmem_limit_bytes=VMEM_LIMIT)


def _split(a):
    hi = a.astype(BF16)
    lo = (a - hi.astype(F32)).astype(BF16)
    return hi, lo


def _dg(a, b, dims):
    return lax.dot_general(a, b, (dims, ((), ())), preferred_element_type=F32)


def _mm(a, b, prec, dims=((1,), (0,))):
    if prec == 6:
        return lax.dot_general(a, b, (dims, ((), ())), precision=HIGHEST,
                               preferred_element_type=F32)
    if prec == 1:
        return _dg(a.astype(BF16), b.astype(BF16), dims)
    ah, al = _split(a)
    bh, bl = _split(b)
    return _dg(ah, bh, dims) + (_dg(ah, bl, dims) + _dg(al, bh, dims))


_NT = ((1,), (1,))
_TN = ((0,), (0,))


def _mm_pieces(a, b, pieces, split="a", dims=((1,), (0,))):
    exact = (b if split == "a" else a).astype(BF16)
    out, rest = None, (a if split == "a" else b)
    for _ in range(pieces):
        part = rest.astype(BF16)
        rest = rest - part.astype(F32)
        term = _dg(part, exact, dims) if split == "a" else _dg(exact, part, dims)
        out = term if out is None else out + term
    return out


def _silu(x):
    return x * jax.nn.sigmoid(x)


def _mod_kernel(cv_ref, w_ref, b_ref, o_ref):
    sv = _silu(cv_ref[...])
    o_ref[0] = _mm(sv, w_ref[0], 3) + b_ref[0]


def _modulation(cv, w_mod, b_mod):
    depth, d, d3 = w_mod.shape
    nblk = d3 // d
    return pl.pallas_call(
        _mod_kernel,
        grid=(depth, nblk),
        in_specs=[pl.BlockSpec((SUBLANES, d), lambda l, n: (0, 0)),
                  pl.BlockSpec((1, d, d), lambda l, n: (l, 0, n)),
                  pl.BlockSpec((1, 1, d), lambda l, n: (l, 0, n))],
        out_specs=pl.BlockSpec((1, SUBLANES, d), lambda l, n: (l, 0, n)),
        out_shape=jax.ShapeDtypeStruct((depth, SUBLANES, d3), F32),
        compiler_params=_cparams("arbitrary", "arbitrary"),
        name="modulation",
    )(cv, w_mod, b_mod.reshape(depth, 1, d3))


def _layer_norm(x):
    mu = jnp.mean(x, axis=-1, keepdims=True)
    xc = x - mu
    var = jnp.mean(xc * xc, axis=-1, keepdims=True)
    return xc * lax.rsqrt(var + NORM_EPS)


def _mod_row(mod_ref, is_ctx):
    mod = mod_ref[...]
    return jnp.where(is_ctx, mod[1:2], mod[0:1])


def _column_ranges(widths):
    out, off = {}, 0
    for name, w in widths:
        out[name] = (off, off + w)
        off += w
    return out


_IN_COLS = _column_ranges((("qkva", A_WIDTH + 2 * A_KV_WIDTH), ("ga", A_WIDTH),
                           ("qkvb", 3 * B_WIDTH), ("gb", B_WIDTH), ("bd", LANES),
                           ("uc", C_WIDTH), ("gc", C_WIDTH)))


def _row_source_specs(n_ctx_tiles, lat_off, d, first_tile=0):
    tm = ROW_TILE
    return [pl.BlockSpec((tm, d), lambda i: (jnp.minimum(i + first_tile, n_ctx_tiles - 1), 0)),
            pl.BlockSpec((tm, d),
                         lambda i: (jnp.maximum(i + first_tile - n_ctx_tiles, 0) + lat_off, 0))]


def _attn_prep_tile(q, k, v, cos, sa, sb, gq, gk, bd, qo_ref, ko_ref, vo_ref):
    def norm_rope(y, gain):
        ms = _mm_pieces(y * y, bd, 2)
        yn = y * lax.rsqrt(ms + NORM_EPS) * gain
        return (yn * cos + pltpu.roll(yn, LANES - HEAD_DIM // 4, 1) * sa
                + pltpu.roll(yn, HEAD_DIM // 4, 1) * sb)

    lane = lax.broadcasted_iota(jnp.int32, (1, LANES), 1)
    lower = lane < HEAD_DIM
    for j in range(A_Q_HEADS // 2):
        y = norm_rope(q[:, j * LANES:(j + 1) * LANES], gq) * (HEAD_DIM ** -0.5 * LOG2E)
        y_sw = pltpu.roll(y, HEAD_DIM, 1)
        for half in range(2):
            h = 2 * j + half
            src = y if (h // A_GROUPS) == half else y_sw
            keep = lower if (h // A_GROUPS) == 0 else jnp.logical_not(lower)
            qo_ref[h] = jnp.where(keep, src, 0.0).astype(BF16)
        yield
    ko_ref[...] = norm_rope(k, gk).astype(BF16)
    yield
    vt = v.T
    ones = jnp.ones((HEAD_DIM, vt.shape[1]), F32)
    for h in range(A_KV_HEADS):
        vo_ref[h] = jnp.concatenate([vt[h * HEAD_DIM:(h + 1) * HEAD_DIM], ones],
                                    axis=0).astype(BF16)


def _dn_prep_tile(z, prev_row, next_row, r, w, par, bd, q_o, k_o, v_o, bg_o):
    tm = z.shape[0]
    row = lax.broadcasted_iota(jnp.int32, (tm, 1), 0)

    def conv(cols):
        zc, wc = z[:, cols], w[:, cols]
        z_prev = jnp.where(row == 0, prev_row[:, cols], pltpu.roll(zc, 1, 0))
        z_next = jnp.where(row == tm - 1, next_row[:, cols], pltpu.roll(zc, tm - 1, 0))
        return _silu(z_prev * wc[0:1] + zc * wc[1:2] + z_next * wc[2:3])

    def l2norm(t):
        return t * lax.rsqrt(_mm_pieces(t * t, bd, 2) + NORM_EPS)

    q_o[...] = l2norm(conv(slice(0, B_WIDTH))) * (HEAD_DIM ** -0.5)
    yield
    k_o[...] = l2norm(conv(slice(B_WIDTH, 2 * B_WIDTH)))
    yield
    v_o[...] = conv(slice(2 * B_WIDTH, 3 * B_WIDTH))
    xa = r + par[1:2]
    softplus = jnp.maximum(xa, 0.0) + jnp.log(1.0 + jnp.exp(-jnp.abs(xa)))
    g = -jnp.exp(par[0:1]) * softplus
    lane = lax.broadcasted_iota(jnp.int32, (1, LANES), 1)
    bg_o[...] = jnp.where(lane < 2 * B_HEADS, jax.nn.sigmoid(r), g)


def _inproj_kernel(c_ref, l_ref, cp_ref, cn_ref, lp_ref, ln_ref, mod_ref, w_ref, cos_ref, sa_ref,
                   sb_ref, gq_ref, gk_ref, bd_ref, cw_ref, par_ref, bdb_ref,
                   qo_ref, ko_ref, vo_ref, ga_ref, qn_ref, kn_ref, vn_ref, bg_ref, gb_ref, uc_ref,
                   gc_ref, *, n_ctx_tiles, n_tiles, d_model):
    i = pl.program_id(0)
    is_ctx = i < n_ctx_tiles
    row = _mod_row(mod_ref, is_ctx)
    shift, scale = row[:, :d_model], row[:, d_model:2 * d_model]
    modulate = lambda t: (_layer_norm(t) * (1.0 + scale) + shift).astype(BF16)
    hb = modulate(jnp.where(is_ctx, c_ref[...], l_ref[...]))
    proj = lambda lhs, name: jnp.dot(lhs, w_ref[:, _IN_COLS[name][0]:_IN_COLS[name][1]],
                                     preferred_element_type=F32)
    za = proj(hb, "qkva")
    cos = jnp.where(is_ctx, 1.0, cos_ref[...])
    sa, sb = jnp.where(is_ctx, 0.0, sa_ref[...]), jnp.where(is_ctx, 0.0, sb_ref[...])
    attn_prep = _attn_prep_tile(za[:, :A_WIDTH], za[:, A_WIDTH:A_WIDTH + A_KV_WIDTH],
                                za[:, A_WIDTH + A_KV_WIDTH:], cos, sa, sb, gq_ref[...],
                                gk_ref[...], bd_ref[...], qo_ref, ko_ref, vo_ref)
    zb = proj(hb, "qkvb")
    next(attn_prep, None)
    halo = modulate(jnp.where(is_ctx, jnp.concatenate([cp_ref[...], cn_ref[...]], axis=0),
                              jnp.concatenate([lp_ref[...], ln_ref[...]], axis=0)))
    zh = proj(halo, "qkvb")
    seg_start = jnp.logical_or(i == 0, i == n_ctx_tiles)
    seg_end = jnp.logical_or(i == n_ctx_tiles - 1, i == n_tiles - 1)
    next(attn_prep, None)
    dn_prep = _dn_prep_tile(zb, jnp.where(seg_start, 0.0, zh[SUBLANES - 1:SUBLANES]),
                            jnp.where(seg_end, 0.0, zh[SUBLANES:SUBLANES + 1]),
                            proj(hb, "bd"), cw_ref[...], par_ref[...], bdb_ref[...],
                            qn_ref, kn_ref, vn_ref, bg_ref)
    for name, ref in (("ga", ga_ref), ("gb", gb_ref), ("uc", uc_ref), ("gc", gc_ref)):
        next(attn_prep, None)
        ref[...] = proj(hb, name)
        next(dn_prep, None)
    for rest in (attn_prep, dn_prep):
        for _ in rest:
            pass


def _pad_w_in(w):
    d = w.shape[0]
    n_main = 2 * A_WIDTH + 2 * A_KV_WIDTH + 4 * B_WIDTH
    n_small = 4 * B_HEADS
    return jnp.concatenate(
        [w[:, :n_main], w[:, n_main:n_main + n_small], jnp.zeros((d, LANES - n_small), w.dtype),
         w[:, n_main + n_small:]], axis=1).astype(BF16)


def _block_diag_const(width, value):
    r = jnp.arange(width) // HEAD_DIM
    return jnp.where(r[:, None] == r[None, :], value, 0.0).astype(F32)


def _halo_specs(n_ctx_tiles, lat_off, d, n_ctx_rows, n_lat_rows):
    r8 = ROW_TILE // SUBLANES
    c_last, l_last = n_ctx_rows // SUBLANES - 1, n_lat_rows // SUBLANES - 1
    lo8 = lat_off * r8
    blk = lambda f: pl.BlockSpec((SUBLANES, d), f)
    return [blk(lambda i: (jnp.clip(i * r8 - 1, 0, c_last), 0)),
            blk(lambda i: (jnp.clip((i + 1) * r8, 0, c_last), 0)),
            blk(lambda i: (jnp.clip((i - n_ctx_tiles) * r8 - 1, 0, l_last) + lo8, 0)),
            blk(lambda i: (jnp.clip((i - n_ctx_tiles + 1) * r8, 0, l_last) + lo8, 0))]


def _inproj(ctx_src, lat_src, lat_off, n, n_ctx, mod_l, w_pad, rope, gq, gk, conv_w, a_log,
            dt_bias):
    d = lat_src.shape[1]
    tm = ROW_TILE
    nct = n_ctx // tm
    bd = _block_diag_const(LANES, 1.0 / HEAD_DIM)
    bdb = _block_diag_const(B_WIDTH, 1.0)
    par = jnp.zeros((2, LANES), F32)
    par = par.at[0, 2 * B_HEADS:4 * B_HEADS].set(a_log.reshape(-1))
    par = par.at[1, 2 * B_HEADS:4 * B_HEADS].set(dt_bias.reshape(-1))
    row = lambda w: pl.BlockSpec((tm, w), lambda i: (i, 0))
    lat_row = lambda w: pl.BlockSpec((tm, w), lambda i: (jnp.maximum(i - nct, 0), 0))
    full = lambda a: pl.BlockSpec(a.shape, lambda i: (0,) * a.ndim)
    outs = [("ga", A_WIDTH), ("qn", B_WIDTH), ("kn", B_WIDTH), ("vn", B_WIDTH), ("bg", LANES),
            ("gb", B_WIDTH), ("uc", C_WIDTH), ("gc", C_WIDTH)]
    return pl.pallas_call(
        functools.partial(_inproj_kernel, n_ctx_tiles=nct, n_tiles=n // tm, d_model=d),
        grid=(n // tm,),
        in_specs=_row_source_specs(nct, lat_off, d)
        + _halo_specs(nct, lat_off, d, n_ctx, n - n_ctx)
        + [full(mod_l), full(w_pad), lat_row(LANES), lat_row(LANES), lat_row(LANES), full(gq),
           full(gk), full(bd), full(conv_w), full(par), full(bdb)],
        out_specs=[pl.BlockSpec((A_Q_HEADS, tm, LANES), lambda i: (0, i, 0)),
                   row(LANES),
                   pl.BlockSpec((A_KV_HEADS, LANES, tm), lambda i: (0, 0, i))]
        + [row(w) for _, w in outs],
        out_shape=[jax.ShapeDtypeStruct((A_Q_HEADS, n, LANES), BF16),
                   jax.ShapeDtypeStruct((n, LANES), BF16),
                   jax.ShapeDtypeStruct((A_KV_HEADS, LANES, n), BF16)]
        + [jax.ShapeDtypeStruct((n, w), F32) for _, w in outs],
        compiler_params=_cparams("arbitrary"),
        name="inproj",
    )(ctx_src, lat_src, ctx_src, ctx_src, lat_src, lat_src, mod_l, w_pad, *rope, gq, gk, bd,
      conv_w, par, bdb)


def _flash_kernel(q_ref, k_ref, vt_ref, g_ref, o_ref, m_sc, mblk_sc, alpha_sc, acc_sc, s_sc, p_sc,
                  *, n_ctx, n_ctx_tiles, tc, n_chunks):
    is_ctx = pl.program_id(0) < n_ctx_tiles
    tq = q_ref.shape[1]
    m_sc[...] = jnp.full(m_sc.shape, -jnp.inf, F32)
    acc_sc[...] = jnp.zeros(acc_sc.shape, F32)

    def q_group(g):
        return q_ref[g * A_GROUPS:(g + 1) * A_GROUPS].reshape(A_GROUPS * tq, LANES)

    def scores(g, c0, size):
        s = _dg(k_ref[pl.ds(c0, size), :], q_group(g), _NT)
        s_sc[g, :size, :] = s
        mblk_sc[g] = jnp.max(s, axis=0, keepdims=True)

    def exponentials(g, size):
        m_old = m_sc[g]
        m_new = jnp.maximum(m_old, mblk_sc[g])
        alpha_sc[g] = jnp.exp2(m_old - m_new)
        p_sc[g, :size, :] = jnp.exp2(s_sc[g, :size, :] - m_new).astype(BF16)
        m_sc[g] = m_new

    def weighted_values(g, c0, size):
        acc_sc[g] = alpha_sc[g] * acc_sc[g] + jnp.dot(
            vt_ref[g, :, pl.ds(c0, size)], p_sc[g, :size, :], preferred_element_type=F32)

    @pl.when(is_ctx)
    def _():
        for g in range(A_KV_HEADS):
            scores(g, 0, n_ctx)
            exponentials(g, n_ctx)
            weighted_values(g, 0, n_ctx)

    @pl.when(jnp.logical_not(is_ctx))
    def _():
        first, second = 0, A_KV_HEADS - 1
        p_sc[second] = jnp.zeros(p_sc.shape[1:], BF16)
        alpha_sc[second] = jnp.ones(alpha_sc.shape[1:], F32)
        scores(first, 0, tc)

        def body(c, carry):
            c0 = pl.multiple_of(c * tc, tc)
            c_prev = pl.multiple_of(jnp.maximum(c - 1, 0) * tc, tc)
            c_next = pl.multiple_of(jnp.minimum(c + 1, n_chunks - 1) * tc, tc)
            scores(second, c0, tc)
            exponentials(first, tc)
            weighted_values(second, c_prev, tc)
            scores(first, c_next, tc)
            exponentials(second, tc)
            weighted_values(first, c0, tc)
            return carry

        lax.fori_loop(0, n_chunks, body, 0)
        weighted_values(second, (n_chunks - 1) * tc, tc)

    _flash_finalize(acc_sc, g_ref, o_ref, tq)


def _flash_finalize(acc_sc, g_ref, o_ref, tq):
    heads = []
    for h in range(A_Q_HEADS):
        g, r = divmod(h, A_GROUPS)
        acc = acc_sc[g, :, r * tq:(r + 1) * tq]
        heads.append((acc[:HEAD_DIM] / acc[HEAD_DIM:HEAD_DIM + 1]).T)
    o_ref[...] = (jnp.concatenate(heads, axis=1) * _silu(g_ref[...])).astype(o_ref.dtype)


def _flash_bounded_kernel(u_ref, q_ref, k_ref, vt_ref, g_ref, o_ref, acc_sc, p_sc, *, n_ctx,
                          n_ctx_tiles, tc, n_chunks):
    is_ctx = pl.program_id(0) < n_ctx_tiles
    tq = q_ref.shape[1]
    acc_sc[...] = jnp.zeros(acc_sc.shape, F32)
    bound = u_ref[0]

    def q_group(g):
        return q_ref[g * A_GROUPS:(g + 1) * A_GROUPS].reshape(A_GROUPS * tq, LANES)

    def probabilities(g, c0, size):
        s = _dg(k_ref[pl.ds(c0, size), :], q_group(g), _NT)
        p_sc[g, :size, :] = jnp.exp2(s - bound).astype(BF16)

    def weighted_values(g, c0, size):
        acc_sc[g] += jnp.dot(vt_ref[g, :, pl.ds(c0, size)], p_sc[g, :size, :],
                             preferred_element_type=F32)

    @pl.when(is_ctx)
    def _():
        for g in range(A_KV_HEADS):
            probabilities(g, 0, n_ctx)
            weighted_values(g, 0, n_ctx)

    @pl.when(jnp.logical_not(is_ctx))
    def _():
        first, second = 0, A_KV_HEADS - 1
        probabilities(first, 0, tc)

        def body(c, carry):
            c0 = pl.multiple_of(c * tc, tc)
            probabilities(second, c0, tc)
            weighted_values(first, c0, tc)
            probabilities(first, pl.multiple_of(c0 + tc, tc), tc)
            weighted_values(second, c0, tc)
            return carry

        lax.fori_loop(0, n_chunks - 1, body, 0)
        last = (n_chunks - 1) * tc
        probabilities(second, last, tc)
        weighted_values(first, last, tc)
        weighted_values(second, last, tc)

    _flash_finalize(acc_sc, g_ref, o_ref, tq)


def _key_chunk(n):
    for tc in (1280, 1024, 640, 512, 256, 128):
        if n % tc == 0:
            return tc
    raise ValueError(f"no key chunk for {n} rows")


_SCORE_BOUND_PER_GAIN = 1.01 * HEAD_DIM * HEAD_DIM ** -0.5 * LOG2E
_MAX_FIXED_REFERENCE = 40.0


def _flash(qz, k, vt, ga, n_ctx, gain_product):
    n = k.shape[0]
    tq = ROW_TILE
    tc = _key_chunk(n)
    gq = A_GROUPS * tq
    assert A_KV_HEADS == 2 and n_ctx <= tc
    full = lambda a: pl.BlockSpec(a.shape, lambda i: (0,) * a.ndim,
                                  pipeline_mode=pl.Buffered(1))
    statics = dict(n_ctx=n_ctx, n_ctx_tiles=n_ctx // tq, tc=tc, n_chunks=n // tc)
    common = dict(
        grid=(n // tq,),
        out_specs=pl.BlockSpec((tq, A_WIDTH), lambda i: (i, 0)),
        out_shape=jax.ShapeDtypeStruct((n, A_WIDTH), BF16),
        compiler_params=_cparams("arbitrary"))
    in_specs = [pl.BlockSpec((A_Q_HEADS, tq, LANES), lambda i: (0, i, 0)), full(k), full(vt),
                pl.BlockSpec((tq, A_WIDTH), lambda i: (i, 0))]
    acc_shape = pltpu.VMEM((A_KV_HEADS, LANES, gq), F32)
    p_shape = pltpu.VMEM((A_KV_HEADS, tc, gq), BF16)

    def online(bound):
        del bound
        return pl.pallas_call(
            functools.partial(_flash_kernel, **statics), in_specs=in_specs,
            scratch_shapes=[pltpu.VMEM((A_KV_HEADS, 1, gq), F32),
                            pltpu.VMEM((A_KV_HEADS, 1, gq), F32),
                            pltpu.VMEM((A_KV_HEADS, 1, gq), F32),
                            acc_shape,
                            pltpu.VMEM((A_KV_HEADS, tc, gq), F32),
                            p_shape],
            name="flash_attn_online", **common)(qz, k, vt, ga)

    def bounded(bound):
        return pl.pallas_call(
            functools.partial(_flash_bounded_kernel, **statics),
            in_specs=[pl.BlockSpec(memory_space=pltpu.SMEM)] + in_specs,
            scratch_shapes=[acc_shape, p_shape],
            name="flash_attn_bounded", **common)(bound, qz, k, vt, ga)

    bound = (_SCORE_BOUND_PER_GAIN * gain_product).astype(F32).reshape(1)
    return lax.cond(bound[0] <= _MAX_FIXED_REFERENCE, bounded, online, bound)


def _dn_scan_kernel(qf, kf, vf, bgf, qb, kb, vb, bgb, of_ref, ob_ref, s_ref, *, prec, cps):
    @pl.when(pl.program_id(0) == 0)
    def _():
        s_ref[...] = jnp.zeros(s_ref.shape, F32)

    c, hc = B_CHUNK, B_HEADS * B_CHUNK
    row = lax.broadcasted_iota(jnp.int32, (hc, hc), 0)
    col = lax.broadcasted_iota(jnp.int32, (hc, hc), 1)
    same_head = (row // c) == (col // c)
    eye = (row == col).astype(F32)
    level_masks = [(row // 2) == (col // 2)]
    bs = 2
    while bs < c:
        level_masks.append(jnp.logical_and((row // (2 * bs)) == (col // (2 * bs)),
                                           (row // bs) != (col // bs)))
        bs *= 2
    rc, cc = lax.broadcasted_iota(jnp.int32, (c, c), 0), lax.broadcasted_iota(jnp.int32, (c, c), 1)

    def heads_bd(x):
        return jnp.where(same_head, jnp.concatenate([x] * B_HEADS, axis=0), 0.0)

    def rows_of(slab, lane0):
        return jnp.concatenate(
            [jnp.broadcast_to(slab[:, lane0 + h:lane0 + h + 1], (c, hc)) for h in range(B_HEADS)],
            axis=0)

    def chunk_setup(d, q_ref, k_ref, v_ref, bg_ref, r0):
        rows = slice(r0, r0 + c)
        incl = (col <= row) if d == 0 else (col >= row)
        strict = (col < row) if d == 0 else (col > row)
        incl_c = (cc <= rc) if d == 0 else (cc >= rc)
        last = c - 1 if d == 0 else 0
        lb = d * B_HEADS
        lg = 2 * B_HEADS + lb
        bg = bg_ref[rows, :]
        gcum = _mm_pieces(incl_c.astype(F32), bg, 3, split="b")
        gcum_t = gcum.T
        g_row = jnp.concatenate([gcum_t[lg + h:lg + h + 1, :] for h in range(B_HEADS)], axis=1)
        g_last = gcum[last:last + 1, :]
        decay = jnp.where(jnp.logical_and(same_head, incl),
                          jnp.exp(rows_of(gcum, lg) - g_row), 0.0)
        beta = rows_of(bg, lb)
        eg = rows_of(jnp.exp(gcum), lg)
        to_last = rows_of(jnp.exp(g_last - gcum), lg)
        q_bd = heads_bd(q_ref[rows, :])
        k_bd = heads_bd(k_ref[rows, :])
        v_bd = heads_bd(v_ref[rows, :])
        k_b = k_bd.astype(BF16)
        kk = _mm(k_b, k_b, prec, _NT)
        qk = _mm(q_bd, k_b, prec, _NT)
        a_mat = jnp.where(jnp.logical_and(same_head, strict), kk * beta * decay, 0.0)
        g_state = jnp.concatenate(
            [jnp.broadcast_to(jnp.exp(g_last[:, lg + h:lg + h + 1]), (c, hc))
             for h in range(B_HEADS)], axis=0)
        return dict(a_mat=a_mat.astype(BF16),
                    t=(eye - jnp.where(level_masks[0], a_mat, 0.0)).astype(BF16),
                    v_beta=(v_bd * beta).astype(BF16), k_beta=(k_bd * (beta * eg)).astype(BF16),
                    q_dec=(q_bd * eg).astype(BF16), attn=(qk * decay).astype(BF16),
                    k_dec=(k_bd * to_last).astype(BF16), g_state=g_state)

    dirs = ((qf, kf, vf, bgf, of_ref), (qb, kb, vb, bgb, ob_ref))
    chains = [chunk_setup(d, *dirs[d][:4], j * c) for j in range(cps) for d in range(2)]
    zero_b = jnp.zeros((), BF16)
    for lm in level_masks[1:]:
        part = [_mm(ch["t"], jnp.where(lm, ch["a_mat"], zero_b), prec) for ch in chains]
        for ch, pt in zip(chains, part):
            ch["t"] = ch["t"] - _mm(pt, ch["t"], prec).astype(BF16)
    for ch in chains:
        ch["w"] = _mm(ch["t"], ch["v_beta"], prec)
        ch["k_cum"] = _mm(ch["t"], ch["k_beta"], prec)

    states = [s_ref[d] for d in range(2)]
    for step in range(cps):
        sub = [step, cps - 1 - step]
        tms = [chains[sub[d] * 2 + d] for d in range(2)]
        us = [tms[d]["w"] - _mm(tms[d]["k_cum"], states[d], prec) for d in range(2)]
        for d in range(2):
            tm, u, s = tms[d], us[d], states[d]
            o_bd = _mm(tm["q_dec"], s, prec) + _mm(tm["attn"], u, prec)
            states[d] = s * tm["g_state"] + _mm(tm["k_dec"], u, prec, _TN)
            dirs[d][4][sub[d] * c:(sub[d] + 1) * c, :] = functools.reduce(
                jnp.add, [o_bd[h * c:(h + 1) * c] for h in range(B_HEADS)])
    for d in range(2):
        s_ref[d] = states[d]


def _scan_order_maps(n_chunks, n_ctx_chunks):
    fwd = lambda j: (j, 0)
    bwd = lambda j: (jnp.where(j < n_ctx_chunks, n_ctx_chunks - 1 - j,
                               n_chunks - 1 - (j - n_ctx_chunks)), 0)
    return fwd, bwd


def _dn_scan(qn, kn, vn, bg, n_ctx):
    n = qn.shape[0]
    assert B_CHUNK == HEAD_DIM
    c = B_CHUNK * DN_CHUNKS_PER_STEP
    fwd, bwd = _scan_order_maps(n // c, n_ctx // c)
    specs = lambda m: [pl.BlockSpec((c, B_WIDTH), m)] * 3 + [pl.BlockSpec((c, LANES), m)]
    return pl.pallas_call(
        functools.partial(_dn_scan_kernel, prec=DN_PREC, cps=DN_CHUNKS_PER_STEP),
        grid=(n // c,),
        in_specs=specs(fwd) + specs(bwd),
        out_specs=[pl.BlockSpec((c, B_WIDTH), fwd), pl.BlockSpec((c, B_WIDTH), bwd)],
        out_shape=[jax.ShapeDtypeStruct((n, B_WIDTH), F32)] * 2,
        scratch_shapes=[pltpu.VMEM((2, B_WIDTH, B_WIDTH), F32)],
        compiler_params=_cparams("arbitrary"),
        name="dn_scan",
    )(qn, kn, vn, bg, qn, kn, vn, bg)


def _s5_kernel(uf_ref, ub_ref, bre_ref, bim_ref, cre_ref, cimn_ref, a_ref, ap_ref,
               yf_ref, yb_ref, hre_sc, him_sc, carry_sc, *, prec):
    @pl.when(pl.program_id(0) == 0)
    def _():
        carry_sc[...] = jnp.zeros(carry_sc.shape, F32)

    steps, nsub = S5_STEPS, SUBLANES
    ts = steps * nsub
    prow = lax.broadcasted_iota(jnp.int32, (ts, ts), 0)
    pcol = lax.broadcasted_iota(jnp.int32, (ts, ts), 1)
    scan_pos = (prow % nsub) * steps + prow // nsub
    perms = [jnp.where(pcol == scan_pos, 1.0, 0.0).astype(BF16),
             jnp.where(pcol == ts - 1 - scan_pos, 1.0, 0.0).astype(BF16)]
    ndir = 2

    def permute(pm, x, dims, pieces):
        return _mm_pieces(pm, x, pieces, split="b", dims=dims)

    u_perms = [permute(perms[d], u_ref[...], ((1,), (0,)), 1 if prec == 1 else 2)
               for d, u_ref in enumerate((uf_ref, ub_ref))]
    for d in range(ndir):
        hre_sc[d] = _mm(u_perms[d], bre_ref[d], prec)
    for d in range(ndir):
        him_sc[d] = _mm(u_perms[d], bim_ref[d], prec)

    def rows_at(jj):
        return pl.ds(pl.multiple_of(jj * nsub, nsub), nsub)

    def local_scan(jj, carry):
        rows = rows_at(jj)
        out = []
        for d in range(ndir):
            a_re, a_im = a_ref[d, 0], a_ref[d, 1]
            h_re, h_im = carry[d]
            n_re = a_re * h_re - a_im * h_im + hre_sc[d, rows, :]
            n_im = a_re * h_im + a_im * h_re + him_sc[d, rows, :]
            hre_sc[d, rows, :] = n_re
            him_sc[d, rows, :] = n_im
            out.append((n_re, n_im))
        return tuple(out)

    zero = jnp.zeros((nsub, C_STATES), F32)
    ends = lax.fori_loop(0, steps, local_scan, ((zero, zero),) * ndir)
    cins = []
    for d in range(ndir):
        as_re = ap_ref[d, 0, steps - 1, 0:1, :]
        as_im = ap_ref[d, 1, steps - 1, 0:1, :]
        c_re, c_im = carry_sc[2 * d:2 * d + 1, :], carry_sc[2 * d + 1:2 * d + 2, :]
        cin = []
        for s in range(nsub):
            cin.append((c_re, c_im))
            c_re, c_im = (ends[d][0][s:s + 1] + as_re * c_re - as_im * c_im,
                          ends[d][1][s:s + 1] + as_re * c_im + as_im * c_re)
        carry_sc[2 * d:2 * d + 1, :] = c_re
        carry_sc[2 * d + 1:2 * d + 2, :] = c_im
        cins.append((jnp.concatenate([x[0] for x in cin], axis=0),
                     jnp.concatenate([x[1] for x in cin], axis=0)))

    def fixup(jj, carry):
        rows = rows_at(jj)
        for d in range(ndir):
            p_re = ap_ref[d, 0, jj]
            p_im = ap_ref[d, 1, jj]
            cin_re, cin_im = cins[d]
            hre_sc[d, rows, :] += p_re * cin_re - p_im * cin_im
            him_sc[d, rows, :] += p_re * cin_im + p_im * cin_re
        return carry

    lax.fori_loop(0, steps, fixup, 0)
    y_re = [_mm(hre_sc[d], cre_ref[d], prec) for d in range(ndir)]
    y_im = [_mm(him_sc[d], cimn_ref[d], prec) for d in range(ndir)]
    for d, y_ref in enumerate((yf_ref, yb_ref)):
        y_ref[...] = permute(perms[d], y_re[d] + y_im[d], _TN, 2)


def _s5_params(a_re, a_im, log_dt, b_re, b_im, c_re, c_im):
    dt = jnp.exp(log_dt.astype(F32))[..., None]
    lam_re, lam_im = a_re.astype(F32), a_im.astype(F32)
    k = jnp.arange(1, S5_STEPS + 1, dtype=F32)[None, :, None, None]
    mag = jnp.exp(lam_re[:, None] * dt[:, None] * k)
    ang = lam_im[:, None] * dt[:, None] * k
    ap = jnp.stack([mag * jnp.cos(ang), mag * jnp.sin(ang)], axis=1)
    ap = ap.reshape(2, 2, S5_STEPS, C_STATES)
    ab_re, ab_im = ap[:, 0, 0], ap[:, 1, 0]
    abar = jnp.stack([ab_re, ab_im], axis=1)
    x, y = ab_re.reshape(lam_re.shape) - 1.0, ab_im.reshape(lam_re.shape)
    den = lam_re * lam_re + lam_im * lam_im
    cf_re, cf_im = (x * lam_re + y * lam_im) / den, (y * lam_re - x * lam_im) / den
    bb_re = cf_re[..., None] * b_re - cf_im[..., None] * b_im
    bb_im = cf_re[..., None] * b_im + cf_im[..., None] * b_re
    eye = jnp.eye(C_GROUPS, dtype=F32)
    dense_b = lambda b: jnp.einsum('dgpc,gh->dgchp', b, eye).reshape(2, C_WIDTH, C_STATES)
    dense_c = lambda c: jnp.einsum('dgcp,gh->dgphc', c, eye).reshape(2, C_STATES, C_WIDTH)
    abar = jnp.broadcast_to(abar[:, :, None, :], (2, 2, SUBLANES, C_STATES))
    ap = jnp.broadcast_to(ap[:, :, :, None, :], (2, 2, S5_STEPS, SUBLANES, C_STATES))
    return (dense_b(bb_re), dense_b(bb_im), dense_c(c_re.astype(F32)),
            dense_c(-c_im.astype(F32)), abar, ap)


def _s5_scan(uc, params, layer, n_ctx):
    n = uc.shape[0]
    ts = ROW_TILE
    bre, bim, cre, cimn, abar, ap = params
    fwd, bwd = _scan_order_maps(n // ts, n_ctx // ts)
    full = lambda a: pl.BlockSpec((None,) + a.shape[1:],
                                  lambda j: (layer,) + (0,) * (a.ndim - 1),
                                  pipeline_mode=pl.Buffered(1))
    return pl.pallas_call(
        functools.partial(_s5_kernel, prec=S5_PREC),
        grid=(n // ts,),
        in_specs=[pl.BlockSpec((ts, C_WIDTH), fwd), pl.BlockSpec((ts, C_WIDTH), bwd),
                  full(bre), full(bim), full(cre), full(cimn), full(abar), full(ap)],
        out_specs=[pl.BlockSpec((ts, C_WIDTH), fwd), pl.BlockSpec((ts, C_WIDTH), bwd)],
        out_shape=[jax.ShapeDtypeStruct((n, C_WIDTH), F32)] * 2,
        scratch_shapes=[pltpu.VMEM((2, ts, C_STATES), F32), pltpu.VMEM((2, ts, C_STATES), F32),
                        pltpu.VMEM((SUBLANES, C_STATES), F32)],
        compiler_params=_cparams("arbitrary"),
        name="s5_scan",
    )(uc, uc, bre, bim, cre, cimn, abar, ap)


def _gelu_tanh(x):
    return 0.5 * x * (1.0 + jnp.tanh(math.sqrt(2.0 / math.pi) * (x + 0.044715 * (x * x * x))))


def _out_kernel(c_ref, l_ref, mod_ref, ao_ref, dof_ref, dob_ref, gb_ref, u_ref, yf_ref, yb_ref,
                gc_ref, vec_ref, bd_ref, gluw_ref, wo_ref, lnv_ref, o_ref, *, n_ctx_tiles,
                first_tile, d_model, alpha):
    is_ctx = pl.program_id(0) + first_tile < n_ctx_tiles
    vec = vec_ref[...]
    o = dof_ref[...] + dob_ref[...]
    ms = _mm_pieces(o * o, bd_ref[...], 2)
    dn = o * lax.rsqrt(ms + NORM_EPS) * vec[0:1] * _silu(gb_ref[...])
    z = _gelu_tanh(u_ref[...] * vec[1:2] + yf_ref[...] + yb_ref[...])
    lin = jnp.dot(z.astype(BF16), gluw_ref[...], preferred_element_type=F32) + vec[2:3]
    s5 = z * jax.nn.sigmoid(lin) * _silu(gc_ref[...])
    wo = wo_ref
    proj = (jnp.dot(ao_ref[...], wo[:A_WIDTH, :], preferred_element_type=F32)
            + jnp.dot(dn.astype(BF16), wo[A_WIDTH:A_WIDTH + B_WIDTH, :],
                      preferred_element_type=F32)
            + jnp.dot(s5.astype(BF16), wo[A_WIDTH + B_WIDTH:, :], preferred_element_type=F32))
    gate = _mod_row(mod_ref, is_ctx)[:, 2 * d_model:]
    lnv = lnv_ref[...]
    x = jnp.where(is_ctx, c_ref[...], l_ref[...])
    o_ref[...] = _layer_norm(alpha * x + gate * proj) * lnv[0:1] + lnv[1:2]


def _out_proj(ctx_src, lat_src, lat_off, n, first_tile, mod_l, ao, dof, dob, gb, uc, yf, yb, gc,
              vec, gluw, wo, lnv, n_ctx_tiles, alpha):
    d = lat_src.shape[1]
    tm = ROW_TILE
    bd = _block_diag_const(B_WIDTH, 1.0 / HEAD_DIM)
    row = lambda w: pl.BlockSpec((tm, w), lambda i: (i + first_tile, 0))
    full = lambda a: pl.BlockSpec(a.shape, lambda i: (0,) * a.ndim)
    return pl.pallas_call(
        functools.partial(_out_kernel, n_ctx_tiles=n_ctx_tiles, first_tile=first_tile, d_model=d,
                          alpha=alpha),
        grid=(n // tm - first_tile,),
        in_specs=_row_source_specs(n_ctx_tiles, lat_off, d, first_tile)
        + [full(mod_l), row(A_WIDTH), row(B_WIDTH), row(B_WIDTH), row(B_WIDTH),
           row(C_WIDTH), row(C_WIDTH), row(C_WIDTH), row(C_WIDTH), full(vec), full(bd),
           full(gluw), full(wo), full(lnv)],
        out_specs=pl.BlockSpec((tm, d), lambda i: (i, 0)),
        out_shape=jax.ShapeDtypeStruct((n - first_tile * tm, d), F32),
        compiler_params=_cparams("arbitrary"),
        name="out_proj",
    )(ctx_src, lat_src, mod_l, ao, dof, dob, gb, uc, yf, yb, gc, vec, bd, gluw, wo, lnv)


def _rope_tables(n_lat):
    rows = n_lat // GRID_W
    axis_dim = HEAD_DIM // 2
    inv = ROPE_THETA ** (-jnp.arange(0, axis_dim, 2, dtype=F32) / axis_dim)
    lane = jnp.arange(LANES)
    inv_lane = inv[lane % (axis_dim // 2)]
    row_lane = (lane % HEAD_DIM) < axis_dim
    upper = (lane % axis_dim) >= axis_dim // 2
    ar = jnp.arange(rows, dtype=F32)[:, None] * inv_lane
    ac = jnp.arange(GRID_W, dtype=F32)[:, None] * inv_lane

    def expand(fr, fc):
        return jnp.where(row_lane, fr[:, None, :], fc[None, :, :]).reshape(n_lat, LANES)

    sin_r, sin_c = jnp.sin(ar), jnp.sin(ac)
    cos = expand(jnp.cos(ar), jnp.cos(ac))
    sa = expand(jnp.where(upper, 0.0, -sin_r), jnp.where(upper, 0.0, -sin_c))
    sb = expand(jnp.where(upper, sin_r, 0.0), jnp.where(upper, sin_c, 0.0))
    return cos, sa, sb


def kernel(x, c, ctx, c_ctx, w_mod, b_mod, w_in, attn_q_gain, attn_k_gain, dn_conv_w, dn_A_log, dn_dt_bias, dn_out_gain, s5_A_re, s5_A_im, s5_log_dt, s5_B_re, s5_B_im, s5_C_re, s5_C_im, s5_D, glu_w, glu_b, w_out, ln_g, ln_b):
    batch, n_lat, d = x.shape
    n_ctx = ctx.shape[1]
    depth = w_mod.shape[0]
    assert batch == 1 and n_ctx % ROW_TILE == 0 and n_lat % ROW_TILE == 0 and n_lat % GRID_W == 0
    nct = n_ctx // ROW_TILE
    alpha = (2 * depth) ** 0.25

    n = n_ctx + n_lat
    cv = jnp.zeros((SUBLANES, d), F32).at[0].set(c[0]).at[1].set(c_ctx)
    mod = _modulation(cv, w_mod, b_mod)
    cos, sa, sb = _rope_tables(n_lat)
    lane_tile = lambda g: jnp.tile(g.astype(F32)[None, :], (1, LANES // HEAD_DIM))
    s5_par = jax.vmap(_s5_params)(s5_A_re, s5_A_im, s5_log_dt, s5_B_re, s5_B_im, s5_C_re, s5_C_im)

    ctx_src, lat_src, lat_off = ctx[0], x[0], 0
    for l in range(depth):
        (qz, kp, vt, ga, qn, kn, vn, bg, gb, uc, gc) = _inproj(
            ctx_src, lat_src, lat_off, n, n_ctx, mod[l], _pad_w_in(w_in[l]), (cos, sa, sb),
            lane_tile(attn_q_gain[l]), lane_tile(attn_k_gain[l]), dn_conv_w[l].astype(F32),
            dn_A_log[l], dn_dt_bias[l])
        gain_product = jnp.max(jnp.abs(attn_q_gain[l])) * jnp.max(jnp.abs(attn_k_gain[l]))
        ao = _flash(qz, kp, vt, ga, n_ctx, gain_product)
        dof, dob = _dn_scan(qn, kn, vn, bg, n_ctx)
        yf, yb = _s5_scan(uc, s5_par, l, n_ctx)
        vec = jnp.zeros((SUBLANES, C_WIDTH), F32)
        vec = vec.at[0].set(jnp.tile(dn_out_gain[l].astype(F32), B_HEADS))
        vec = vec.at[1].set(s5_D[l].astype(F32)).at[2].set(glu_b[l].astype(F32))
        lnv = jnp.zeros((SUBLANES, d), F32).at[0].set(ln_g[l]).at[1].set(ln_b[l])
        first_tile = nct if l == depth - 1 else 0
        xall = _out_proj(ctx_src, lat_src, lat_off, n, first_tile, mod[l], ao, dof, dob, gb, uc,
                         yf, yb, gc, vec, glu_w[l].astype(BF16), w_out[l].astype(BF16), lnv, nct,
                         alpha)
        ctx_src, lat_src, lat_off = xall, xall, nct
    return xall[None]
```

```python
import functools
import math

import jax
import jax.numpy as jnp
from jax import lax
from jax.experimental import pallas as pl
from jax.experimental.pallas import tpu as pltpu

F32 = jnp.float32
BF16 = jnp.bfloat16
HIGHEST = lax.Precision.HIGHEST

GRID_W = 64
HEAD_DIM = 64
NORM_EPS = 1e-6
ROPE_THETA = 10000.0
A_Q_HEADS = 8
A_KV_HEADS = 2
A_GROUPS = A_Q_HEADS // A_KV_HEADS
A_WIDTH = A_Q_HEADS * HEAD_DIM
A_KV_WIDTH = A_KV_HEADS * HEAD_DIM
B_HEADS = 4
B_WIDTH = B_HEADS * HEAD_DIM
B_CHUNK = 64
C_WIDTH = 256
C_GROUP = 16
C_GROUPS = C_WIDTH // C_GROUP
C_STATE = 64
C_STATES = C_GROUPS * C_STATE
LANES = 128
SUBLANES = 8
ROW_TILE = 256
S5_STEPS = ROW_TILE // SUBLANES
VMEM_LIMIT = 48 * 1024 * 1024
LOG2E = math.log2(math.e)

DN_PREC = 1
S5_PREC = 1
DN_CHUNKS_PER_STEP = 4


def _cparams(*sem):
    return pltpu.CompilerParams(dimension_semantics=sem, vmem_limit_bytes=VMEM_LIMIT)


def _split(a):
    hi = a.astype(BF16)
    lo = (a - hi.astype(F32)).astype(BF16)
    return hi, lo


def _dg(a, b, dims):
    return lax.dot_general(a, b, (dims, ((), ())), preferred_element_type=F32)


def _mm(a, b, prec, dims=((1,), (0,))):
    if prec == 6:
        return lax.dot_general(a, b, (dims, ((), ())), precision=HIGHEST,
                               preferred_element_type=F32)
    if prec == 1:
        return _dg(a.astype(BF16), b.astype(BF16), dims)
    ah, al = _split(a)
    bh, bl = _split(b)
    return _dg(ah, bh, dims) + (_dg(ah, bl, dims) + _dg(al, bh, dims))


_NT = ((1,), (1,))
_TN = ((0,), (0,))


def _mm_pieces(a, b, pieces, split="a", dims=((1,), (0,))):
    exact = (b if split == "a" else a).astype(BF16)
    out, rest = None, (a if split == "a" else b)
    for _ in range(pieces):
        part = rest.astype(BF16)
        rest = rest - part.astype(F32)
        term = _dg(part, exact, dims) if split == "a" else _dg(exact, part, dims)
        out = term if out is None else out + term
    return out


def _silu(x):
    return x * jax.nn.sigmoid(x)


def _mod_kernel(cv_ref, w_ref, b_ref, o_ref):
    sv = _silu(cv_ref[...])
    o_ref[0] = _mm(sv, w_ref[0], 3) + b_ref[0]


def _modulation(cv, w_mod, b_mod):
    depth, d, d3 = w_mod.shape
    nblk = d3 // d
    return pl.pallas_call(
        _mod_kernel,
        grid=(depth, nblk),
        in_specs=[pl.BlockSpec((SUBLANES, d), lambda l, n: (0, 0)),
                  pl.BlockSpec((1, d, d), lambda l, n: (l, 0, n)),
                  pl.BlockSpec((1, 1, d), lambda l, n: (l, 0, n))],
        out_specs=pl.BlockSpec((1, SUBLANES, d), lambda l, n: (l, 0, n)),
        out_shape=jax.ShapeDtypeStruct((depth, SUBLANES, d3), F32),
        compiler_params=_cparams("arbitrary", "arbitrary"),
        name="modulation",
    )(cv, w_mod, b_mod.reshape(depth, 1, d3))


def _layer_norm(x):
    mu = jnp.mean(x, axis=-1, keepdims=True)
    xc = x - mu
    var = jnp.mean(xc * xc, axis=-1, keepdims=True)
    return xc * lax.rsqrt(var + NORM_EPS)


def _mod_row(mod_ref, is_ctx):
    mod = mod_ref[...]
    return jnp.where(is_ctx, mod[1:2], mod[0:1])


def _column_ranges(widths):
    out, off = {}, 0
    for name, w in widths:
        out[name] = (off, off + w)
        off += w
    return out


_IN_COLS = _column_ranges((("qkva", A_WIDTH + 2 * A_KV_WIDTH), ("ga", A_WIDTH),
                           ("qkvb", 3 * B_WIDTH), ("gb", B_WIDTH), ("bd", LANES),
                           ("uc", C_WIDTH), ("gc", C_WIDTH)))


def _row_source_specs(n_ctx_tiles, lat_off, d, first_tile=0):
    tm = ROW_TILE
    return [pl.BlockSpec((tm, d), lambda i: (jnp.minimum(i + first_tile, n_ctx_tiles - 1), 0)),
            pl.BlockSpec((tm, d),
                         lambda i: (jnp.maximum(i + first_tile - n_ctx_tiles, 0) + lat_off, 0))]


def _attn_prep_tile(q, k, v, cos, sa, sb, gq, gk, bd, qo_ref, ko_ref, vo_ref):
    def norm_rope(y, gain):
        ms = _mm_pieces(y * y, bd, 2)
        yn = y * lax.rsqrt(ms + NORM_EPS) * gain
        return (yn * cos + pltpu.roll(yn, LANES - HEAD_DIM // 4, 1) * sa
                + pltpu.roll(yn, HEAD_DIM // 4, 1) * sb)

    lane = lax.broadcasted_iota(jnp.int32, (1, LANES), 1)
    lower = lane < HEAD_DIM
    for j in range(A_Q_HEADS // 2):
        y = norm_rope(q[:, j * LANES:(j + 1) * LANES], gq) * (HEAD_DIM ** -0.5 * LOG2E)
        y_sw = pltpu.roll(y, HEAD_DIM, 1)
        for half in range(2):
            h = 2 * j + half
            src = y if (h // A_GROUPS) == half else y_sw
            keep = lower if (h // A_GROUPS) == 0 else jnp.logical_not(lower)
            qo_ref[h] = jnp.where(keep, src, 0.0).astype(BF16)
        yield
    ko_ref[...] = norm_rope(k, gk).astype(BF16)
    yield
    vt = v.T
    ones = jnp.ones((HEAD_DIM, vt.shape[1]), F32)
    for h in range(A_KV_HEADS):
        vo_ref[h] = jnp.concatenate([vt[h * HEAD_DIM:(h + 1) * HEAD_DIM], ones],
                                    axis=0).astype(BF16)


def _dn_prep_tile(z, prev_row, next_row, r, w, par, bd, q_o, k_o, v_o, bg_o):
    tm = z.shape[0]
    row = lax.broadcasted_iota(jnp.int32, (tm, 1), 0)

    def conv(cols):
        zc, wc = z[:, cols], w[:, cols]
        z_prev = jnp.where(row == 0, prev_row[:, cols], pltpu.roll(zc, 1, 0))
        z_next = jnp.where(row == tm - 1, next_row[:, cols], pltpu.roll(zc, tm - 1, 0))
        return _silu(z_prev * wc[0:1] + zc * wc[1:2] + z_next * wc[2:3])

    def l2norm(t):
        return t * lax.rsqrt(_mm_pieces(t * t, bd, 2) + NORM_EPS)

    q_o[...] = l2norm(conv(slice(0, B_WIDTH))) * (HEAD_DIM ** -0.5)
    yield
    k_o[...] = l2norm(conv(slice(B_WIDTH, 2 * B_WIDTH)))
    yield
    v_o[...] = conv(slice(2 * B_WIDTH, 3 * B_WIDTH))
    xa = r + par[1:2]
    softplus = jnp.maximum(xa, 0.0) + jnp.log(1.0 + jnp.exp(-jnp.abs(xa)))
    g = -jnp.exp(par[0:1]) * softplus
    lane = lax.broadcasted_iota(jnp.int32, (1, LANES), 1)
    bg_o[...] = jnp.where(lane < 2 * B_HEADS, jax.nn.sigmoid(r), g)


def _inproj_kernel(c_ref, l_ref, cp_ref, cn_ref, lp_ref, ln_ref, mod_ref, w_ref, cos_ref, sa_ref,
                   sb_ref, gq_ref, gk_ref, bd_ref, cw_ref, par_ref, bdb_ref,
                   qo_ref, ko_ref, vo_ref, ga_ref, qn_ref, kn_ref, vn_ref, bg_ref, gb_ref, uc_ref,
                   gc_ref, *, n_ctx_tiles, n_tiles, d_model):
    i = pl.program_id(0)
    is_ctx = i < n_ctx_tiles
    row = _mod_row(mod_ref, is_ctx)
    shift, scale = row[:, :d_model], row[:, d_model:2 * d_model]
    modulate = lambda t: (_layer_norm(t) * (1.0 + scale) + shift).astype(BF16)
    hb = modulate(jnp.where(is_ctx, c_ref[...], l_ref[...]))
    proj = lambda lhs, name: jnp.dot(lhs, w_ref[:, _IN_COLS[name][0]:_IN_COLS[name][1]],
                                     preferred_element_type=F32)
    za = proj(hb, "qkva")
    cos = jnp.where(is_ctx, 1.0, cos_ref[...])
    sa, sb = jnp.where(is_ctx, 0.0, sa_ref[...]), jnp.where(is_ctx, 0.0, sb_ref[...])
    attn_prep = _attn_prep_tile(za[:, :A_WIDTH], za[:, A_WIDTH:A_WIDTH + A_KV_WIDTH],
                                za[:, A_WIDTH + A_KV_WIDTH:], cos, sa, sb, gq_ref[...],
                                gk_ref[...], bd_ref[...], qo_ref, ko_ref, vo_ref)
    zb = proj(hb, "qkvb")
    next(attn_prep, None)
    halo = modulate(jnp.where(is_ctx, jnp.concatenate([cp_ref[...], cn_ref[...]], axis=0),
                              jnp.concatenate([lp_ref[...], ln_ref[...]], axis=0)))
    zh = proj(halo, "qkvb")
    seg_start = jnp.logical_or(i == 0, i == n_ctx_tiles)
    seg_end = jnp.logical_or(i == n_ctx_tiles - 1, i == n_tiles - 1)
    next(attn_prep, None)
    dn_prep = _dn_prep_tile(zb, jnp.where(seg_start, 0.0, zh[SUBLANES - 1:SUBLANES]),
                            jnp.where(seg_end, 0.0, zh[SUBLANES:SUBLANES + 1]),
                            proj(hb, "bd"), cw_ref[...], par_ref[...], bdb_ref[...],
                            qn_ref, kn_ref, vn_ref, bg_ref)
    for name, ref in (("ga", ga_ref), ("gb", gb_ref), ("uc", uc_ref), ("gc", gc_ref)):
        next(attn_prep, None)
        ref[...] = proj(hb, name)
        next(dn_prep, None)
    for rest in (attn_prep, dn_prep):
        for _ in rest:
            pass


def _pad_w_in(w):
    d = w.shape[0]
    n_main = 2 * A_WIDTH + 2 * A_KV_WIDTH + 4 * B_WIDTH
    n_small = 4 * B_HEADS
    return jnp.concatenate(
        [w[:, :n_main], w[:, n_main:n_main + n_small], jnp.zeros((d, LANES - n_small), w.dtype),
         w[:, n_main + n_small:]], axis=1).astype(BF16)


def _block_diag_const(width, value):
    r = jnp.arange(width) // HEAD_DIM
    return jnp.where(r[:, None] == r[None, :], value, 0.0).astype(F32)


def _halo_specs(n_ctx_tiles, lat_off, d, n_ctx_rows, n_lat_rows):
    r8 = ROW_TILE // SUBLANES
    c_last, l_last = n_ctx_rows // SUBLANES - 1, n_lat_rows // SUBLANES - 1
    lo8 = lat_off * r8
    blk = lambda f: pl.BlockSpec((SUBLANES, d), f)
    return [blk(lambda i: (jnp.clip(i * r8 - 1, 0, c_last), 0)),
            blk(lambda i: (jnp.clip((i + 1) * r8, 0, c_last), 0)),
            blk(lambda i: (jnp.clip((i - n_ctx_tiles) * r8 - 1, 0, l_last) + lo8, 0)),
            blk(lambda i: (jnp.clip((i - n_ctx_tiles + 1) * r8, 0, l_last) + lo8, 0))]


def _inproj(ctx_src, lat_src, lat_off, n, n_ctx, mod_l, w_pad, rope, gq, gk, conv_w, a_log,
            dt_bias):
    d = lat_src.shape[1]
    tm = ROW_TILE
    nct = n_ctx // tm
    bd = _block_diag_const(LANES, 1.0 / HEAD_DIM)
    bdb = _block_diag_const(B_WIDTH, 1.0)
    par = jnp.zeros((2, LANES), F32)
    par = par.at[0, 2 * B_HEADS:4 * B_HEADS].set(a_log.reshape(-1))
    par = par.at[1, 2 * B_HEADS:4 * B_HEADS].set(dt_bias.reshape(-1))
    row = lambda w: pl.BlockSpec((tm, w), lambda i: (i, 0))
    lat_row = lambda w: pl.BlockSpec((tm, w), lambda i: (jnp.maximum(i - nct, 0), 0))
    full = lambda a: pl.BlockSpec(a.shape, lambda i: (0,) * a.ndim)
    outs = [("ga", A_WIDTH), ("qn", B_WIDTH), ("kn", B_WIDTH), ("vn", B_WIDTH), ("bg", LANES),
            ("gb", B_WIDTH), ("uc", C_WIDTH), ("gc", C_WIDTH)]
    return pl.pallas_call(
        functools.partial(_inproj_kernel, n_ctx_tiles=nct, n_tiles=n // tm, d_model=d),
        grid=(n // tm,),
        in_specs=_row_source_specs(nct, lat_off, d)
        + _halo_specs(nct, lat_off, d, n_ctx, n - n_ctx)
        + [full(mod_l), full(w_pad), lat_row(LANES), lat_row(LANES), lat_row(LANES), full(gq),
           full(gk), full(bd), full(conv_w), full(par), full(bdb)],
        out_specs=[pl.BlockSpec((A_Q_HEADS, tm, LANES), lambda i: (0, i, 0)),
                   row(LANES),
                   pl.BlockSpec((A_KV_HEADS, LANES, tm), lambda i: (0, 0, i))]
        + [row(w) for _, w in outs],
        out_shape=[jax.ShapeDtypeStruct((A_Q_HEADS, n, LANES), BF16),
                   jax.ShapeDtypeStruct((n, LANES), BF16),
                   jax.ShapeDtypeStruct((A_KV_HEADS, LANES, n), BF16)]
        + [jax.ShapeDtypeStruct((n, w), F32) for _, w in outs],
        compiler_params=_cparams("arbitrary"),
        name="inproj",
    )(ctx_src, lat_src, ctx_src, ctx_src, lat_src, lat_src, mod_l, w_pad, *rope, gq, gk, bd,
      conv_w, par, bdb)


def _flash_kernel(q_ref, k_ref, vt_ref, g_ref, o_ref, m_sc, mblk_sc, alpha_sc, acc_sc, s_sc, p_sc,
                  *, n_ctx, n_ctx_tiles, tc, n_chunks):
    is_ctx = pl.program_id(0) < n_ctx_tiles
    tq = q_ref.shape[1]
    m_sc[...] = jnp.full(m_sc.shape, -jnp.inf, F32)
    acc_sc[...] = jnp.zeros(acc_sc.shape, F32)

    def q_group(g):
        return q_ref[g * A_GROUPS:(g + 1) * A_GROUPS].reshape(A_GROUPS * tq, LANES)

    def scores(g, c0, size):
        s = _dg(k_ref[pl.ds(c0, size), :], q_group(g), _NT)
        s_sc[g, :size, :] = s
        mblk_sc[g] = jnp.max(s, axis=0, keepdims=True)

    def exponentials(g, size):
        m_old = m_sc[g]
        m_new = jnp.maximum(m_old, mblk_sc[g])
        alpha_sc[g] = jnp.exp2(m_old - m_new)
        p_sc[g, :size, :] = jnp.exp2(s_sc[g, :size, :] - m_new).astype(BF16)
        m_sc[g] = m_new

    def weighted_values(g, c0, size):
        acc_sc[g] = alpha_sc[g] * acc_sc[g] + jnp.dot(
            vt_ref[g, :, pl.ds(c0, size)], p_sc[g, :size, :], preferred_element_type=F32)

    @pl.when(is_ctx)
    def _():
        for g in range(A_KV_HEADS):
            scores(g, 0, n_ctx)
            exponentials(g, n_ctx)
            weighted_values(g, 0, n_ctx)

    @pl.when(jnp.logical_not(is_ctx))
    def _():
        first, second = 0, A_KV_HEADS - 1
        p_sc[second] = jnp.zeros(p_sc.shape[1:], BF16)
        alpha_sc[second] = jnp.ones(alpha_sc.shape[1:], F32)
        scores(first, 0, tc)

        def body(c, carry):
            c0 = pl.multiple_of(c * tc, tc)
            c_prev = pl.multiple_of(jnp.maximum(c - 1, 0) * tc, tc)
            c_next = pl.multiple_of(jnp.minimum(c + 1, n_chunks - 1) * tc, tc)
            scores(second, c0, tc)
            exponentials(first, tc)
            weighted_values(second, c_prev, tc)
            scores(first, c_next, tc)
            exponentials(second, tc)
            weighted_values(first, c0, tc)
            return carry

        lax.fori_loop(0, n_chunks, body, 0)
        weighted_values(second, (n_chunks - 1) * tc, tc)

    _flash_finalize(acc_sc, g_ref, o_ref, tq)


def _flash_finalize(acc_sc, g_ref, o_ref, tq):
    heads = []
    for h in range(A_Q_HEADS):
        g, r = divmod(h, A_GROUPS)
        acc = acc_sc[g, :, r * tq:(r + 1) * tq]
        heads.append((acc[:HEAD_DIM] / acc[HEAD_DIM:HEAD_DIM + 1]).T)
    o_ref[...] = (jnp.concatenate(heads, axis=1) * _silu(g_ref[...])).astype(o_ref.dtype)


def _flash_bounded_kernel(u_ref, q_ref, k_ref, vt_ref, g_ref, o_ref, acc_sc, p_sc, *, n_ctx,
                          n_ctx_tiles, tc, n_chunks):
    is_ctx = pl.program_id(0) < n_ctx_tiles
    tq = q_ref.shape[1]
    acc_sc[...] = jnp.zeros(acc_sc.shape, F32)
    bound = u_ref[0]

    def q_group(g):
        return q_ref[g * A_GROUPS:(g + 1) * A_GROUPS].reshape(A_GROUPS * tq, LANES)

    def probabilities(g, c0, size):
        s = _dg(k_ref[pl.ds(c0, size), :], q_group(g), _NT)
        p_sc[g, :size, :] = jnp.exp2(s - bound).astype(BF16)

    def weighted_values(g, c0, size):
        acc_sc[g] += jnp.dot(vt_ref[g, :, pl.ds(c0, size)], p_sc[g, :size, :],
                             preferred_element_type=F32)

    @pl.when(is_ctx)
    def _():
        for g in range(A_KV_HEADS):
            probabilities(g, 0, n_ctx)
            weighted_values(g, 0, n_ctx)

    @pl.when(jnp.logical_not(is_ctx))
    def _():
        first, second = 0, A_KV_HEADS - 1
        probabilities(first, 0, tc)

        def body(c, carry):
            c0 = pl.multiple_of(c * tc, tc)
            probabilities(second, c0, tc)
            weighted_values(first, c0, tc)
            probabilities(first, pl.multiple_of(c0 + tc, tc), tc)
            weighted_values(second, c0, tc)
            return carry

        lax.fori_loop(0, n_chunks - 1, body, 0)
        last = (n_chunks - 1) * tc
        probabilities(second, last, tc)
        weighted_values(first, last, tc)
        weighted_values(second, last, tc)

    _flash_finalize(acc_sc, g_ref, o_ref, tq)


def _key_chunk(n):
    for tc in (1280, 1024, 640, 512, 256, 128):
        if n % tc == 0:
            return tc
    raise ValueError(f"no key chunk for {n} rows")


_SCORE_BOUND_PER_GAIN = 1.01 * HEAD_DIM * HEAD_DIM ** -0.5 * LOG2E
_MAX_FIXED_REFERENCE = 40.0


def _flash(qz, k, vt, ga, n_ctx, gain_product):
    n = k.shape[0]
    tq = ROW_TILE
    tc = _key_chunk(n)
    gq = A_GROUPS * tq
    assert A_KV_HEADS == 2 and n_ctx <= tc
    full = lambda a: pl.BlockSpec(a.shape, lambda i: (0,) * a.ndim,
                                  pipeline_mode=pl.Buffered(1))
    statics = dict(n_ctx=n_ctx, n_ctx_tiles=n_ctx // tq, tc=tc, n_chunks=n // tc)
    common = dict(
        grid=(n // tq,),
        out_specs=pl.BlockSpec((tq, A_WIDTH), lambda i: (i, 0)),
        out_shape=jax.ShapeDtypeStruct((n, A_WIDTH), BF16),
        compiler_params=_cparams("arbitrary"))
    in_specs = [pl.BlockSpec((A_Q_HEADS, tq, LANES), lambda i: (0, i, 0)), full(k), full(vt),
                pl.BlockSpec((tq, A_WIDTH), lambda i: (i, 0))]
    acc_shape = pltpu.VMEM((A_KV_HEADS, LANES, gq), F32)
    p_shape = pltpu.VMEM((A_KV_HEADS, tc, gq), BF16)

    def online(bound):
        del bound
        return pl.pallas_call(
            functools.partial(_flash_kernel, **statics), in_specs=in_specs,
            scratch_shapes=[pltpu.VMEM((A_KV_HEADS, 1, gq), F32),
                            pltpu.VMEM((A_KV_HEADS, 1, gq), F32),
                            pltpu.VMEM((A_KV_HEADS, 1, gq), F32),
                            acc_shape,
                            pltpu.VMEM((A_KV_HEADS, tc, gq), F32),
                            p_shape],
            name="flash_attn_online", **common)(qz, k, vt, ga)

    def bounded(bound):
        return pl.pallas_call(
            functools.partial(_flash_bounded_kernel, **statics),
            in_specs=[pl.BlockSpec(memory_space=pltpu.SMEM)] + in_specs,
            scratch_shapes=[acc_shape, p_shape],
            name="flash_attn_bounded", **common)(bound, qz, k, vt, ga)

    bound = (_SCORE_BOUND_PER_GAIN * gain_product).astype(F32).reshape(1)
    return lax.cond(bound[0] <= _MAX_FIXED_REFERENCE, bounded, online, bound)


def _dn_scan_kernel(qf, kf, vf, bgf, qb, kb, vb, bgb, of_ref, ob_ref, s_ref, *, prec, cps):
    @pl.when(pl.program_id(0) == 0)
    def _():
        s_ref[...] = jnp.zeros(s_ref.shape, F32)

    c, hc = B_CHUNK, B_HEADS * B_CHUNK
    row = lax.broadcasted_iota(jnp.int32, (hc, hc), 0)
    col = lax.broadcasted_iota(jnp.int32, (hc, hc), 1)
    same_head = (row // c) == (col // c)
    eye = (row == col).astype(F32)
    level_masks = [(row // 2) == (col // 2)]
    bs = 2
    while bs < c:
        level_masks.append(jnp.logical_and((row // (2 * bs)) == (col // (2 * bs)),
                                           (row // bs) != (col // bs)))
        bs *= 2
    rc, cc = lax.broadcasted_iota(jnp.int32, (c, c), 0), lax.broadcasted_iota(jnp.int32, (c, c), 1)

    def heads_bd(x):
        return jnp.where(same_head, jnp.concatenate([x] * B_HEADS, axis=0), 0.0)

    def rows_of(slab, lane0):
        return jnp.concatenate(
            [jnp.broadcast_to(slab[:, lane0 + h:lane0 + h + 1], (c, hc)) for h in range(B_HEADS)],
            axis=0)

    def chunk_setup(d, q_ref, k_ref, v_ref, bg_ref, r0):
        rows = slice(r0, r0 + c)
        incl = (col <= row) if d == 0 else (col >= row)
        strict = (col < row) if d == 0 else (col > row)
        incl_c = (cc <= rc) if d == 0 else (cc >= rc)
        last = c - 1 if d == 0 else 0
        lb = d * B_HEADS
        lg = 2 * B_HEADS + lb
        bg = bg_ref[rows, :]
        gcum = _mm_pieces(incl_c.astype(F32), bg, 3, split="b")
        gcum_t = gcum.T
        g_row = jnp.concatenate([gcum_t[lg + h:lg + h + 1, :] for h in range(B_HEADS)], axis=1)
        g_last = gcum[last:last + 1, :]
        decay = jnp.where(jnp.logical_and(same_head, incl),
                          jnp.exp(rows_of(gcum, lg) - g_row), 0.0)
        beta = rows_of(bg, lb)
        eg = rows_of(jnp.exp(gcum), lg)
        to_last = rows_of(jnp.exp(g_last - gcum), lg)
        q_bd = heads_bd(q_ref[rows, :])
        k_bd = heads_bd(k_ref[rows, :])
        v_bd = heads_bd(v_ref[rows, :])
        k_b = k_bd.astype(BF16)
        kk = _mm(k_b, k_b, prec, _NT)
        qk = _mm(q_bd, k_b, prec, _NT)
        a_mat = jnp.where(jnp.logical_and(same_head, strict), kk * beta * decay, 0.0)
        g_state = jnp.concatenate(
            [jnp.broadcast_to(jnp.exp(g_last[:, lg + h:lg + h + 1]), (c, hc))
             for h in range(B_HEADS)], axis=0)
        return dict(a_mat=a_mat.astype(BF16),
                    t=(eye - jnp.where(level_masks[0], a_mat, 0.0)).astype(BF16),
                    v_beta=(v_bd * beta).astype(BF16), k_beta=(k_bd * (beta * eg)).astype(BF16),
                    q_dec=(q_bd * eg).astype(BF16), attn=(qk * decay).astype(BF16),
                    k_dec=(k_bd * to_last).astype(BF16), g_state=g_state)

    dirs = ((qf, kf, vf, bgf, of_ref), (qb, kb, vb, bgb, ob_ref))
    chains = [chunk_setup(d, *dirs[d][:4], j * c) for j in range(cps) for d in range(2)]
    zero_b = jnp.zeros((), BF16)
    for lm in level_masks[1:]:
        part = [_mm(ch["t"], jnp.where(lm, ch["a_mat"], zero_b), prec) for ch in chains]
        for ch, pt in zip(chains, part):
            ch["t"] = ch["t"] - _mm(pt, ch["t"], prec).astype(BF16)
    for ch in chains:
        ch["w"] = _mm(ch["t"], ch["v_beta"], prec)
        ch["k_cum"] = _mm(ch["t"], ch["k_beta"], prec)

    states = [s_ref[d] for d in range(2)]
    for step in range(cps):
        sub = [step, cps - 1 - step]
        tms = [chains[sub[d] * 2 + d] for d in range(2)]
        us = [tms[d]["w"] - _mm(tms[d]["k_cum"], states[d], prec) for d in range(2)]
        for d in range(2):
            tm, u, s = tms[d], us[d], states[d]
            o_bd = _mm(tm["q_dec"], s, prec) + _mm(tm["attn"], u, prec)
            states[d] = s * tm["g_state"] + _mm(tm["k_dec"], u, prec, _TN)
            dirs[d][4][sub[d] * c:(sub[d] + 1) * c, :] = functools.reduce(
                jnp.add, [o_bd[h * c:(h + 1) * c] for h in range(B_HEADS)])
    for d in range(2):
        s_ref[d] = states[d]


def _scan_order_maps(n_chunks, n_ctx_chunks):
    fwd = lambda j: (j, 0)
    bwd = lambda j: (jnp.where(j < n_ctx_chunks, n_ctx_chunks - 1 - j,
                               n_chunks - 1 - (j - n_ctx_chunks)), 0)
    return fwd, bwd


def _dn_scan(qn, kn, vn, bg, n_ctx):
    n = qn.shape[0]
    assert B_CHUNK == HEAD_DIM
    c = B_CHUNK * DN_CHUNKS_PER_STEP
    fwd, bwd = _scan_order_maps(n // c, n_ctx // c)
    specs = lambda m: [pl.BlockSpec((c, B_WIDTH), m)] * 3 + [pl.BlockSpec((c, LANES), m)]
    return pl.pallas_call(
        functools.partial(_dn_scan_kernel, prec=DN_PREC, cps=DN_CHUNKS_PER_STEP),
        grid=(n // c,),
        in_specs=specs(fwd) + specs(bwd),
        out_specs=[pl.BlockSpec((c, B_WIDTH), fwd), pl.BlockSpec((c, B_WIDTH), bwd)],
        out_shape=[jax.ShapeDtypeStruct((n, B_WIDTH), F32)] * 2,
        scratch_shapes=[pltpu.VMEM((2, B_WIDTH, B_WIDTH), F32)],
        compiler_params=_cparams("arbitrary"),
        name="dn_scan",
    )(qn, kn, vn, bg, qn, kn, vn, bg)


def _s5_kernel(uf_ref, ub_ref, bre_ref, bim_ref, cre_ref, cimn_ref, ap_ref,
               yf_ref, yb_ref, hre_sc, him_sc, carry_sc, apb_sc, *, prec):
    @pl.when(pl.program_id(0) == 0)
    def _():
        carry_sc[...] = jnp.zeros(carry_sc.shape, F32)
        for d in range(2):
            for ri in range(2):
                for k in range(S5_STEPS):
                    apb_sc[d, ri, k] = jnp.broadcast_to(ap_ref[d, ri, k:k + 1, :],
                                                        (SUBLANES, C_STATES))

    steps, nsub = S5_STEPS, SUBLANES
    ts = steps * nsub
    prow = lax.broadcasted_iota(jnp.int32, (ts, ts), 0)
    pcol = lax.broadcasted_iota(jnp.int32, (ts, ts), 1)
    scan_pos = (prow % nsub) * steps + prow // nsub
    perms = [jnp.where(pcol == scan_pos, 1.0, 0.0).astype(BF16),
             jnp.where(pcol == ts - 1 - scan_pos, 1.0, 0.0).astype(BF16)]
    ndir = 2

    def permute(pm, x, dims, pieces):
        return _mm_pieces(pm, x, pieces, split="b", dims=dims)

    u_perms = [permute(perms[d], u_ref[...], ((1,), (0,)), 1 if prec == 1 else 2)
               for d, u_ref in enumerate((uf_ref, ub_ref))]
    for d in range(ndir):
        hre_sc[d] = _mm(u_perms[d], bre_ref[d], prec)
    for d in range(ndir):
        him_sc[d] = _mm(u_perms[d], bim_ref[d], prec)

    def rows_at(jj):
        return pl.ds(pl.multiple_of(jj * nsub, nsub), nsub)

    def local_scan(jj, carry):
        rows = rows_at(jj)
        out = []
        for d in range(ndir):
            a_re, a_im = apb_sc[d, 0, 0], apb_sc[d, 1, 0]
            h_re, h_im = carry[d]
            n_re = a_re * h_re - a_im * h_im + hre_sc[d, rows, :]
            n_im = a_re * h_im + a_im * h_re + him_sc[d, rows, :]
            hre_sc[d, rows, :] = n_re
            him_sc[d, rows, :] = n_im
            out.append((n_re, n_im))
        return tuple(out)

    zero = jnp.zeros((nsub, C_STATES), F32)
    ends = lax.fori_loop(0, steps, local_scan, ((zero, zero),) * ndir)
    cins = []
    for d in range(ndir):
        as_re = ap_ref[d, 0, steps - 1:steps, :]
        as_im = ap_ref[d, 1, steps - 1:steps, :]
        c_re, c_im = carry_sc[2 * d:2 * d + 1, :], carry_sc[2 * d + 1:2 * d + 2, :]
        cin = []
        for s in range(nsub):
            cin.append((c_re, c_im))
            c_re, c_im = (ends[d][0][s:s + 1] + as_re * c_re - as_im * c_im,
                          ends[d][1][s:s + 1] + as_re * c_im + as_im * c_re)
        carry_sc[2 * d:2 * d + 1, :] = c_re
        carry_sc[2 * d + 1:2 * d + 2, :] = c_im
        cins.append((jnp.concatenate([x[0] for x in cin], axis=0),
                     jnp.concatenate([x[1] for x in cin], axis=0)))

    def fixup(jj, carry):
        rows = rows_at(jj)
        for d in range(ndir):
            p_re = apb_sc[d, 0, jj]
            p_im = apb_sc[d, 1, jj]
            cin_re, cin_im = cins[d]
            hre_sc[d, rows, :] += p_re * cin_re - p_im * cin_im
            him_sc[d, rows, :] += p_re * cin_im + p_im * cin_re
        return carry

    lax.fori_loop(0, steps, fixup, 0)
    y_re = [_mm(hre_sc[d], cre_ref[d], prec) for d in range(ndir)]
    y_im = [_mm(him_sc[d], cimn_ref[d], prec) for d in range(ndir)]
    for d, y_ref in enumerate((yf_ref, yb_ref)):
        y_ref[...] = permute(perms[d], y_re[d] + y_im[d], _TN, 2)


def _s5_params(a_re, a_im, log_dt, b_re, b_im, c_re, c_im):
    dt = jnp.exp(log_dt.astype(F32))[..., None]
    lam_re, lam_im = a_re.astype(F32), a_im.astype(F32)
    k = jnp.arange(1, S5_STEPS + 1, dtype=F32)[None, :, None, None]
    mag = jnp.exp(lam_re[:, None] * dt[:, None] * k)
    ang = lam_im[:, None] * dt[:, None] * k
    ap = jnp.stack([mag * jnp.cos(ang), mag * jnp.sin(ang)], axis=1)
    ap = ap.reshape(2, 2, S5_STEPS, C_STATES)
    ab_re, ab_im = ap[:, 0, 0], ap[:, 1, 0]
    x, y = ab_re.reshape(lam_re.shape) - 1.0, ab_im.reshape(lam_re.shape)
    den = lam_re * lam_re + lam_im * lam_im
    cf_re, cf_im = (x * lam_re + y * lam_im) / den, (y * lam_re - x * lam_im) / den
    bb_re = cf_re[..., None] * b_re - cf_im[..., None] * b_im
    bb_im = cf_re[..., None] * b_im + cf_im[..., None] * b_re
    eye = jnp.eye(C_GROUPS, dtype=F32)[None, :, None, :, None]

    def group_diag(m):
        mt = jnp.swapaxes(m, -1, -2)
        return (mt[:, :, :, None, :] * eye).reshape(2, C_GROUPS * mt.shape[2],
                                                    C_GROUPS * mt.shape[3])

    dense_b = group_diag
    dense_c = group_diag
    return (dense_b(bb_re), dense_b(bb_im), dense_c(c_re.astype(F32)),
            dense_c(-c_im.astype(F32)), ap)


def _s5_scan(uc, params, layer, n_ctx):
    n = uc.shape[0]
    ts = ROW_TILE
    bre, bim, cre, cimn, ap = params
    fwd, bwd = _scan_order_maps(n // ts, n_ctx // ts)
    full = lambda a: pl.BlockSpec((None,) + a.shape[1:],
                                  lambda j: (layer,) + (0,) * (a.ndim - 1),
                                  pipeline_mode=pl.Buffered(1))
    return pl.pallas_call(
        functools.partial(_s5_kernel, prec=S5_PREC),
        grid=(n // ts,),
        in_specs=[pl.BlockSpec((ts, C_WIDTH), fwd), pl.BlockSpec((ts, C_WIDTH), bwd),
                  full(bre), full(bim), full(cre), full(cimn), full(ap)],
        out_specs=[pl.BlockSpec((ts, C_WIDTH), fwd), pl.BlockSpec((ts, C_WIDTH), bwd)],
        out_shape=[jax.ShapeDtypeStruct((n, C_WIDTH), F32)] * 2,
        scratch_shapes=[pltpu.VMEM((2, ts, C_STATES), F32), pltpu.VMEM((2, ts, C_STATES), F32),
                        pltpu.VMEM((SUBLANES, C_STATES), F32),
                        pltpu.VMEM((2, 2, S5_STEPS, SUBLANES, C_STATES), F32)],
        compiler_params=_cparams("arbitrary"),
        name="s5_scan",
    )(uc, uc, bre, bim, cre, cimn, ap)


def _gelu_tanh(x):
    return 0.5 * x * (1.0 + jnp.tanh(math.sqrt(2.0 / math.pi) * (x + 0.044715 * (x * x * x))))


def _out_kernel(c_ref, l_ref, mod_ref, ao_ref, dof_ref, dob_ref, gb_ref, u_ref, yf_ref, yb_ref,
                gc_ref, vec_ref, bd_ref, gluw_ref, wo_ref, lnv_ref, o_ref, *, n_ctx_tiles,
                first_tile, d_model, alpha):
    is_ctx = pl.program_id(0) + first_tile < n_ctx_tiles
    vec = vec_ref[...]
    o = dof_ref[...] + dob_ref[...]
    ms = _mm_pieces(o * o, bd_ref[...], 2)
    dn = o * lax.rsqrt(ms + NORM_EPS) * vec[0:1] * _silu(gb_ref[...])
    z = _gelu_tanh(u_ref[...] * vec[1:2] + yf_ref[...] + yb_ref[...])
    lin = jnp.dot(z.astype(BF16), gluw_ref[...], preferred_element_type=F32) + vec[2:3]
    s5 = z * jax.nn.sigmoid(lin) * _silu(gc_ref[...])
    wo = wo_ref
    proj = (jnp.dot(ao_ref[...], wo[:A_WIDTH, :], preferred_element_type=F32)
            + jnp.dot(dn.astype(BF16), wo[A_WIDTH:A_WIDTH + B_WIDTH, :],
                      preferred_element_type=F32)
            + jnp.dot(s5.astype(BF16), wo[A_WIDTH + B_WIDTH:, :], preferred_element_type=F32))
    gate = _mod_row(mod_ref, is_ctx)[:, 2 * d_model:]
    lnv = lnv_ref[...]
    x = jnp.where(is_ctx, c_ref[...], l_ref[...])
    o_ref[...] = _layer_norm(alpha * x + gate * proj) * lnv[0:1] + lnv[1:2]


def _out_proj(ctx_src, lat_src, lat_off, n, first_tile, mod_l, ao, dof, dob, gb, uc, yf, yb, gc,
              vec, gluw, wo, lnv, n_ctx_tiles, alpha):
    d = lat_src.shape[1]
    tm = ROW_TILE
    bd = _block_diag_const(B_WIDTH, 1.0 / HEAD_DIM)
    row = lambda w: pl.BlockSpec((tm, w), lambda i: (i + first_tile, 0))
    full = lambda a: pl.BlockSpec(a.shape, lambda i: (0,) * a.ndim)
    return pl.pallas_call(
        functools.partial(_out_kernel, n_ctx_tiles=n_ctx_tiles, first_tile=first_tile, d_model=d,
                          alpha=alpha),
        grid=(n // tm - first_tile,),
        in_specs=_row_source_specs(n_ctx_tiles, lat_off, d, first_tile)
        + [full(mod_l), row(A_WIDTH), row(B_WIDTH), row(B_WIDTH), row(B_WIDTH),
           row(C_WIDTH), row(C_WIDTH), row(C_WIDTH), row(C_WIDTH), full(vec), full(bd),
           full(gluw), full(wo), full(lnv)],
        out_specs=pl.BlockSpec((tm, d), lambda i: (i, 0)),
        out_shape=jax.ShapeDtypeStruct((n - first_tile * tm, d), F32),
        compiler_params=_cparams("arbitrary"),
        name="out_proj",
    )(ctx_src, lat_src, mod_l, ao, dof, dob, gb, uc, yf, yb, gc, vec, bd, gluw, wo, lnv)


def _rope_tables(n_lat):
    rows = n_lat // GRID_W
    axis_dim = HEAD_DIM // 2
    inv = ROPE_THETA ** (-jnp.arange(0, axis_dim, 2, dtype=F32) / axis_dim)
    lane = jnp.arange(LANES)
    inv_lane = inv[lane % (axis_dim // 2)]
    row_lane = (lane % HEAD_DIM) < axis_dim
    upper = (lane % axis_dim) >= axis_dim // 2
    ar = jnp.arange(rows, dtype=F32)[:, None] * inv_lane
    ac = jnp.arange(GRID_W, dtype=F32)[:, None] * inv_lane

    def expand(fr, fc):
        return jnp.where(row_lane, fr[:, None, :], fc[None, :, :]).reshape(n_lat, LANES)

    sin_r, sin_c = jnp.sin(ar), jnp.sin(ac)
    cos = expand(jnp.cos(ar), jnp.cos(ac))
    sa = expand(jnp.where(upper, 0.0, -sin_r), jnp.where(upper, 0.0, -sin_c))
    sb = expand(jnp.where(upper, sin_r, 0.0), jnp.where(upper, sin_c, 0.0))
    return cos, sa, sb


def kernel(x, c, ctx, c_ctx, w_mod, b_mod, w_in, attn_q_gain, attn_k_gain, dn_conv_w, dn_A_log, dn_dt_bias, dn_out_gain, s5_A_re, s5_A_im, s5_log_dt, s5_B_re, s5_B_im, s5_C_re, s5_C_im, s5_D, glu_w, glu_b, w_out, ln_g, ln_b):
    batch, n_lat, d = x.shape
    n_ctx = ctx.shape[1]
    depth = w_mod.shape[0]
    assert batch == 1 and n_ctx % ROW_TILE == 0 and n_lat % ROW_TILE == 0 and n_lat % GRID_W == 0
    nct = n_ctx // ROW_TILE
    alpha = (2 * depth) ** 0.25

    n = n_ctx + n_lat
    cv = jnp.zeros((SUBLANES, d), F32).at[0].set(c[0]).at[1].set(c_ctx)
    mod = _modulation(cv, w_mod, b_mod)
    cos, sa, sb = _rope_tables(n_lat)
    lane_tile = lambda g: jnp.tile(g.astype(F32)[None, :], (1, LANES // HEAD_DIM))
    s5_par = jax.vmap(_s5_params)(s5_A_re, s5_A_im, s5_log_dt, s5_B_re, s5_B_im, s5_C_re, s5_C_im)

    ctx_src, lat_src, lat_off = ctx[0], x[0], 0
    for l in range(depth):
        (qz, kp, vt, ga, qn, kn, vn, bg, gb, uc, gc) = _inproj(
            ctx_src, lat_src, lat_off, n, n_ctx, mod[l], _pad_w_in(w_in[l]), (cos, sa, sb),
            lane_tile(attn_q_gain[l]), lane_tile(attn_k_gain[l]), dn_conv_w[l].astype(F32),
            dn_A_log[l], dn_dt_bias[l])
        gain_product = jnp.max(jnp.abs(attn_q_gain[l])) * jnp.max(jnp.abs(attn_k_gain[l]))
        ao = _flash(qz, kp, vt, ga, n_ctx, gain_product)
        dof, dob = _dn_scan(qn, kn, vn, bg, n_ctx)
        yf, yb = _s5_scan(uc, s5_par, l, n_ctx)
        vec = jnp.zeros((SUBLANES, C_WIDTH), F32)
        vec = vec.at[0].set(jnp.tile(dn_out_gain[l].astype(F32), B_HEADS))
        vec = vec.at[1].set(s5_D[l].astype(F32)).at[2].set(glu_b[l].astype(F32))
        lnv = jnp.zeros((SUBLANES, d), F32).at[0].set(ln_g[l]).at[1].set(ln_b[l])
        first_tile = nct if l == depth - 1 else 0
        xall = _out_proj(ctx_src, lat_src, lat_off, n, first_tile, mod[l], ao, dof, dob, gb, uc,
                         yf, yb, gc, vec, glu_w[l].astype(BF16), w_out[l].astype(BF16), lnv, nct,
                         alpha)
        ctx_src, lat_src, lat_off = xall, xall, nct
    return xall[None]
```

```python
import functools
import math

import jax
import jax.numpy as jnp
from jax import lax
from jax.experimental import pallas as pl
from jax.experimental.pallas import tpu as pltpu

F32 = jnp.float32
BF16 = jnp.bfloat16
HIGHEST = lax.Precision.HIGHEST

GRID_W = 64
HEAD_DIM = 64
NORM_EPS = 1e-6
ROPE_THETA = 10000.0
A_Q_HEADS = 8
A_KV_HEADS = 2
A_GROUPS = A_Q_HEADS // A_KV_HEADS
A_WIDTH = A_Q_HEADS * HEAD_DIM
A_KV_WIDTH = A_KV_HEADS * HEAD_DIM
B_HEADS = 4
B_WIDTH = B_HEADS * HEAD_DIM
B_CHUNK = 64
C_WIDTH = 256
C_GROUP = 16
C_GROUPS = C_WIDTH // C_GROUP
C_STATE = 64
C_STATES = C_GROUPS * C_STATE
LANES = 128
SUBLANES = 8
ROW_TILE = 256
S5_STEPS = ROW_TILE // SUBLANES
VMEM_LIMIT = 48 * 1024 * 1024
LOG2E = math.log2(math.e)

DN_PREC = 1
S5_PREC = 1
DN_CHUNKS_PER_STEP = 4


def _cparams(*sem):
    return pltpu.CompilerParams(dimension_semantics=sem, vmem_limit_bytes=VMEM_LIMIT)


def _split(a):
    hi = a.astype(BF16)
    lo = (a - hi.astype(F32)).astype(BF16)
    return hi, lo


def _dg(a, b, dims):
    return lax.dot_general(a, b, (dims, ((), ())), preferred_element_type=F32)


def _mm(a, b, prec, dims=((1,), (0,))):
    if prec == 6:
        return lax.dot_general(a, b, (dims, ((), ())), precision=HIGHEST,
                               preferred_element_type=F32)
    if prec == 1:
        return _dg(a.astype(BF16), b.astype(BF16), dims)
    ah, al = _split(a)
    bh, bl = _split(b)
    return _dg(ah, bh, dims) + (_dg(ah, bl, dims) + _dg(al, bh, dims))


_NT = ((1,), (1,))
_TN = ((0,), (0,))


def _mm_pieces(a, b, pieces, split="a", dims=((1,), (0,))):
    exact = (b if split == "a" else a).astype(BF16)
    out, rest = None, (a if split == "a" else b)
    for _ in range(pieces):
        part = rest.astype(BF16)
        rest = rest - part.astype(F32)
        term = _dg(part, exact, dims) if split == "a" else _dg(exact, part, dims)
        out = term if out is None else out + term
    return out


def _silu(x):
    return x * jax.nn.sigmoid(x)


def _mod_kernel(cv_ref, w_ref, b_ref, o_ref):
    sv = _silu(cv_ref[...])
    o_ref[0] = _mm(sv, w_ref[0], 3) + b_ref[0]


def _modulation(cv, w_mod, b_mod):
    depth, d, d3 = w_mod.shape
    nblk = d3 // d
    return pl.pallas_call(
        _mod_kernel,
        grid=(depth, nblk),
        in_specs=[pl.BlockSpec((SUBLANES, d), lambda l, n: (0, 0)),
                  pl.BlockSpec((1, d, d), lambda l, n: (l, 0, n)),
                  pl.BlockSpec((1, 1, d), lambda l, n: (l, 0, n))],
        out_specs=pl.BlockSpec((1, SUBLANES, d), lambda l, n: (l, 0, n)),
        out_shape=jax.ShapeDtypeStruct((depth, SUBLANES, d3), F32),
        compiler_params=_cparams("arbitrary", "arbitrary"),
        name="modulation",
    )(cv, w_mod, b_mod.reshape(depth, 1, d3))


def _layer_norm(x):
    mu = jnp.mean(x, axis=-1, keepdims=True)
    xc = x - mu
    var = jnp.mean(xc * xc, axis=-1, keepdims=True)
    return xc * lax.rsqrt(var + NORM_EPS)


def _mod_row(mod_ref, is_ctx):
    mod = mod_ref[...]
    return jnp.where(is_ctx, mod[1:2], mod[0:1])


def _column_ranges(widths):
    out, off = {}, 0
    for name, w in widths:
        out[name] = (off, off + w)
        off += w
    return out


_IN_COLS = _column_ranges((("qkva", A_WIDTH + 2 * A_KV_WIDTH), ("ga", A_WIDTH),
                           ("qkvb", 3 * B_WIDTH), ("gb", B_WIDTH), ("bd", LANES),
                           ("uc", C_WIDTH), ("gc", C_WIDTH)))


def _row_source_specs(n_ctx_tiles, lat_off, d, first_tile=0):
    tm = ROW_TILE
    return [pl.BlockSpec((tm, d), lambda i: (jnp.minimum(i + first_tile, n_ctx_tiles - 1), 0)),
            pl.BlockSpec((tm, d),
                         lambda i: (jnp.maximum(i + first_tile - n_ctx_tiles, 0) + lat_off, 0))]


def _attn_prep_tile(q, k, v, cos, sa, sb, gq, gk, bd, qo_ref, ko_ref, vo_ref):
    def norm_rope(y, gain):
        ms = _mm_pieces(y * y, bd, 2)
        yn = y * lax.rsqrt(ms + NORM_EPS) * gain
        return (yn * cos + pltpu.roll(yn, LANES - HEAD_DIM // 4, 1) * sa
                + pltpu.roll(yn, HEAD_DIM // 4, 1) * sb)

    lane = lax.broadcasted_iota(jnp.int32, (1, LANES), 1)
    lower = lane < HEAD_DIM
    for j in range(A_Q_HEADS // 2):
        y = norm_rope(q[:, j * LANES:(j + 1) * LANES], gq) * (HEAD_DIM ** -0.5 * LOG2E)
        y_sw = pltpu.roll(y, HEAD_DIM, 1)
        for half in range(2):
            h = 2 * j + half
            src = y if (h // A_GROUPS) == half else y_sw
            keep = lower if (h // A_GROUPS) == 0 else jnp.logical_not(lower)
            qo_ref[h] = jnp.where(keep, src, 0.0).astype(BF16)
        yield
    ko_ref[...] = norm_rope(k, gk).astype(BF16)
    yield
    vt = v.T
    ones = jnp.ones((HEAD_DIM, vt.shape[1]), F32)
    for h in range(A_KV_HEADS):
        vo_ref[h] = jnp.concatenate([vt[h * HEAD_DIM:(h + 1) * HEAD_DIM], ones],
                                    axis=0).astype(BF16)


def _dn_prep_tile(z, prev_row, next_row, r, w, par, bd, q_o, k_o, v_o, bg_o):
    tm = z.shape[0]
    row = lax.broadcasted_iota(jnp.int32, (tm, 1), 0)

    def conv(cols):
        zc, wc = z[:, cols], w[:, cols]
        z_prev = jnp.where(row == 0, prev_row[:, cols], pltpu.roll(zc, 1, 0))
        z_next = jnp.where(row == tm - 1, next_row[:, cols], pltpu.roll(zc, tm - 1, 0))
        return _silu(z_prev * wc[0:1] + zc * wc[1:2] + z_next * wc[2:3])

    def l2norm(t):
        return t * lax.rsqrt(_mm_pieces(t * t, bd, 2) + NORM_EPS)

    q_o[...] = l2norm(conv(slice(0, B_WIDTH))) * (HEAD_DIM ** -0.5)
    yield
    k_o[...] = l2norm(conv(slice(B_WIDTH, 2 * B_WIDTH)))
    yield
    v_o[...] = conv(slice(2 * B_WIDTH, 3 * B_WIDTH))
    xa = r + par[1:2]
    softplus = jnp.maximum(xa, 0.0) + jnp.log(1.0 + jnp.exp(-jnp.abs(xa)))
    g = -jnp.exp(par[0:1]) * softplus
    lane = lax.broadcasted_iota(jnp.int32, (1, LANES), 1)
    bg_o[...] = jnp.where(lane < 2 * B_HEADS, jax.nn.sigmoid(r), g)


def _inproj_kernel(c_ref, l_ref, cp_ref, cn_ref, lp_ref, ln_ref, mod_ref, w_ref, cos_ref, sa_ref,
                   sb_ref, gq_ref, gk_ref, bd_ref, cw_ref, par_ref, bdb_ref,
                   qo_ref, ko_ref, vo_ref, ga_ref, qn_ref, kn_ref, vn_ref, bg_ref, gb_ref, uc_ref,
                   gc_ref, *, n_ctx_tiles, n_tiles, d_model):
    i = pl.program_id(0)
    is_ctx = i < n_ctx_tiles
    row = _mod_row(mod_ref, is_ctx)
    shift, scale = row[:, :d_model], row[:, d_model:2 * d_model]
    modulate = lambda t: (_layer_norm(t) * (1.0 + scale) + shift).astype(BF16)
    hb = modulate(jnp.where(is_ctx, c_ref[...], l_ref[...]))
    proj = lambda lhs, name: jnp.dot(lhs, w_ref[:, _IN_COLS[name][0]:_IN_COLS[name][1]],
                                     preferred_element_type=F32)
    za = proj(hb, "qkva")
    cos = jnp.where(is_ctx, 1.0, cos_ref[...])
    sa, sb = jnp.where(is_ctx, 0.0, sa_ref[...]), jnp.where(is_ctx, 0.0, sb_ref[...])
    attn_prep = _attn_prep_tile(za[:, :A_WIDTH], za[:, A_WIDTH:A_WIDTH + A_KV_WIDTH],
                                za[:, A_WIDTH + A_KV_WIDTH:], cos, sa, sb, gq_ref[...],
                                gk_ref[...], bd_ref[...], qo_ref, ko_ref, vo_ref)
    zb = proj(hb, "qkvb")
    next(attn_prep, None)
    halo = modulate(jnp.where(is_ctx, jnp.concatenate([cp_ref[...], cn_ref[...]], axis=0),
                              jnp.concatenate([lp_ref[...], ln_ref[...]], axis=0)))
    zh = proj(halo, "qkvb")
    seg_start = jnp.logical_or(i == 0, i == n_ctx_tiles)
    seg_end = jnp.logical_or(i == n_ctx_tiles - 1, i == n_tiles - 1)
    next(attn_prep, None)
    dn_prep = _dn_prep_tile(zb, jnp.where(seg_start, 0.0, zh[SUBLANES - 1:SUBLANES]),
                            jnp.where(seg_end, 0.0, zh[SUBLANES:SUBLANES + 1]),
                            proj(hb, "bd"), cw_ref[...], par_ref[...], bdb_ref[...],
                            qn_ref, kn_ref, vn_ref, bg_ref)
    for name, ref in (("ga", ga_ref), ("gb", gb_ref), ("uc", uc_ref), ("gc", gc_ref)):
        next(attn_prep, None)
        ref[...] = proj(hb, name)
        next(dn_prep, None)
    for rest in (attn_prep, dn_prep):
        for _ in rest:
            pass


def _pad_w_in(w):
    d = w.shape[0]
    n_main = 2 * A_WIDTH + 2 * A_KV_WIDTH + 4 * B_WIDTH
    n_small = 4 * B_HEADS
    return jnp.concatenate(
        [w[:, :n_main], w[:, n_main:n_main + n_small], jnp.zeros((d, LANES - n_small), w.dtype),
         w[:, n_main + n_small:]], axis=1).astype(BF16)


def _block_diag_const(width, value):
    r = jnp.arange(width) // HEAD_DIM
    return jnp.where(r[:, None] == r[None, :], value, 0.0).astype(F32)


def _halo_specs(n_ctx_tiles, lat_off, d, n_ctx_rows, n_lat_rows):
    r8 = ROW_TILE // SUBLANES
    c_last, l_last = n_ctx_rows // SUBLANES - 1, n_lat_rows // SUBLANES - 1
    lo8 = lat_off * r8
    blk = lambda f: pl.BlockSpec((SUBLANES, d), f)
    return [blk(lambda i: (jnp.clip(i * r8 - 1, 0, c_last), 0)),
            blk(lambda i: (jnp.clip((i + 1) * r8, 0, c_last), 0)),
            blk(lambda i: (jnp.clip((i - n_ctx_tiles) * r8 - 1, 0, l_last) + lo8, 0)),
            blk(lambda i: (jnp.clip((i - n_ctx_tiles + 1) * r8, 0, l_last) + lo8, 0))]


def _inproj(ctx_src, lat_src, lat_off, n, n_ctx, mod_l, w_pad, rope, gq, gk, conv_w, a_log,
            dt_bias):
    d = lat_src.shape[1]
    tm = ROW_TILE
    nct = n_ctx // tm
    bd = _block_diag_const(LANES, 1.0 / HEAD_DIM)
    bdb = _block_diag_const(B_WIDTH, 1.0)
    par = jnp.zeros((2, LANES), F32)
    par = par.at[0, 2 * B_HEADS:4 * B_HEADS].set(a_log.reshape(-1))
    par = par.at[1, 2 * B_HEADS:4 * B_HEADS].set(dt_bias.reshape(-1))
    row = lambda w: pl.BlockSpec((tm, w), lambda i: (i, 0))
    lat_row = lambda w: pl.BlockSpec((tm, w), lambda i: (jnp.maximum(i - nct, 0), 0))
    full = lambda a: pl.BlockSpec(a.shape, lambda i: (0,) * a.ndim)
    outs = [("ga", A_WIDTH), ("qn", B_WIDTH), ("kn", B_WIDTH), ("vn", B_WIDTH), ("bg", LANES),
            ("gb", B_WIDTH), ("uc", C_WIDTH), ("gc", C_WIDTH)]
    return pl.pallas_call(
        functools.partial(_inproj_kernel, n_ctx_tiles=nct, n_tiles=n // tm, d_model=d),
        grid=(n // tm,),
        in_specs=_row_source_specs(nct, lat_off, d)
        + _halo_specs(nct, lat_off, d, n_ctx, n - n_ctx)
        + [full(mod_l), full(w_pad), lat_row(LANES), lat_row(LANES), lat_row(LANES), full(gq),
           full(gk), full(bd), full(conv_w), full(par), full(bdb)],
        out_specs=[pl.BlockSpec((A_Q_HEADS, tm, LANES), lambda i: (0, i, 0)),
                   row(LANES),
                   pl.BlockSpec((A_KV_HEADS, LANES, tm), lambda i: (0, 0, i))]
        + [row(w) for _, w in outs],
        out_shape=[jax.ShapeDtypeStruct((A_Q_HEADS, n, LANES), BF16),
                   jax.ShapeDtypeStruct((n, LANES), BF16),
                   jax.ShapeDtypeStruct((A_KV_HEADS, LANES, n), BF16)]
        + [jax.ShapeDtypeStruct((n, w), F32) for _, w in outs],
        compiler_params=_cparams("arbitrary"),
        name="inproj",
    )(ctx_src, lat_src, ctx_src, ctx_src, lat_src, lat_src, mod_l, w_pad, *rope, gq, gk, bd,
      conv_w, par, bdb)


def _flash_kernel(q_ref, k_ref, vt_ref, g_ref, o_ref, m_sc, mblk_sc, alpha_sc, acc_sc, s_sc, p_sc,
                  *, n_ctx, n_ctx_tiles, tc, n_chunks):
    is_ctx = pl.program_id(0) < n_ctx_tiles
    tq = q_ref.shape[1]
    m_sc[...] = jnp.full(m_sc.shape, -jnp.inf, F32)
    acc_sc[...] = jnp.zeros(acc_sc.shape, F32)

    def q_group(g):
        return q_ref[g * A_GROUPS:(g + 1) * A_GROUPS].reshape(A_GROUPS * tq, LANES)

    def scores(g, c0, size):
        s = _dg(k_ref[pl.ds(c0, size), :], q_group(g), _NT)
        s_sc[g, :size, :] = s
        mblk_sc[g] = jnp.max(s, axis=0, keepdims=True)

    def exponentials(g, size):
        m_old = m_sc[g]
        m_new = jnp.maximum(m_old, mblk_sc[g])
        alpha_sc[g] = jnp.exp2(m_old - m_new)
        p_sc[g, :size, :] = jnp.exp2(s_sc[g, :size, :] - m_new).astype(BF16)
        m_sc[g] = m_new

    def weighted_values(g, c0, size):
        acc_sc[g] = alpha_sc[g] * acc_sc[g] + jnp.dot(
            vt_ref[g, :, pl.ds(c0, size)], p_sc[g, :size, :], preferred_element_type=F32)

    @pl.when(is_ctx)
    def _():
        for g in range(A_KV_HEADS):
            scores(g, 0, n_ctx)
            exponentials(g, n_ctx)
            weighted_values(g, 0, n_ctx)

    @pl.when(jnp.logical_not(is_ctx))
    def _():
        first, second = 0, A_KV_HEADS - 1
        p_sc[second] = jnp.zeros(p_sc.shape[1:], BF16)
        alpha_sc[second] = jnp.ones(alpha_sc.shape[1:], F32)
        scores(first, 0, tc)

        def body(c, carry):
            c0 = pl.multiple_of(c * tc, tc)
            c_prev = pl.multiple_of(jnp.maximum(c - 1, 0) * tc, tc)
            c_next = pl.multiple_of(jnp.minimum(c + 1, n_chunks - 1) * tc, tc)
            scores(second, c0, tc)
            exponentials(first, tc)
            weighted_values(second, c_prev, tc)
            scores(first, c_next, tc)
            exponentials(second, tc)
            weighted_values(first, c0, tc)
            return carry

        lax.fori_loop(0, n_chunks, body, 0)
        weighted_values(second, (n_chunks - 1) * tc, tc)

    _flash_finalize(acc_sc, g_ref, o_ref, tq)


def _flash_finalize(acc_sc, g_ref, o_ref, tq):
    heads = []
    for h in range(A_Q_HEADS):
        g, r = divmod(h, A_GROUPS)
        acc = acc_sc[g, :, r * tq:(r + 1) * tq]
        heads.append((acc[:HEAD_DIM] / acc[HEAD_DIM:HEAD_DIM + 1]).T)
    o_ref[...] = (jnp.concatenate(heads, axis=1) * _silu(g_ref[...])).astype(o_ref.dtype)


def _flash_bounded_kernel(u_ref, q_ref, k_ref, vt_ref, g_ref, o_ref, acc_sc, p_sc, *, n_ctx,
                          n_ctx_tiles, tc, n_chunks):
    is_ctx = pl.program_id(0) < n_ctx_tiles
    tq = q_ref.shape[1]
    acc_sc[...] = jnp.zeros(acc_sc.shape, F32)
    bound = u_ref[0]

    def q_group(g):
        return q_ref[g * A_GROUPS:(g + 1) * A_GROUPS].reshape(A_GROUPS * tq, LANES)

    def probabilities(g, c0, size):
        s = _dg(k_ref[pl.ds(c0, size), :], q_group(g), _NT)
        p_sc[g, :size, :] = jnp.exp2(s - bound).astype(BF16)

    def weighted_values(g, c0, size):
        acc_sc[g] += jnp.dot(vt_ref[g, :, pl.ds(c0, size)], p_sc[g, :size, :],
                             preferred_element_type=F32)

    @pl.when(is_ctx)
    def _():
        for g in range(A_KV_HEADS):
            probabilities(g, 0, n_ctx)
            weighted_values(g, 0, n_ctx)

    @pl.when(jnp.logical_not(is_ctx))
    def _():
        first, second = 0, A_KV_HEADS - 1
        probabilities(first, 0, tc)

        def body(c, carry):
            c0 = pl.multiple_of(c * tc, tc)
            probabilities(second, c0, tc)
            weighted_values(first, c0, tc)
            probabilities(first, pl.multiple_of(c0 + tc, tc), tc)
            weighted_values(second, c0, tc)
            return carry

        lax.fori_loop(0, n_chunks - 1, body, 0)
        last = (n_chunks - 1) * tc
        probabilities(second, last, tc)
        weighted_values(first, last, tc)
        weighted_values(second, last, tc)

    _flash_finalize(acc_sc, g_ref, o_ref, tq)


def _key_chunk(n):
    for tc in (1280, 1024, 640, 512, 256, 128):
        if n % tc == 0:
            return tc
    raise ValueError(f"no key chunk for {n} rows")


_SCORE_BOUND_PER_GAIN = 1.01 * HEAD_DIM * HEAD_DIM ** -0.5 * LOG2E
_MAX_FIXED_REFERENCE = 40.0


def _flash(qz, k, vt, ga, n_ctx, gain_product):
    n = k.shape[0]
    tq = ROW_TILE
    tc = _key_chunk(n)
    gq = A_GROUPS * tq
    assert A_KV_HEADS == 2 and n_ctx <= tc
    full = lambda a: pl.BlockSpec(a.shape, lambda i: (0,) * a.ndim,
                                  pipeline_mode=pl.Buffered(1))
    statics = dict(n_ctx=n_ctx, n_ctx_tiles=n_ctx // tq, tc=tc, n_chunks=n // tc)
    common = dict(
        grid=(n // tq,),
        out_specs=pl.BlockSpec((tq, A_WIDTH), lambda i: (i, 0)),
        out_shape=jax.ShapeDtypeStruct((n, A_WIDTH), BF16),
        compiler_params=_cparams("arbitrary"))
    in_specs = [pl.BlockSpec((A_Q_HEADS, tq, LANES), lambda i: (0, i, 0)), full(k), full(vt),
                pl.BlockSpec((tq, A_WIDTH), lambda i: (i, 0))]
    acc_shape = pltpu.VMEM((A_KV_HEADS, LANES, gq), F32)
    p_shape = pltpu.VMEM((A_KV_HEADS, tc, gq), BF16)

    def online(bound):
        del bound
        return pl.pallas_call(
            functools.partial(_flash_kernel, **statics), in_specs=in_specs,
            scratch_shapes=[pltpu.VMEM((A_KV_HEADS, 1, gq), F32),
                            pltpu.VMEM((A_KV_HEADS, 1, gq), F32),
                            pltpu.VMEM((A_KV_HEADS, 1, gq), F32),
                            acc_shape,
                            pltpu.VMEM((A_KV_HEADS, tc, gq), F32),
                            p_shape],
            name="flash_attn_online", **common)(qz, k, vt, ga)

    def bounded(bound):
        return pl.pallas_call(
            functools.partial(_flash_bounded_kernel, **statics),
            in_specs=[pl.BlockSpec(memory_space=pltpu.SMEM)] + in_specs,
            scratch_shapes=[acc_shape, p_shape],
            name="flash_attn_bounded", **common)(bound, qz, k, vt, ga)

    bound = (_SCORE_BOUND_PER_GAIN * gain_product).astype(F32).reshape(1)
    return lax.cond(bound[0] <= _MAX_FIXED_REFERENCE, bounded, online, bound)


def _dn_scan_kernel(qf, kf, vf, bgf, qb, kb, vb, bgb, of_ref, ob_ref, s_ref, *, prec, cps):
    @pl.when(pl.program_id(0) == 0)
    def _():
        s_ref[...] = jnp.zeros(s_ref.shape, F32)

    c, hc = B_CHUNK, B_HEADS * B_CHUNK
    row = lax.broadcasted_iota(jnp.int32, (hc, hc), 0)
    col = lax.broadcasted_iota(jnp.int32, (hc, hc), 1)
    same_head = (row // c) == (col // c)
    eye = (row == col).astype(F32)
    level_masks = [(row // 2) == (col // 2)]
    bs = 2
    while bs < c:
        level_masks.append(jnp.logical_and((row // (2 * bs)) == (col // (2 * bs)),
                                           (row // bs) != (col // bs)))
        bs *= 2
    rc, cc = lax.broadcasted_iota(jnp.int32, (c, c), 0), lax.broadcasted_iota(jnp.int32, (c, c), 1)

    def heads_bd(x):
        return jnp.where(same_head, jnp.concatenate([x] * B_HEADS, axis=0), 0.0)

    def rows_of(slab, lane0):
        return jnp.concatenate(
            [jnp.broadcast_to(slab[:, lane0 + h:lane0 + h + 1], (c, hc)) for h in range(B_HEADS)],
            axis=0)

    def chunk_setup(d, q_ref, k_ref, v_ref, bg_ref, r0):
        rows = slice(r0, r0 + c)
        incl = (col <= row) if d == 0 else (col >= row)
        strict = (col < row) if d == 0 else (col > row)
        incl_c = (cc <= rc) if d == 0 else (cc >= rc)
        last = c - 1 if d == 0 else 0
        lb = d * B_HEADS
        lg = 2 * B_HEADS + lb
        bg = bg_ref[rows, :]
        gcum = _mm_pieces(incl_c.astype(F32), bg, 3, split="b")
        gcum_t = gcum.T
        g_row = jnp.concatenate([gcum_t[lg + h:lg + h + 1, :] for h in range(B_HEADS)], axis=1)
        g_last = gcum[last:last + 1, :]
        decay = jnp.where(jnp.logical_and(same_head, incl),
                          jnp.exp(rows_of(gcum, lg) - g_row), 0.0)
        beta = rows_of(bg, lb)
        eg = rows_of(jnp.exp(gcum), lg)
        to_last = rows_of(jnp.exp(g_last - gcum), lg)
        q_bd = heads_bd(q_ref[rows, :])
        k_bd = heads_bd(k_ref[rows, :])
        v_bd = heads_bd(v_ref[rows, :])
        k_b = k_bd.astype(BF16)
        kk = _mm(k_b, k_b, prec, _NT)
        qk = _mm(q_bd, k_b, prec, _NT)
        a_mat = jnp.where(jnp.logical_and(same_head, strict), kk * beta * decay, 0.0)
        g_state = jnp.concatenate(
            [jnp.broadcast_to(jnp.exp(g_last[:, lg + h:lg + h + 1]), (c, hc))
             for h in range(B_HEADS)], axis=0)
        return dict(a_mat=a_mat.astype(BF16),
                    t=(eye - jnp.where(level_masks[0], a_mat, 0.0)).astype(BF16),
                    v_beta=(v_bd * beta).astype(BF16), k_beta=(k_bd * (beta * eg)).astype(BF16),
                    q_dec=(q_bd * eg).astype(BF16), attn=(qk * decay).astype(BF16),
                    k_dec=(k_bd * to_last).astype(BF16), g_state=g_state)

    dirs = ((qf, kf, vf, bgf, of_ref), (qb, kb, vb, bgb, ob_ref))
    chains = [chunk_setup(d, *dirs[d][:4], j * c) for j in range(cps) for d in range(2)]
    zero_b = jnp.zeros((), BF16)
    for lm in level_masks[1:]:
        part = [_mm(ch["t"], jnp.where(lm, ch["a_mat"], zero_b), prec) for ch in chains]
        for ch, pt in zip(chains, part):
            ch["t"] = ch["t"] - _mm(pt, ch["t"], prec).astype(BF16)
    for ch in chains:
        ch["w"] = _mm(ch["t"], ch["v_beta"], prec)
        ch["k_cum"] = _mm(ch["t"], ch["k_beta"], prec)

    states = [s_ref[d] for d in range(2)]
    for step in range(cps):
        sub = [step, cps - 1 - step]
        tms = [chains[sub[d] * 2 + d] for d in range(2)]
        us = [tms[d]["w"] - _mm(tms[d]["k_cum"], states[d], prec) for d in range(2)]
        for d in range(2):
            tm, u, s = tms[d], us[d], states[d]
            o_bd = _mm(tm["q_dec"], s, prec) + _mm(tm["attn"], u, prec)
            states[d] = s * tm["g_state"] + _mm(tm["k_dec"], u, prec, _TN)
            dirs[d][4][sub[d] * c:(sub[d] + 1) * c, :] = functools.reduce(
                jnp.add, [o_bd[h * c:(h + 1) * c] for h in range(B_HEADS)])
    for d in range(2):
        s_ref[d] = states[d]


def _scan_order_maps(n_chunks, n_ctx_chunks):
    fwd = lambda j: (j, 0)
    bwd = lambda j: (jnp.where(j < n_ctx_chunks, n_ctx_chunks - 1 - j,
                               n_chunks - 1 - (j - n_ctx_chunks)), 0)
    return fwd, bwd


def _dn_scan(qn, kn, vn, bg, n_ctx):
    n = qn.shape[0]
    assert B_CHUNK == HEAD_DIM
    c = B_CHUNK * DN_CHUNKS_PER_STEP
    fwd, bwd = _scan_order_maps(n // c, n_ctx // c)
    specs = lambda m: [pl.BlockSpec((c, B_WIDTH), m)] * 3 + [pl.BlockSpec((c, LANES), m)]
    return pl.pallas_call(
        functools.partial(_dn_scan_kernel, prec=DN_PREC, cps=DN_CHUNKS_PER_STEP),
        grid=(n // c,),
        in_specs=specs(fwd) + specs(bwd),
        out_specs=[pl.BlockSpec((c, B_WIDTH), fwd), pl.BlockSpec((c, B_WIDTH), bwd)],
        out_shape=[jax.ShapeDtypeStruct((n, B_WIDTH), F32)] * 2,
        scratch_shapes=[pltpu.VMEM((2, B_WIDTH, B_WIDTH), F32)],
        compiler_params=_cparams("arbitrary"),
        name="dn_scan",
    )(qn, kn, vn, bg, qn, kn, vn, bg)


def _s5_kernel(uf_ref, ub_ref, b_ref, c_ref, ap_ref,
               yf_ref, yb_ref, hre_sc, him_sc, carry_sc, apb_sc, *, prec):
    @pl.when(pl.program_id(0) == 0)
    def _():
        carry_sc[...] = jnp.zeros(carry_sc.shape, F32)
        for d in range(2):
            for ri in range(2):
                for k in range(S5_STEPS):
                    apb_sc[d, ri, k] = jnp.broadcast_to(ap_ref[d, ri, k:k + 1, :],
                                                        (SUBLANES, C_STATES))

    steps, nsub = S5_STEPS, SUBLANES
    ts = steps * nsub
    prow = lax.broadcasted_iota(jnp.int32, (ts, ts), 0)
    pcol = lax.broadcasted_iota(jnp.int32, (ts, ts), 1)
    scan_pos = (prow % nsub) * steps + prow // nsub
    perms = [jnp.where(pcol == scan_pos, 1.0, 0.0).astype(BF16),
             jnp.where(pcol == ts - 1 - scan_pos, 1.0, 0.0).astype(BF16)]
    ndir = 2

    def permute(pm, x, dims, pieces):
        return _mm_pieces(pm, x, pieces, split="b", dims=dims)

    u_perms = [permute(perms[d], u_ref[...], ((1,), (0,)), 1 if prec == 1 else 2)
               for d, u_ref in enumerate((uf_ref, ub_ref))]
    bu = [_mm(u_perms[d], b_ref[d], prec) for d in range(ndir)]
    for d in range(ndir):
        hre_sc[d] = bu[d][:, :C_STATES]
        him_sc[d] = bu[d][:, C_STATES:]

    def rows_at(jj):
        return pl.ds(pl.multiple_of(jj * nsub, nsub), nsub)

    def local_scan(jj, carry):
        rows = rows_at(jj)
        out = []
        for d in range(ndir):
            a_re, a_im = apb_sc[d, 0, 0], apb_sc[d, 1, 0]
            h_re, h_im = carry[d]
            n_re = a_re * h_re - a_im * h_im + hre_sc[d, rows, :]
            n_im = a_re * h_im + a_im * h_re + him_sc[d, rows, :]
            hre_sc[d, rows, :] = n_re
            him_sc[d, rows, :] = n_im
            out.append((n_re, n_im))
        return tuple(out)

    zero = jnp.zeros((nsub, C_STATES), F32)
    ends = lax.fori_loop(0, steps, local_scan, ((zero, zero),) * ndir)
    cins = []
    for d in range(ndir):
        as_re = ap_ref[d, 0, steps - 1:steps, :]
        as_im = ap_ref[d, 1, steps - 1:steps, :]
        c_re, c_im = carry_sc[2 * d:2 * d + 1, :], carry_sc[2 * d + 1:2 * d + 2, :]
        cin = []
        for s in range(nsub):
            cin.append((c_re, c_im))
            c_re, c_im = (ends[d][0][s:s + 1] + as_re * c_re - as_im * c_im,
                          ends[d][1][s:s + 1] + as_re * c_im + as_im * c_re)
        carry_sc[2 * d:2 * d + 1, :] = c_re
        carry_sc[2 * d + 1:2 * d + 2, :] = c_im
        cins.append((jnp.concatenate([x[0] for x in cin], axis=0),
                     jnp.concatenate([x[1] for x in cin], axis=0)))

    def fixup(jj, carry):
        rows = rows_at(jj)
        for d in range(ndir):
            p_re = apb_sc[d, 0, jj]
            p_im = apb_sc[d, 1, jj]
            cin_re, cin_im = cins[d]
            hre_sc[d, rows, :] += p_re * cin_re - p_im * cin_im
            him_sc[d, rows, :] += p_re * cin_im + p_im * cin_re
        return carry

    lax.fori_loop(0, steps, fixup, 0)
    ys = [_mm(jnp.concatenate([hre_sc[d], him_sc[d]], axis=1), c_ref[d], prec)
          for d in range(ndir)]
    for d, y_ref in enumerate((yf_ref, yb_ref)):
        y_ref[...] = permute(perms[d], ys[d], _TN, 2)


def _s5_params(a_re, a_im, log_dt, b_re, b_im, c_re, c_im):
    dt = jnp.exp(log_dt.astype(F32))[..., None]
    lam_re, lam_im = a_re.astype(F32), a_im.astype(F32)
    k = jnp.arange(1, S5_STEPS + 1, dtype=F32)[None, :, None, None]
    mag = jnp.exp(lam_re[:, None] * dt[:, None] * k)
    ang = lam_im[:, None] * dt[:, None] * k
    ap = jnp.stack([mag * jnp.cos(ang), mag * jnp.sin(ang)], axis=1)
    ap = ap.reshape(2, 2, S5_STEPS, C_STATES)
    ab_re, ab_im = ap[:, 0, 0], ap[:, 1, 0]
    x, y = ab_re.reshape(lam_re.shape) - 1.0, ab_im.reshape(lam_re.shape)
    den = lam_re * lam_re + lam_im * lam_im
    cf_re, cf_im = (x * lam_re + y * lam_im) / den, (y * lam_re - x * lam_im) / den
    bb_re = cf_re[..., None] * b_re - cf_im[..., None] * b_im
    bb_im = cf_re[..., None] * b_im + cf_im[..., None] * b_re
    eye = jnp.eye(C_GROUPS, dtype=F32)[None, :, None, :, None]

    def group_diag(m):
        mt = jnp.swapaxes(m, -1, -2)
        return (mt[:, :, :, None, :] * eye).reshape(2, C_GROUPS * mt.shape[2],
                                                    C_GROUPS * mt.shape[3])

    dense_b = group_diag
    dense_c = group_diag
    return (jnp.concatenate([dense_b(bb_re), dense_b(bb_im)], axis=-1),
            jnp.concatenate([dense_c(c_re.astype(F32)), dense_c(-c_im.astype(F32))], axis=-2), ap)


def _s5_scan(uc, params, layer, n_ctx):
    n = uc.shape[0]
    ts = ROW_TILE
    b_cat, c_cat, ap = params
    fwd, bwd = _scan_order_maps(n // ts, n_ctx // ts)
    full = lambda a: pl.BlockSpec((None,) + a.shape[1:],
                                  lambda j: (layer,) + (0,) * (a.ndim - 1),
                                  pipeline_mode=pl.Buffered(1))
    return pl.pallas_call(
        functools.partial(_s5_kernel, prec=S5_PREC),
        grid=(n // ts,),
        in_specs=[pl.BlockSpec((ts, C_WIDTH), fwd), pl.BlockSpec((ts, C_WIDTH), bwd),
                  full(b_cat), full(c_cat), full(ap)],
        out_specs=[pl.BlockSpec((ts, C_WIDTH), fwd), pl.BlockSpec((ts, C_WIDTH), bwd)],
        out_shape=[jax.ShapeDtypeStruct((n, C_WIDTH), F32)] * 2,
        scratch_shapes=[pltpu.VMEM((2, ts, C_STATES), F32), pltpu.VMEM((2, ts, C_STATES), F32),
                        pltpu.VMEM((SUBLANES, C_STATES), F32),
                        pltpu.VMEM((2, 2, S5_STEPS, SUBLANES, C_STATES), F32)],
        compiler_params=_cparams("arbitrary"),
        name="s5_scan",
    )(uc, uc, b_cat, c_cat, ap)


def _gelu_tanh(x):
    return 0.5 * x * (1.0 + jnp.tanh(math.sqrt(2.0 / math.pi) * (x + 0.044715 * (x * x * x))))


def _out_kernel(c_ref, l_ref, mod_ref, ao_ref, dof_ref, dob_ref, gb_ref, u_ref, yf_ref, yb_ref,
                gc_ref, vec_ref, bd_ref, gluw_ref, wo_ref, lnv_ref, o_ref, *, n_ctx_tiles,
                first_tile, d_model, alpha):
    is_ctx = pl.program_id(0) + first_tile < n_ctx_tiles
    vec = vec_ref[...]
    o = dof_ref[...] + dob_ref[...]
    ms = _mm_pieces(o * o, bd_ref[...], 2)
    dn = o * lax.rsqrt(ms + NORM_EPS) * vec[0:1] * _silu(gb_ref[...])
    z = _gelu_tanh(u_ref[...] * vec[1:2] + yf_ref[...] + yb_ref[...])
    lin = jnp.dot(z.astype(BF16), gluw_ref[...], preferred_element_type=F32) + vec[2:3]
    s5 = z * jax.nn.sigmoid(lin) * _silu(gc_ref[...])
    wo = wo_ref
    proj = (jnp.dot(ao_ref[...], wo[:A_WIDTH, :], preferred_element_type=F32)
            + jnp.dot(dn.astype(BF16), wo[A_WIDTH:A_WIDTH + B_WIDTH, :],
                      preferred_element_type=F32)
            + jnp.dot(s5.astype(BF16), wo[A_WIDTH + B_WIDTH:, :], preferred_element_type=F32))
    gate = _mod_row(mod_ref, is_ctx)[:, 2 * d_model:]
    lnv = lnv_ref[...]
    x = jnp.where(is_ctx, c_ref[...], l_ref[...])
    o_ref[...] = _layer_norm(alpha * x + gate * proj) * lnv[0:1] + lnv[1:2]


def _out_proj(ctx_src, lat_src, lat_off, n, first_tile, mod_l, ao, dof, dob, gb, uc, yf, yb, gc,
              vec, gluw, wo, lnv, n_ctx_tiles, alpha):
    d = lat_src.shape[1]
    tm = ROW_TILE
    bd = _block_diag_const(B_WIDTH, 1.0 / HEAD_DIM)
    row = lambda w: pl.BlockSpec((tm, w), lambda i: (i + first_tile, 0))
    full = lambda a: pl.BlockSpec(a.shape, lambda i: (0,) * a.ndim)
    return pl.pallas_call(
        functools.partial(_out_kernel, n_ctx_tiles=n_ctx_tiles, first_tile=first_tile, d_model=d,
                          alpha=alpha),
        grid=(n // tm - first_tile,),
        in_specs=_row_source_specs(n_ctx_tiles, lat_off, d, first_tile)
        + [full(mod_l), row(A_WIDTH), row(B_WIDTH), row(B_WIDTH), row(B_WIDTH),
           row(C_WIDTH), row(C_WIDTH), row(C_WIDTH), row(C_WIDTH), full(vec), full(bd),
           full(gluw), full(wo), full(lnv)],
        out_specs=pl.BlockSpec((tm, d), lambda i: (i, 0)),
        out_shape=jax.ShapeDtypeStruct((n - first_tile * tm, d), F32),
        compiler_params=_cparams("arbitrary"),
        name="out_proj",
    )(ctx_src, lat_src, mod_l, ao, dof, dob, gb, uc, yf, yb, gc, vec, bd, gluw, wo, lnv)


def _rope_tables(n_lat):
    rows = n_lat // GRID_W
    axis_dim = HEAD_DIM // 2
    inv = ROPE_THETA ** (-jnp.arange(0, axis_dim, 2, dtype=F32) / axis_dim)
    lane = jnp.arange(LANES)
    inv_lane = inv[lane % (axis_dim // 2)]
    row_lane = (lane % HEAD_DIM) < axis_dim
    upper = (lane % axis_dim) >= axis_dim // 2
    ar = jnp.arange(rows, dtype=F32)[:, None] * inv_lane
    ac = jnp.arange(GRID_W, dtype=F32)[:, None] * inv_lane

    def expand(fr, fc):
        return jnp.where(row_lane, fr[:, None, :], fc[None, :, :]).reshape(n_lat, LANES)

    sin_r, sin_c = jnp.sin(ar), jnp.sin(ac)
    cos = expand(jnp.cos(ar), jnp.cos(ac))
    sa = expand(jnp.where(upper, 0.0, -sin_r), jnp.where(upper, 0.0, -sin_c))
    sb = expand(jnp.where(upper, sin_r, 0.0), jnp.where(upper, sin_c, 0.0))
    return cos, sa, sb


def kernel(x, c, ctx, c_ctx, w_mod, b_mod, w_in, attn_q_gain, attn_k_gain, dn_conv_w, dn_A_log, dn_dt_bias, dn_out_gain, s5_A_re, s5_A_im, s5_log_dt, s5_B_re, s5_B_im, s5_C_re, s5_C_im, s5_D, glu_w, glu_b, w_out, ln_g, ln_b):
    batch, n_lat, d = x.shape
    n_ctx = ctx.shape[1]
    depth = w_mod.shape[0]
    assert batch == 1 and n_ctx % ROW_TILE == 0 and n_lat % ROW_TILE == 0 and n_lat % GRID_W == 0
    nct = n_ctx // ROW_TILE
    alpha = (2 * depth) ** 0.25

    n = n_ctx + n_lat
    cv = jnp.zeros((SUBLANES, d), F32).at[0].set(c[0]).at[1].set(c_ctx)
    mod = _modulation(cv, w_mod, b_mod)
    cos, sa, sb = _rope_tables(n_lat)
    lane_tile = lambda g: jnp.tile(g.astype(F32)[None, :], (1, LANES // HEAD_DIM))
    s5_par = jax.vmap(_s5_params)(s5_A_re, s5_A_im, s5_log_dt, s5_B_re, s5_B_im, s5_C_re, s5_C_im)

    ctx_src, lat_src, lat_off = ctx[0], x[0], 0
    for l in range(depth):
        (qz, kp, vt, ga, qn, kn, vn, bg, gb, uc, gc) = _inproj(
            ctx_src, lat_src, lat_off, n, n_ctx, mod[l], _pad_w_in(w_in[l]), (cos, sa, sb),
            lane_tile(attn_q_gain[l]), lane_tile(attn_k_gain[l]), dn_conv_w[l].astype(F32),
            dn_A_log[l], dn_dt_bias[l])
        gain_product = jnp.max(jnp.abs(attn_q_gain[l])) * jnp.max(jnp.abs(attn_k_gain[l]))
        ao = _flash(qz, kp, vt, ga, n_ctx, gain_product)
        dof, dob = _dn_scan(qn, kn, vn, bg, n_ctx)
        yf, yb = _s5_scan(uc, s5_par, l, n_ctx)
        vec = jnp.zeros((SUBLANES, C_WIDTH), F32)
        vec = vec.at[0].set(jnp.tile(dn_out_gain[l].astype(F32), B_HEADS))
        vec = vec.at[1].set(s5_D[l].astype(F32)).at[2].set(glu_b[l].astype(F32))
        lnv = jnp.zeros((SUBLANES, d), F32).at[0].set(ln_g[l]).at[1].set(ln_b[l])
        first_tile = nct if l == depth - 1 else 0
        xall = _out_proj(ctx_src, lat_src, lat_off, n, first_tile, mod[l], ao, dof, dob, gb, uc,
                         yf, yb, gc, vec, glu_w[l].astype(BF16), w_out[l].astype(BF16), lnv, nct,
                         alpha)
        ctx_src, lat_src, lat_off = xall, xall, nct
    return xall[None]
```
